```python
import math
import jax, jax.numpy as jnp
from jax import lax
import numpy as np

D_MODEL = 1024
BATCH = 4
SEQ = 4096
DEPTH = 1

CHUNK = 64
Q_BLOCK = 128

DSA_WIDTH = D_MODEL // 2
DSA_HEAD_DIM = 64
DSA_HEADS = DSA_WIDTH // DSA_HEAD_DIM
IDX_HEADS = 8
IDX_DIM = 64
TOPK_MAX = 256

HG_WIDTH = D_MODEL - DSA_WIDTH
HG_DK = 128
HG_HEADS = HG_WIDTH // HG_DK
HG_DV = HG_WIDTH // HG_HEADS

D_FF = ((8 * D_MODEL // 3 + 255) // 256) * 256

ALPHA = (2.0 * DEPTH) ** 0.25
BETA = (8.0 * DEPTH) ** -0.25
LN_EPS = 1e-5
RMS_EPS = 1e-6

SPLIT_SIZES = (DSA_WIDTH, DSA_WIDTH, DSA_WIDTH, IDX_HEADS * IDX_DIM, IDX_DIM, IDX_HEADS,
               HG_HEADS * HG_DK, HG_HEADS * HG_DK, HG_WIDTH, HG_WIDTH)
D_PROJ = sum(SPLIT_SIZES)

kernel_name = "hybrid_dsa_hgrn2_deepnorm_block"


def _split_points():
    pts, acc = [], 0
    for s in SPLIT_SIZES[:-1]:
        acc += s
        pts.append(acc)
    return pts


def _layer_norm(x, g, b):
    xf = x.astype(jnp.float32)
    mu = jnp.mean(xf, axis=-1, keepdims=True)
    var = jnp.mean(jnp.square(xf - mu), axis=-1, keepdims=True)
    y = (xf - mu) * lax.rsqrt(var + LN_EPS) * g.astype(jnp.float32) + b.astype(jnp.float32)
    return y.astype(x.dtype)


def _alibi_slopes(n):
    return 2.0 ** (-8.0 * jnp.arange(1, n + 1, dtype=jnp.float32) / n)


def _dsa_attention(q, k, v, iq, ik, iw):
    B, S = q.shape[0], q.shape[1]
    topk = min(TOPK_MAX, S // 4)
    n_blk = S // Q_BLOCK
    slopes = _alibi_slopes(DSA_HEADS)
    key_pos = jnp.arange(S, dtype=jnp.int32)
    scale = DSA_HEAD_DIM ** -0.5
    idx_scale = (IDX_DIM ** -0.5) * (IDX_HEADS ** -0.5)
    gather = jax.vmap(lambda t, i: t[i])

    def block(bi):
        start = bi * Q_BLOCK
        qb = lax.dynamic_slice_in_dim(q, start, Q_BLOCK, axis=1)
        iqb = lax.dynamic_slice_in_dim(iq, start, Q_BLOCK, axis=1)
        iwb = lax.dynamic_slice_in_dim(iw, start, Q_BLOCK, axis=1)
        qpos = start + jnp.arange(Q_BLOCK, dtype=jnp.int32)
        limit = (qpos // CHUNK + 1) * CHUNK
        adm = key_pos[None, :] < limit[:, None]
        rel = jax.nn.relu(jnp.einsum('bqhd,bsd->bqhs', iqb, ik).astype(jnp.float32))
        score = jnp.einsum('bqh,bqhs->bqs', iwb.astype(jnp.float32), rel) * idx_scale
        score = jnp.where(adm[None], score, -jnp.inf)
        _, sel = lax.top_k(score, topk)
        kg = gather(k, sel)
        vg = gather(v, sel)
        logits = jnp.einsum('bqhd,bqkhd->bhqk', qb, kg).astype(jnp.float32) * scale
        dist = jnp.abs(qpos[None, :, None] - sel).astype(jnp.float32)
        logits = logits - slopes[None, :, None, None] * dist[:, None]
        valid = sel < limit[None, :, None]
        logits = jnp.where(valid[:, None], logits, -jnp.inf)
        p = jax.nn.softmax(logits, axis=-1).astype(v.dtype)
        return jnp.einsum('bhqk,bqkhd->bqhd', p, vg)

    out = lax.map(block, jnp.arange(n_blk))
    return out.transpose(1, 0, 2, 3, 4).reshape(B, S, DSA_HEADS * DSA_HEAD_DIM)


def _hgrn2(q, f_logit, inp, lb):
    f32 = jnp.float32
    B, S, H, Dk = q.shape
    Dv = inp.shape[-1]
    nc = S // CHUNK
    f = lb + (1.0 - lb) * jax.nn.sigmoid(f_logit.astype(f32))
    logf = jnp.log(f)
    kk = 1.0 - f

    def to_chunks(t):
        return t.reshape(B, nc, CHUNK, t.shape[2], t.shape[3]).transpose(1, 0, 3, 2, 4)

    tri = jnp.tril(jnp.ones((CHUNK, CHUNK), dtype=bool))

    def step(state, xs):
        qc, lfc, kc, vc = xs
        G = jnp.cumsum(lfc, axis=2)
        diff = G[:, :, :, None, :] - G[:, :, None, :, :]
        decay = jnp.exp(jnp.where(tri[:, :, None], diff, -jnp.inf))
        A = jnp.einsum('bhtk,bhtsk,bhsk->bhts', qc, decay, kc)
        o = jnp.einsum('bhts,bhsv->bhtv', A, vc) + \
            jnp.einsum('bhtk,bhkv->bhtv', qc * jnp.exp(G), state)
        G_last = G[:, :, -1:, :]
        state = jnp.exp(G_last[:, :, 0])[..., None] * state + \
            jnp.einsum('bhsk,bhsv->bhkv', kc * jnp.exp(G_last - G), vc)
        return state, o

    s0 = jnp.zeros((B, H, Dk, Dv), f32)
    _, o = lax.scan(step, s0, (to_chunks(q.astype(f32)), to_chunks(logf),
                               to_chunks(kk), to_chunks(inp.astype(f32))))
    return o.transpose(1, 0, 3, 2, 4).reshape(B, S, H, Dv)


def setup_inputs(seed: int = 0) -> dict:
    key = jax.random.key(seed)
    keys = jax.random.split(key, 16)
    f32 = jnp.float32
    x = jax.random.normal(keys[0], (BATCH, SEQ, D_MODEL), f32)
    col_scale = []
    for j, s in enumerate(SPLIT_SIZES):
        col_scale.append(jnp.full((s,), BETA if j in (2, 8) else 1.0, f32))
    col_scale = jnp.concatenate(col_scale)
    w_in = jax.random.normal(keys[1], (DEPTH, D_MODEL, D_PROJ), f32) * (D_MODEL ** -0.5) * col_scale
    w_out = jax.random.normal(keys[2], (DEPTH, D_MODEL, D_MODEL), f32) * (D_MODEL ** -0.5) * BETA
    hg_lb_logits = 0.1 * jax.random.normal(keys[3], (DEPTH + 1, HG_HEADS * HG_DK), f32)
    hg_norm_g = 1.0 + 0.02 * jax.random.normal(keys[4], (DEPTH, HG_WIDTH), f32)
    ln1_g = 1.0 + 0.02 * jax.random.normal(keys[5], (DEPTH, D_MODEL), f32)
    ln1_b = 0.02 * jax.random.normal(keys[6], (DEPTH, D_MODEL), f32)
    w_gate = jax.random.normal(keys[7], (DEPTH, D_MODEL, D_FF), f32) * (D_MODEL ** -0.5) * BETA
    w_up = jax.random.normal(keys[8], (DEPTH, D_MODEL, D_FF), f32) * (D_MODEL ** -0.5) * BETA
    w_down = jax.random.normal(keys[9], (DEPTH, D_FF, D_MODEL), f32) * (D_FF ** -0.5) * BETA
    ln2_g = 1.0 + 0.02 * jax.random.normal(keys[10], (DEPTH, D_MODEL), f32)
    ln2_b = 0.02 * jax.random.normal(keys[11], (DEPTH, D_MODEL), f32)
    return {"x": x, "w_in": w_in, "w_out": w_out, "hg_lb_logits": hg_lb_logits,
            "hg_norm_g": hg_norm_g, "ln1_g": ln1_g, "ln1_b": ln1_b, "w_gate": w_gate,
            "w_up": w_up, "w_down": w_down, "ln2_g": ln2_g, "ln2_b": ln2_b}


def reference(x, w_in, w_out, hg_lb_logits, hg_norm_g, ln1_g, ln1_b, w_gate, w_up, w_down, ln2_g, ln2_b):
    B, S, _ = x.shape
    f32 = jnp.float32
    lb_all = jnp.cumsum(jax.nn.softmax(hg_lb_logits.astype(f32), axis=0), axis=0)
    pts = _split_points()
    for l in range(DEPTH):
        proj = x @ w_in[l]
        dq, dk, dv, iq, ik, iw, hq, hf, hi, hg = jnp.split(proj, pts, axis=-1)
        a = _dsa_attention(dq.reshape(B, S, DSA_HEADS, DSA_HEAD_DIM),
                           dk.reshape(B, S, DSA_HEADS, DSA_HEAD_DIM),
                           dv.reshape(B, S, DSA_HEADS, DSA_HEAD_DIM),
                           iq.reshape(B, S, IDX_HEADS, IDX_DIM), ik, iw)
        lb = lb_all[l].reshape(HG_HEADS, HG_DK)
        o = _hgrn2(hq.reshape(B, S, HG_HEADS, HG_DK), hf.reshape(B, S, HG_HEADS, HG_DK),
                   hi.reshape(B, S, HG_HEADS, HG_DV), lb)
        o = o * lax.rsqrt(jnp.mean(jnp.square(o), axis=-1, keepdims=True) + RMS_EPS)
        r = o.reshape(B, S, HG_WIDTH) * hg_norm_g[l].astype(f32) * jax.nn.silu(hg.astype(f32))
        mix = jnp.concatenate([a, r.astype(x.dtype)], axis=-1) @ w_out[l]
        x = _layer_norm(ALPHA * x + mix, ln1_g[l], ln1_b[l])
        ff = (jax.nn.silu(x @ w_gate[l]) * (x @ w_up[l])) @ w_down[l]
        x = _layer_norm(ALPHA * x + ff, ln2_g[l], ln2_b[l])
    return x
```

```python
import functools

import jax
import jax.numpy as jnp
from jax import lax
from jax.experimental import pallas as pl
from jax.experimental.pallas import tpu as pltpu

D_MODEL = 1024
CHUNK = 64
Q_BLOCK = 128
DSA_WIDTH = 512
DSA_HEAD_DIM = 64
DSA_HEADS = 8
IDX_HEADS = 8
IDX_DIM = 64
TOPK_MAX = 256
HG_WIDTH = 512
HG_DK = 128
HG_HEADS = 4
D_FF = 2816
DEPTH = 1
ALPHA = (2.0 * DEPTH) ** 0.25
LN_EPS = 1e-5
RMS_EPS = 1e-6
SPLIT_SIZES = (512, 512, 512, 512, 64, 8, 512, 512, 512, 512)

F32 = jnp.float32
BF16 = jnp.bfloat16

LANES = 128
VMEM_LIMIT_BYTES = 56 * 1024 * 1024

KEY_TILE = 256
PROJ_ROWS = 512
HG_ROWS = 512
MIX_ROWS = 512
FFN_ROWS = 256
MASK_BIG = 1e30
LOWEST = -3.0e38
BISECT_MAX_ITERS = 80

NT_DIMS = (((1,), (1,)), ((), ()))
TN_DIMS = (((0,), (0,)), ((), ()))


def _proj_kernel(x_ref, wn_ref, wt_ref, q_ref, v_ref, iq_ref, iw_ref, hq_ref, hf_ref,
                 hi_ref, hg_ref, kt_ref, ikt_ref):
    xb = x_ref[...].astype(BF16)

    def nn(j, width=512):
        return jnp.dot(xb, wn_ref[:, j * 512:j * 512 + width], preferred_element_type=F32)

    q_ref[...] = (nn(0) * (DSA_HEAD_DIM ** -0.5)).astype(BF16)
    v_ref[...] = nn(1).astype(BF16)
    iq_ref[...] = nn(2).astype(BF16)
    hq_ref[...] = nn(3)
    hf_ref[...] = nn(4)
    hi_ref[...] = nn(5).astype(BF16)
    hg_ref[...] = nn(6).astype(BF16)
    iw_ref[...] = nn(7, LANES)
    kt = lax.dot_general(wt_ref[...], xb, NT_DIMS, preferred_element_type=F32)
    for j in range(PROJ_ROWS // KEY_TILE):
        cols = slice(j * KEY_TILE, (j + 1) * KEY_TILE)
        kt_ref[j] = kt[:DSA_WIDTH, cols].astype(BF16)
        ikt_ref[j] = kt[DSA_WIDTH:, cols].astype(BF16)


def _project(x2, wn, wt, batch, seq):
    n = x2.shape[0]
    nb = seq // PROJ_ROWS
    tiles = PROJ_ROWS // KEY_TILE
    row_spec = lambda w: pl.BlockSpec((PROJ_ROWS, w), lambda i: (i, 0))
    out_shape = (
        jax.ShapeDtypeStruct((n, 512), BF16),
        jax.ShapeDtypeStruct((n, 512), BF16),
        jax.ShapeDtypeStruct((n, 512), BF16),
        jax.ShapeDtypeStruct((n, LANES), F32),
        jax.ShapeDtypeStruct((n, 512), F32),
        jax.ShapeDtypeStruct((n, 512), F32),
        jax.ShapeDtypeStruct((n, 512), BF16),
        jax.ShapeDtypeStruct((n, 512), BF16),
        jax.ShapeDtypeStruct((batch, seq // KEY_TILE, 512, KEY_TILE), BF16),
        jax.ShapeDtypeStruct((batch, seq // KEY_TILE, 2 * IDX_DIM, KEY_TILE), BF16),
    )
    out_specs = (
        row_spec(512), row_spec(512), row_spec(512), row_spec(LANES), row_spec(512),
        row_spec(512), row_spec(512), row_spec(512),
        pl.BlockSpec((None, tiles, 512, KEY_TILE), lambda i: (i // nb, i % nb, 0, 0)),
        pl.BlockSpec((None, tiles, 2 * IDX_DIM, KEY_TILE), lambda i: (i // nb, i % nb, 0, 0)),
    )
    return pl.pallas_call(
        _proj_kernel,
        grid=(n // PROJ_ROWS,),
        in_specs=[
            pl.BlockSpec((PROJ_ROWS, D_MODEL), lambda i: (i, 0)),
            pl.BlockSpec(wn.shape, lambda i: (0, 0)),
            pl.BlockSpec(wt.shape, lambda i: (0, 0)),
        ],
        out_specs=out_specs,
        out_shape=out_shape,
        compiler_params=pltpu.CompilerParams(
            dimension_semantics=("arbitrary",), vmem_limit_bytes=VMEM_LIMIT_BYTES),
        name="proj",
    )(x2, wn, wt)


def _dsa_kernel(q_ref, iq_ref, iw_ref, kt_ref, v_ref, ikt_ref, o_ref,
                score_ref, qm_ref, iqm_ref, acc_ref, m_ref, l_ref):
    i = pl.program_id(1)
    n_tiles = (i + 2) // 2
    topk = TOPK_MAX
    idx_scale = (IDX_DIM ** -0.5) * (IDX_HEADS ** -0.5)

    lane = lax.broadcasted_iota(jnp.int32, (Q_BLOCK, LANES), 1)
    lo_half = lane < DSA_HEAD_DIM
    for p in range(DSA_HEADS // 2):
        qp = q_ref[:, p * LANES:(p + 1) * LANES]
        iqp = iq_ref[:, p * LANES:(p + 1) * LANES]
        zero = jnp.zeros_like(qp)
        qm_ref[2 * p] = jnp.where(lo_half, qp, zero)
        qm_ref[2 * p + 1] = jnp.where(lo_half, zero, qp)
        iqm_ref[2 * p] = jnp.where(lo_half, iqp, zero)
        iqm_ref[2 * p + 1] = jnp.where(lo_half, zero, iqp)

    w = iw_ref[...] * idx_scale
    row = lax.broadcasted_iota(jnp.int32, (Q_BLOCK, 1), 0)
    qpos = i * Q_BLOCK + row
    limit = (qpos // CHUNK + 1) * CHUNK
    key_iota = lax.broadcasted_iota(jnp.int32, (1, KEY_TILE), 1)

    def fold_lanes(x, op):
        out = x[:, :LANES]
        for j in range(1, KEY_TILE // LANES):
            out = op(out, x[:, j * LANES:(j + 1) * LANES])
        return out

    def score_tile(kt, carry):
        rmax, rmin = carry
        ik = ikt_ref[kt]
        acc = jnp.zeros((Q_BLOCK, KEY_TILE), F32)
        for h in range(IDX_HEADS):
            s = jnp.dot(iqm_ref[h], ik, preferred_element_type=F32)
            acc = acc + w[:, h:h + 1] * jnp.maximum(s, 0.0)
        adm = (kt * KEY_TILE + key_iota) < limit
        score_ref[kt] = jnp.where(adm, acc, -jnp.inf)
        rmax = jnp.maximum(rmax, fold_lanes(jnp.where(adm, acc, -jnp.inf), jnp.maximum))
        rmin = jnp.minimum(rmin, fold_lanes(jnp.where(adm, acc, jnp.inf), jnp.minimum))
        return rmax, rmin

    rmax, rmin = lax.fori_loop(
        0, n_tiles, score_tile,
        (jnp.full((Q_BLOCK, LANES), -jnp.inf, F32), jnp.full((Q_BLOCK, LANES), jnp.inf, F32)))
    row_max = jnp.max(rmax, axis=1, keepdims=True)
    row_min = jnp.min(rmin, axis=1, keepdims=True)

    def count_ge(thr):
        def body(kt, acc):
            c = jnp.where(score_ref[kt] >= thr, 1.0, 0.0)
            return acc + fold_lanes(c, jnp.add)
        acc = lax.fori_loop(0, n_tiles, body, jnp.zeros((Q_BLOCK, LANES), F32))
        return jnp.sum(acc, axis=1, keepdims=True)

    need = limit > topk
    active0 = need.astype(jnp.int32)

    def bis_cond(c):
        return jnp.logical_and(c[3] > 0, c[4] < BISECT_MAX_ITERS)

    def bis_body(c):
        lo, hi, active, _, it = c
        mid = lo + 0.5 * (hi - lo)
        cnt = count_ge(mid)
        ge = cnt >= topk
        stuck = jnp.logical_or(mid <= lo, mid >= hi)
        upd = active > 0
        lo = jnp.where(jnp.logical_and(upd, ge), mid, lo)
        hi = jnp.where(jnp.logical_and(upd, jnp.logical_not(ge)), mid, hi)
        finished = jnp.logical_or(cnt == topk, stuck)
        active = jnp.where(finished, 0, active)
        return lo, hi, active, jnp.sum(active), it + 1

    lo, _, _, _, _ = lax.while_loop(
        bis_cond, bis_body, (row_min, row_max, active0, jnp.sum(active0), jnp.int32(0)))
    thr = jnp.where(need, lo, LOWEST)

    m_ref[...] = jnp.full(m_ref.shape, -jnp.inf, F32)
    l_ref[...] = jnp.zeros(l_ref.shape, F32)
    acc_ref[...] = jnp.zeros(acc_ref.shape, F32)
    qpos_f = qpos.astype(F32)

    def attn_tile(kt, carry):
        sc = score_ref[kt]
        kpos_f = (kt * KEY_TILE + key_iota).astype(F32)
        dist = jnp.abs(qpos_f - kpos_f)
        dmask = jnp.where(sc >= thr, dist, MASK_BIG)
        koff = pl.multiple_of(kt * KEY_TILE, KEY_TILE)
        for h in range(DSA_HEADS):
            pair = slice((h // 2) * LANES, (h // 2 + 1) * LANES)
            slope = 2.0 ** (-8.0 * (h + 1) / DSA_HEADS)
            s = jnp.dot(qm_ref[h], kt_ref[kt, pair, :], preferred_element_type=F32)
            s = s - slope * dmask
            m_old = m_ref[h]
            m_new = jnp.maximum(m_old, jnp.max(s, axis=1, keepdims=True))
            alpha = jnp.exp(m_old - m_new)
            p = jnp.exp(s - m_new)
            l_ref[h] = alpha * l_ref[h] + jnp.sum(p, axis=1, keepdims=True)
            pv = jnp.dot(p.astype(BF16), v_ref[pl.ds(koff, KEY_TILE), pair],
                         preferred_element_type=F32)
            acc_ref[h] = alpha * acc_ref[h] + pv
            m_ref[h] = m_new
        return carry

    lax.fori_loop(0, n_tiles, attn_tile, 0)

    for p in range(DSA_HEADS // 2):
        even = acc_ref[2 * p] / l_ref[2 * p]
        odd = acc_ref[2 * p + 1] / l_ref[2 * p + 1]
        o_ref[:, p * LANES:(p + 1) * LANES] = jnp.where(lo_half, even, odd).astype(o_ref.dtype)


def _dsa(q, iq, iw, kt, v3, ikt, batch, seq):
    n = q.shape[0]
    nqb = seq // Q_BLOCK
    n_key_tiles = seq // KEY_TILE
    row_spec = lambda w: pl.BlockSpec((Q_BLOCK, w), lambda b, i: (b * nqb + i, 0))
    return pl.pallas_call(
        _dsa_kernel,
        grid=(batch, nqb),
        in_specs=[
            row_spec(512), row_spec(512), row_spec(LANES),
            pl.BlockSpec((None, n_key_tiles, 512, KEY_TILE), lambda b, i: (b, 0, 0, 0)),
            pl.BlockSpec((None, seq, 512), lambda b, i: (b, 0, 0)),
            pl.BlockSpec((None, n_key_tiles, 2 * IDX_DIM, KEY_TILE), lambda b, i: (b, 0, 0, 0)),
        ],
        out_specs=row_spec(512),
        out_shape=jax.ShapeDtypeStruct((n, 512), BF16),
        scratch_shapes=[
            pltpu.VMEM((n_key_tiles, Q_BLOCK, KEY_TILE), F32),
            pltpu.VMEM((DSA_HEADS, Q_BLOCK, LANES), BF16),
            pltpu.VMEM((IDX_HEADS, Q_BLOCK, LANES), BF16),
            pltpu.VMEM((DSA_HEADS, Q_BLOCK, LANES), F32),
            pltpu.VMEM((DSA_HEADS, Q_BLOCK, 1), F32),
            pltpu.VMEM((DSA_HEADS, Q_BLOCK, 1), F32),
        ],
        compiler_params=pltpu.CompilerParams(
            dimension_semantics=("arbitrary", "arbitrary"), vmem_limit_bytes=VMEM_LIMIT_BYTES),
        name="dsa",
    )(q, iq, iw, kt, v3, ikt)


def _hgrn_kernel(hq_ref, hf_ref, hi_ref, hg_ref, lb_ref, gain_ref, r_ref, st_ref):
    @pl.when(pl.program_id(1) == 0)
    def _():
        st_ref[...] = jnp.zeros(st_ref.shape, F32)

    r_i = lax.broadcasted_iota(jnp.int32, (CHUNK, CHUNK), 0)
    c_i = lax.broadcasted_iota(jnp.int32, (CHUNK, CHUNK), 1)
    tril = r_i >= c_i
    tril_b = jnp.where(tril, 1.0, 0.0).astype(BF16)
    lb = lb_ref[...]
    gain = gain_ref[...]

    def chunk(c, carry):
        off = pl.multiple_of(c * CHUNK, CHUNK)
        rows = pl.ds(off, CHUNK)
        f = lb + (1.0 - lb) * jax.nn.sigmoid(hf_ref[rows, :])
        logf = jnp.log(f)
        kk = 1.0 - f
        t0 = logf.astype(BF16)
        r1 = logf - t0.astype(F32)
        t1 = r1.astype(BF16)
        t2 = (r1 - t1.astype(F32)).astype(BF16)
        g = (jnp.dot(tril_b, t0, preferred_element_type=F32)
             + jnp.dot(tril_b, t1, preferred_element_type=F32)
             + jnp.dot(tril_b, t2, preferred_element_type=F32))
        g_last = g[CHUNK - 1:CHUNK, :]
        g_mid = g[CHUNK // 2 - 1:CHUNK // 2, :]
        q = hq_ref[rows, :]
        q_state = (q * jnp.exp(g)).astype(BF16)
        q_intra = (q * jnp.exp(g - g_mid)).astype(BF16)
        k_intra = (kk * jnp.exp(g_mid - g)).astype(BF16)
        k_state = (kk * jnp.exp(g_last - g)).astype(BF16)
        decay = jnp.exp(g_last)
        v = hi_ref[rows, :]
        for h in range(HG_HEADS):
            sl = slice(h * HG_DK, (h + 1) * HG_DK)
            a = lax.dot_general(q_intra[:, sl], k_intra[:, sl], NT_DIMS,
                                preferred_element_type=F32)
            a = jnp.where(tril, a, 0.0)
            st = st_ref[h]
            o = (jnp.dot(a.astype(BF16), v[:, sl], preferred_element_type=F32)
                 + lax.dot_general(q_state[:, sl], st.astype(BF16), NT_DIMS,
                                   preferred_element_type=F32))
            st_ref[h] = st * decay[:, sl] + lax.dot_general(
                v[:, sl], k_state[:, sl], TN_DIMS, preferred_element_type=F32)
            ms = jnp.mean(o * o, axis=1, keepdims=True)
            on = o * lax.rsqrt(ms + RMS_EPS)
            gate = hg_ref[rows, sl].astype(F32)
            r = on * gain[:, sl] * (gate * jax.nn.sigmoid(gate))
            r_ref[rows, sl] = r.astype(r_ref.dtype)
        return carry

    lax.fori_loop(0, HG_ROWS // CHUNK, chunk, 0)


def _hgrn(hq, hf, hi, hg, lb, gain, batch, seq):
    n = hq.shape[0]
    nb = seq // HG_ROWS
    row_spec = pl.BlockSpec((HG_ROWS, 512), lambda b, t: (b * nb + t, 0))
    vec_spec = pl.BlockSpec((1, 512), lambda b, t: (0, 0))
    return pl.pallas_call(
        _hgrn_kernel,
        grid=(batch, nb),
        in_specs=[row_spec, row_spec, row_spec, row_spec, vec_spec, vec_spec],
        out_specs=row_spec,
        out_shape=jax.ShapeDtypeStruct((n, 512), BF16),
        scratch_shapes=[pltpu.VMEM((HG_HEADS, HG_DK, HG_DK), F32)],
        compiler_params=pltpu.CompilerParams(
            dimension_semantics=("arbitrary", "arbitrary"), vmem_limit_bytes=VMEM_LIMIT_BYTES),
        name="hgrn",
    )(hq, hf, hi, hg, lb, gain)


def _layer_norm(y, g, b):
    mu = jnp.mean(y, axis=1, keepdims=True)
    d = y - mu
    var = jnp.mean(d * d, axis=1, keepdims=True)
    return d * lax.rsqrt(var + LN_EPS) * g + b


def _mix_kernel(a_ref, r_ref, x_ref, wo_ref, g_ref, b_ref, o_ref):
    mix = (jnp.dot(a_ref[...], wo_ref[:DSA_WIDTH, :], preferred_element_type=F32)
           + jnp.dot(r_ref[...], wo_ref[DSA_WIDTH:, :], preferred_element_type=F32))
    o_ref[...] = _layer_norm(ALPHA * x_ref[...] + mix, g_ref[...], b_ref[...])


def _mix(a, r, x2, wo, g, b):
    n = x2.shape[0]
    half = pl.BlockSpec((MIX_ROWS, 512), lambda i: (i, 0))
    full = pl.BlockSpec((MIX_ROWS, D_MODEL), lambda i: (i, 0))
    vec = pl.BlockSpec((1, D_MODEL), lambda i: (0, 0))
    return pl.pallas_call(
        _mix_kernel,
        grid=(n // MIX_ROWS,),
        in_specs=[half, half, full, pl.BlockSpec(wo.shape, lambda i: (0, 0)), vec, vec],
        out_specs=full,
        out_shape=jax.ShapeDtypeStruct((n, D_MODEL), F32),
        compiler_params=pltpu.CompilerParams(
            dimension_semantics=("arbitrary",), vmem_limit_bytes=VMEM_LIMIT_BYTES),
        name="mix_ln",
    )(a, r, x2, wo, g, b)


def _ffn_kernel(x_ref, wg_ref, wu_ref, wd_ref, g_ref, b_ref, o_ref):
    x = x_ref[...]
    xb = x.astype(BF16)
    gate = jnp.dot(xb, wg_ref[...], preferred_element_type=F32)
    up = jnp.dot(xb, wu_ref[...], preferred_element_type=F32)
    h = (gate * jax.nn.sigmoid(gate) * up).astype(BF16)
    ff = jnp.dot(h, wd_ref[...], preferred_element_type=F32)
    o_ref[...] = _layer_norm(ALPHA * x + ff, g_ref[...], b_ref[...])


def _ffn(x1, wg, wu, wd, g, b):
    n = x1.shape[0]
    full = pl.BlockSpec((FFN_ROWS, D_MODEL), lambda i: (i, 0))
    vec = pl.BlockSpec((1, D_MODEL), lambda i: (0, 0))
    const = lambda w: pl.BlockSpec(w.shape, lambda i: (0, 0), pipeline_mode=pl.Buffered(1))
    return pl.pallas_call(
        _ffn_kernel,
        grid=(n // FFN_ROWS,),
        in_specs=[full, const(wg), const(wu), const(wd), vec, vec],
        out_specs=full,
        out_shape=jax.ShapeDtypeStruct((n, D_MODEL), F32),
        compiler_params=pltpu.CompilerParams(
            dimension_semantics=("arbitrary",), vmem_limit_bytes=VMEM_LIMIT_BYTES),
        name="ffn_ln",
    )(x1, wg, wu, wd, g, b)


def kernel(x, w_in, w_out, hg_lb_logits, hg_norm_g, ln1_g, ln1_b, w_gate, w_up, w_down, ln2_g, ln2_b):
    batch, seq, _ = x.shape
    n = batch * seq
    pts = [0]
    for s in SPLIT_SIZES:
        pts.append(pts[-1] + s)
    lb_all = jnp.cumsum(jax.nn.softmax(hg_lb_logits.astype(F32), axis=0), axis=0)

    x2 = x.reshape(n, D_MODEL)
    for l in range(DEPTH):
        w = w_in[l]
        col = lambda j: w[:, pts[j]:pts[j + 1]]
        wn = jnp.concatenate(
            [col(0), col(2), col(3), col(6), col(7), col(8), col(9),
             jnp.pad(col(5), ((0, 0), (0, LANES - IDX_HEADS)))], axis=1).astype(BF16)
        wt = jnp.concatenate([col(1), col(4), col(4)], axis=1).T.astype(BF16)
        q, v, iq, iw, hq, hf, hi, hg, kt, ikt = _project(x2, wn, wt, batch, seq)
        a = _dsa(q, iq, iw, kt, v.reshape(batch, seq, 512), ikt, batch, seq)
        r = _hgrn(hq, hf, hi, hg, lb_all[l].reshape(1, 512), hg_norm_g[l].reshape(1, 512).astype(F32),
                  batch, seq)
        x1 = _mix(a, r, x2, w_out[l].astype(BF16), ln1_g[l].reshape(1, D_MODEL),
                  ln1_b[l].reshape(1, D_MODEL))
        x2 = _ffn(x1, w_gate[l].astype(BF16), w_up[l].astype(BF16), w_down[l].astype(BF16),
                  ln2_g[l].reshape(1, D_MODEL), ln2_b[l].reshape(1, D_MODEL))
    return x2.reshape(batch, seq, D_MODEL)
```

```python
import jax
import jax.numpy as jnp
from jax import lax
from jax.experimental import pallas as pl
from jax.experimental.pallas import tpu as pltpu

D_MODEL = 1024
CHUNK = 64
DSA_WIDTH = 512
DSA_HEAD_DIM = 64
DSA_HEADS = 8
IDX_HEADS = 8
IDX_DIM = 64
TOPK_MAX = 256
HG_WIDTH = 512
HG_DK = 128
HG_HEADS = 4
D_FF = 2816
DEPTH = 1
ALPHA = (2.0 * DEPTH) ** 0.25
LN_EPS = 1e-5
RMS_EPS = 1e-6
SPLIT_SIZES = (512, 512, 512, 512, 64, 8, 512, 512, 512, 512)

F32 = jnp.float32
BF16 = jnp.bfloat16

LANES = 128
SUBLANES = 8
VMEM_LIMIT_BYTES = 56 * 1024 * 1024

Q_TILE = 256
KEY_TILE = 256
IW_ROWS = 16
PROJ_ROWS = 512
HG_ROWS = 512
MIX_ROWS = 512
FFN_ROWS = 256
MASK_BIG = 1e30
LOWEST = -3.0e38
BISECT_MAX_ITERS = 80

NT_DIMS = (((1,), (1,)), ((), ()))
TN_DIMS = (((0,), (0,)), ((), ()))


def _proj_kernel(x_ref, wn_ref, wt_ref, k_ref, ik_ref, hq_ref, hf_ref, hi_ref, hg_ref,
                 qt_ref, iqt_ref, vt_ref, iwt_ref):
    xb = x_ref[...].astype(BF16)

    def nn(j, width=512):
        return jnp.dot(xb, wn_ref[:, j * 512:j * 512 + width], preferred_element_type=F32)

    k_ref[...] = nn(0).astype(BF16)
    hq_ref[...] = nn(1)
    hf_ref[...] = nn(2)
    hi_ref[...] = nn(3).astype(BF16)
    hg_ref[...] = nn(4).astype(BF16)
    ik_ref[...] = nn(5, 2 * IDX_DIM).astype(BF16)
    t = lax.dot_general(wt_ref[...], xb, NT_DIMS, preferred_element_type=F32)
    for j in range(PROJ_ROWS // Q_TILE):
        cols = slice(j * Q_TILE, (j + 1) * Q_TILE)
        qt_ref[j] = (t[0:512, cols] * (DSA_HEAD_DIM ** -0.5)).astype(BF16)
        iqt_ref[j] = t[512:1024, cols].astype(BF16)
        vt_ref[j] = t[1024:1536, cols].astype(BF16)
        iwt_ref[j] = t[1536:1536 + IW_ROWS, cols]


def _project(x2, wn, wt, batch, seq):
    n = x2.shape[0]
    nb = seq // PROJ_ROWS
    tiles = PROJ_ROWS // Q_TILE
    n_tiles = seq // Q_TILE
    row_spec = lambda w: pl.BlockSpec((PROJ_ROWS, w), lambda i: (i, 0))
    t_shape = lambda rows, dt: jax.ShapeDtypeStruct((batch, n_tiles, rows, Q_TILE), dt)
    t_spec = lambda rows: pl.BlockSpec((None, tiles, rows, Q_TILE), lambda i: (i // nb, i % nb, 0, 0))
    out_shape = (
        jax.ShapeDtypeStruct((n, 512), BF16),
        jax.ShapeDtypeStruct((n, 2 * IDX_DIM), BF16),
        jax.ShapeDtypeStruct((n, 512), F32),
        jax.ShapeDtypeStruct((n, 512), F32),
        jax.ShapeDtypeStruct((n, 512), BF16),
        jax.ShapeDtypeStruct((n, 512), BF16),
        t_shape(512, BF16),
        t_shape(512, BF16),
        t_shape(512, BF16),
        t_shape(IW_ROWS, F32),
    )
    out_specs = (
        row_spec(512), row_spec(2 * IDX_DIM), row_spec(512), row_spec(512), row_spec(512),
        row_spec(512), t_spec(512), t_spec(512), t_spec(512), t_spec(IW_ROWS),
    )
    return pl.pallas_call(
        _proj_kernel,
        grid=(n // PROJ_ROWS,),
        in_specs=[
            pl.BlockSpec((PROJ_ROWS, D_MODEL), lambda i: (i, 0)),
            pl.BlockSpec(wn.shape, lambda i: (0, 0)),
            pl.BlockSpec(wt.shape, lambda i: (0, 0)),
        ],
        out_specs=out_specs,
        out_shape=out_shape,
        compiler_params=pltpu.CompilerParams(
            dimension_semantics=("arbitrary",), vmem_limit_bytes=VMEM_LIMIT_BYTES),
        name="proj",
    )(x2, wn, wt)


def _fold_rows(x, op):
    return op(x.reshape(x.shape[0] // SUBLANES, SUBLANES, x.shape[1]), axis=0)


def _dsa_kernel(qt_ref, iqt_ref, iwt_ref, k_ref, vt_ref, ik_ref, o_ref,
                score_ref, qm_ref, iqm_ref, s_ref, acc_ref, m_ref, l_ref):
    i = pl.program_id(1)
    n_tiles = i + 1
    topk = TOPK_MAX
    idx_scale = (IDX_DIM ** -0.5) * (IDX_HEADS ** -0.5)

    even_rows = lax.broadcasted_iota(jnp.int32, (LANES, Q_TILE), 0) < DSA_HEAD_DIM
    for p in range(DSA_HEADS // 2):
        rows = slice(p * LANES, (p + 1) * LANES)
        qp = qt_ref[rows, :].astype(F32)
        iqp = iqt_ref[rows, :].astype(F32)
        qm_ref[2 * p] = jnp.where(even_rows, qp, 0.0).astype(BF16)
        qm_ref[2 * p + 1] = jnp.where(even_rows, 0.0, qp).astype(BF16)
        iqm_ref[2 * p] = jnp.where(even_rows, iqp, 0.0).astype(BF16)
        iqm_ref[2 * p + 1] = jnp.where(even_rows, 0.0, iqp).astype(BF16)

    w = iwt_ref[...] * idx_scale
    qpos = i * Q_TILE + lax.broadcasted_iota(jnp.int32, (1, Q_TILE), 1)
    limit = (qpos // CHUNK + 1) * CHUNK
    key_iota = lax.broadcasted_iota(jnp.int32, (KEY_TILE, 1), 0)

    def score_tile(kt, carry):
        rmax, rmin = carry
        koff = pl.multiple_of(kt * KEY_TILE, KEY_TILE)
        ik = ik_ref[pl.ds(koff, KEY_TILE), :]
        acc = jnp.zeros((KEY_TILE, Q_TILE), F32)
        for h in range(IDX_HEADS):
            s = jnp.dot(ik, iqm_ref[h], preferred_element_type=F32)
            acc = acc + w[h:h + 1, :] * jnp.maximum(s, 0.0)
        adm = (kt * KEY_TILE + key_iota) < limit
        score_ref[kt] = jnp.where(adm, acc, -jnp.inf)
        rmax = jnp.maximum(rmax, _fold_rows(jnp.where(adm, acc, -jnp.inf), jnp.max))
        rmin = jnp.minimum(rmin, _fold_rows(jnp.where(adm, acc, jnp.inf), jnp.min))
        return rmax, rmin

    rmax, rmin = lax.fori_loop(
        0, n_tiles, score_tile,
        (jnp.full((SUBLANES, Q_TILE), -jnp.inf, F32), jnp.full((SUBLANES, Q_TILE), jnp.inf, F32)))
    col_max = jnp.max(rmax, axis=0, keepdims=True)
    col_min = jnp.min(rmin, axis=0, keepdims=True)

    def count_ge(thr):
        def body(kt, acc):
            return acc + _fold_rows(jnp.where(score_ref[kt] >= thr, 1.0, 0.0), jnp.sum)
        acc = lax.fori_loop(0, n_tiles, body, jnp.zeros((SUBLANES, Q_TILE), F32))
        return jnp.sum(acc, axis=0, keepdims=True)

    need = limit > topk
    active0 = need.astype(jnp.int32)

    def bis_cond(c):
        return jnp.logical_and(c[3] > 0, c[4] < BISECT_MAX_ITERS)

    def bis_body(c):
        lo, hi, active, _, it = c
        mid = lo + 0.5 * (hi - lo)
        cnt = count_ge(mid)
        ge = cnt >= topk
        stuck = jnp.logical_or(mid <= lo, mid >= hi)
        upd = active > 0
        lo = jnp.where(jnp.logical_and(upd, ge), mid, lo)
        hi = jnp.where(jnp.logical_and(upd, jnp.logical_not(ge)), mid, hi)
        finished = jnp.logical_or(cnt == topk, stuck)
        active = jnp.where(finished, 0, active)
        return lo, hi, active, jnp.sum(active), it + 1

    lo, _, _, _, _ = lax.while_loop(
        bis_cond, bis_body, (col_min, col_max, active0, jnp.sum(active0), jnp.int32(0)))
    thr = jnp.where(need, lo, LOWEST)

    m_ref[...] = jnp.full(m_ref.shape, -jnp.inf, F32)
    l_ref[...] = jnp.zeros(l_ref.shape, F32)
    acc_ref[...] = jnp.zeros(acc_ref.shape, F32)
    qpos_f = qpos.astype(F32)

    def attn_tile(kt, carry):
        koff = pl.multiple_of(kt * KEY_TILE, KEY_TILE)
        kpos_f = (kt * KEY_TILE + key_iota).astype(F32)
        dist = jnp.abs(kpos_f - qpos_f)
        dmask = jnp.where(score_ref[kt] >= thr, dist, MASK_BIG)
        alphas = []
        for h in range(DSA_HEADS):
            pair = slice((h // 2) * LANES, (h // 2 + 1) * LANES)
            slope = 2.0 ** (-8.0 * (h + 1) / DSA_HEADS)
            s = jnp.dot(k_ref[pl.ds(koff, KEY_TILE), pair], qm_ref[h],
                        preferred_element_type=F32)
            s = s - slope * dmask
            s_ref[h] = s
            m_old = m_ref[h:h + 1, :]
            m_new = jnp.maximum(m_old, jnp.max(_fold_rows(s, jnp.max), axis=0, keepdims=True))
            alphas.append(jnp.exp(m_old - m_new))
            m_ref[h:h + 1, :] = m_new
        for h in range(DSA_HEADS):
            rows = slice(h * DSA_HEAD_DIM, (h + 1) * DSA_HEAD_DIM)
            p = jnp.exp(s_ref[h] - m_ref[h:h + 1, :])
            l_ref[h:h + 1, :] = (alphas[h] * l_ref[h:h + 1, :]
                                 + jnp.sum(_fold_rows(p, jnp.sum), axis=0, keepdims=True))
            pv = jnp.dot(vt_ref[kt, rows, :], p.astype(BF16), preferred_element_type=F32)
            acc_ref[rows, :] = alphas[h] * acc_ref[rows, :] + pv
        return carry

    lax.fori_loop(0, n_tiles, attn_tile, 0)

    for h in range(DSA_HEADS):
        rows = slice(h * DSA_HEAD_DIM, (h + 1) * DSA_HEAD_DIM)
        acc_ref[rows, :] = acc_ref[rows, :] / l_ref[h:h + 1, :]
    o_ref[...] = acc_ref[...].T.astype(o_ref.dtype)


def _dsa(qt, iqt, iwt, k3, vt, ik3, batch, seq):
    n = batch * seq
    nqb = seq // Q_TILE
    n_key_tiles = seq // KEY_TILE
    t_spec = lambda rows: pl.BlockSpec((None, None, rows, Q_TILE), lambda b, i: (b, i, 0, 0))
    return pl.pallas_call(
        _dsa_kernel,
        grid=(batch, nqb),
        in_specs=[
            t_spec(512), t_spec(512), t_spec(IW_ROWS),
            pl.BlockSpec((None, seq, 512), lambda b, i: (b, 0, 0)),
            pl.BlockSpec((None, n_key_tiles, 512, KEY_TILE), lambda b, i: (b, 0, 0, 0)),
            pl.BlockSpec((None, seq, 2 * IDX_DIM), lambda b, i: (b, 0, 0)),
        ],
        out_specs=pl.BlockSpec((Q_TILE, 512), lambda b, i: (b * nqb + i, 0)),
        out_shape=jax.ShapeDtypeStruct((n, 512), BF16),
        scratch_shapes=[
            pltpu.VMEM((n_key_tiles, KEY_TILE, Q_TILE), F32),
            pltpu.VMEM((DSA_HEADS, LANES, Q_TILE), BF16),
            pltpu.VMEM((IDX_HEADS, LANES, Q_TILE), BF16),
            pltpu.VMEM((DSA_HEADS, KEY_TILE, Q_TILE), F32),
            pltpu.VMEM((DSA_WIDTH, Q_TILE), F32),
            pltpu.VMEM((DSA_HEADS, Q_TILE), F32),
            pltpu.VMEM((DSA_HEADS, Q_TILE), F32),
        ],
        compiler_params=pltpu.CompilerParams(
            dimension_semantics=("arbitrary", "arbitrary"), vmem_limit_bytes=VMEM_LIMIT_BYTES),
        name="dsa",
    )(qt, iqt, iwt, k3, vt, ik3)


def _hgrn_kernel(hq_ref, hf_ref, hi_ref, hg_ref, lb_ref, gain_ref, r_ref, st_ref):
    @pl.when(pl.program_id(1) == 0)
    def _():
        st_ref[...] = jnp.zeros(st_ref.shape, F32)

    r_i = lax.broadcasted_iota(jnp.int32, (CHUNK, CHUNK), 0)
    c_i = lax.broadcasted_iota(jnp.int32, (CHUNK, CHUNK), 1)
    tril = r_i >= c_i
    tril_b = jnp.where(tril, 1.0, 0.0).astype(BF16)
    lb = lb_ref[...]
    gain = gain_ref[...]

    def chunk(c, carry):
        off = pl.multiple_of(c * CHUNK, CHUNK)
        rows = pl.ds(off, CHUNK)
        f = lb + (1.0 - lb) * jax.nn.sigmoid(hf_ref[rows, :])
        logf = jnp.log(f)
        kk = 1.0 - f
        t0 = logf.astype(BF16)
        r1 = logf - t0.astype(F32)
        t1 = r1.astype(BF16)
        t2 = (r1 - t1.astype(F32)).astype(BF16)
        g = (jnp.dot(tril_b, t0, preferred_element_type=F32)
             + jnp.dot(tril_b, t1, preferred_element_type=F32)
             + jnp.dot(tril_b, t2, preferred_element_type=F32))
        g_last = g[CHUNK - 1:CHUNK, :]
        g_mid = g[CHUNK // 2 - 1:CHUNK // 2, :]
        q = hq_ref[rows, :]
        q_state = (q * jnp.exp(g)).astype(BF16)
        q_intra = (q * jnp.exp(g - g_mid)).astype(BF16)
        k_intra = (kk * jnp.exp(g_mid - g)).astype(BF16)
        k_state = (kk * jnp.exp(g_last - g)).astype(BF16)
        decay = jnp.exp(g_last)
        v = hi_ref[rows, :]
        for h in range(HG_HEADS):
            sl = slice(h * HG_DK, (h + 1) * HG_DK)
            a = lax.dot_general(q_intra[:, sl], k_intra[:, sl], NT_DIMS,
                                preferred_element_type=F32)
            a = jnp.where(tril, a, 0.0)
            st = st_ref[h]
            o = (jnp.dot(a.astype(BF16), v[:, sl], preferred_element_type=F32)
                 + lax.dot_general(q_state[:, sl], st.astype(BF16), NT_DIMS,
                                   preferred_element_type=F32))
            st_ref[h] = st * decay[:, sl] + lax.dot_general(
                v[:, sl], k_state[:, sl], TN_DIMS, preferred_element_type=F32)
            ms = jnp.mean(o * o, axis=1, keepdims=True)
            on = o * lax.rsqrt(ms + RMS_EPS)
            gate = hg_ref[rows, sl].astype(F32)
            r = on * gain[:, sl] * (gate * jax.nn.sigmoid(gate))
            r_ref[rows, sl] = r.astype(r_ref.dtype)
        return carry

    lax.fori_loop(0, HG_ROWS // CHUNK, chunk, 0)


def _hgrn(hq, hf, hi, hg, lb, gain, batch, seq):
    n = hq.shape[0]
    nb = seq // HG_ROWS
    row_spec = pl.BlockSpec((HG_ROWS, 512), lambda b, t: (b * nb + t, 0))
    vec_spec = pl.BlockSpec((1, 512), lambda b, t: (0, 0))
    return pl.pallas_call(
        _hgrn_kernel,
        grid=(batch, nb),
        in_specs=[row_spec, row_spec, row_spec, row_spec, vec_spec, vec_spec],
        out_specs=row_spec,
        out_shape=jax.ShapeDtypeStruct((n, 512), BF16),
        scratch_shapes=[pltpu.VMEM((HG_HEADS, HG_DK, HG_DK), F32)],
        compiler_params=pltpu.CompilerParams(
            dimension_semantics=("arbitrary", "arbitrary"), vmem_limit_bytes=VMEM_LIMIT_BYTES),
        name="hgrn",
    )(hq, hf, hi, hg, lb, gain)


def _layer_norm(y, g, b):
    mu = jnp.mean(y, axis=1, keepdims=True)
    d = y - mu
    var = jnp.mean(d * d, axis=1, keepdims=True)
    return d * lax.rsqrt(var + LN_EPS) * g + b


def _mix_kernel(a_ref, r_ref, x_ref, wo_ref, g_ref, b_ref, o_ref):
    mix = (jnp.dot(a_ref[...], wo_ref[:DSA_WIDTH, :], preferred_element_type=F32)
           + jnp.dot(r_ref[...], wo_ref[DSA_WIDTH:, :], preferred_element_type=F32))
    o_ref[...] = _layer_norm(ALPHA * x_ref[...] + mix, g_ref[...], b_ref[...])


def _mix(a, r, x2, wo, g, b):
    n = x2.shape[0]
    half = pl.BlockSpec((MIX_ROWS, 512), lambda i: (i, 0))
    full = pl.BlockSpec((MIX_ROWS, D_MODEL), lambda i: (i, 0))
    vec = pl.BlockSpec((1, D_MODEL), lambda i: (0, 0))
    return pl.pallas_call(
        _mix_kernel,
        grid=(n // MIX_ROWS,),
        in_specs=[half, half, full, pl.BlockSpec(wo.shape, lambda i: (0, 0)), vec, vec],
        out_specs=full,
        out_shape=jax.ShapeDtypeStruct((n, D_MODEL), F32),
        compiler_params=pltpu.CompilerParams(
            dimension_semantics=("arbitrary",), vmem_limit_bytes=VMEM_LIMIT_BYTES),
        name="mix_ln",
    )(a, r, x2, wo, g, b)


def _ffn_kernel(x_ref, wg_ref, wu_ref, wd_ref, g_ref, b_ref, o_ref):
    x = x_ref[...]
    xb = x.astype(BF16)
    gate = jnp.dot(xb, wg_ref[...], preferred_element_type=F32)
    up = jnp.dot(xb, wu_ref[...], preferred_element_type=F32)
    h = (gate * jax.nn.sigmoid(gate) * up).astype(BF16)
    ff = jnp.dot(h, wd_ref[...], preferred_element_type=F32)
    o_ref[...] = _layer_norm(ALPHA * x + ff, g_ref[...], b_ref[...])


def _ffn(x1, wg, wu, wd, g, b):
    n = x1.shape[0]
    full = pl.BlockSpec((FFN_ROWS, D_MODEL), lambda i: (i, 0))
    vec = pl.BlockSpec((1, D_MODEL), lambda i: (0, 0))
    const = lambda w: pl.BlockSpec(w.shape, lambda i: (0, 0), pipeline_mode=pl.Buffered(1))
    return pl.pallas_call(
        _ffn_kernel,
        grid=(n // FFN_ROWS,),
        in_specs=[full, const(wg), const(wu), const(wd), vec, vec],
        out_specs=full,
        out_shape=jax.ShapeDtypeStruct((n, D_MODEL), F32),
        compiler_params=pltpu.CompilerParams(
            dimension_semantics=("arbitrary",), vmem_limit_bytes=VMEM_LIMIT_BYTES),
        name="ffn_ln",
    )(x1, wg, wu, wd, g, b)


def _proj_weights(w):
    pts = [0]
    for s in SPLIT_SIZES:
        pts.append(pts[-1] + s)
    col = lambda j: w[:, pts[j]:pts[j + 1]]
    wn = jnp.concatenate([col(1), col(6), col(7), col(8), col(9), col(4), col(4)], axis=1)
    wt = jnp.concatenate(
        [col(0), col(3), col(2), jnp.pad(col(5), ((0, 0), (0, IW_ROWS - IDX_HEADS)))], axis=1).T
    return wn.astype(BF16), wt.astype(BF16)


def kernel(x, w_in, w_out, hg_lb_logits, hg_norm_g, ln1_g, ln1_b, w_gate, w_up, w_down, ln2_g, ln2_b):
    batch, seq, _ = x.shape
    n = batch * seq
    lb_all = jnp.cumsum(jax.nn.softmax(hg_lb_logits.astype(F32), axis=0), axis=0)

    x2 = x.reshape(n, D_MODEL)
    for l in range(DEPTH):
        wn, wt = _proj_weights(w_in[l])
        k, ik, hq, hf, hi, hg, qt, iqt, vt, iwt = _project(x2, wn, wt, batch, seq)
        a = _dsa(qt, iqt, iwt, k.reshape(batch, seq, 512), vt,
                 ik.reshape(batch, seq, 2 * IDX_DIM), batch, seq)
        r = _hgrn(hq, hf, hi, hg, lb_all[l].reshape(1, 512), hg_norm_g[l].reshape(1, 512).astype(F32),
                  batch, seq)
        x1 = _mix(a, r, x2, w_out[l].astype(BF16), ln1_g[l].reshape(1, D_MODEL),
                  ln1_b[l].reshape(1, D_MODEL))
        x2 = _ffn(x1, w_gate[l].astype(BF16), w_up[l].astype(BF16), w_down[l].astype(BF16),
                  ln2_g[l].reshape(1, D_MODEL), ln2_b[l].reshape(1, D_MODEL))
    return x2.reshape(batch, seq, D_MODEL)
```

```python
import numpy as np
import jax
import jax.numpy as jnp
from jax import lax
from jax.experimental import pallas as pl
from jax.experimental.pallas import tpu as pltpu

D_MODEL = 1024
CHUNK = 64
DSA_WIDTH = 512
DSA_HEAD_DIM = 64
DSA_HEADS = 8
IDX_HEADS = 8
IDX_DIM = 64
TOPK_MAX = 256
HG_WIDTH = 512
HG_DK = 128
HG_HEADS = 4
D_FF = 2816
DEPTH = 1
ALPHA = (2.0 * DEPTH) ** 0.25
LN_EPS = 1e-5
RMS_EPS = 1e-6
SPLIT_SIZES = (512, 512, 512, 512, 64, 8, 512, 512, 512, 512)

F32 = jnp.float32
BF16 = jnp.bfloat16

LANES = 128
SUBLANES = 8
VMEM_LIMIT_BYTES = 56 * 1024 * 1024

Q_TILE = 256
KEY_TILE = 256
IW_ROWS = 16
PROJ_ROWS = 512
HG_ROWS = 512
MIX_ROWS = 512
FFN_ROWS = 256
MASK_BIG = 1e30
LOWEST = -3.0e38
BISECT_VALUE_STEPS = 8
LOG2E = 1.4426950408889634
V_ROWS = DSA_HEAD_DIM + 16

NT_DIMS = (((1,), (1,)), ((), ()))
TN_DIMS = (((0,), (0,)), ((), ()))


def _proj_kernel(x_ref, wn_ref, wt_ref, k_ref, ik_ref, hq_ref, hf_ref, hi_ref, hg_ref,
                 qt_ref, iqt_ref, vt_ref, iwt_ref):
    xb = x_ref[...].astype(BF16)

    def nn(j, width=512):
        return jnp.dot(xb, wn_ref[:, j * 512:j * 512 + width], preferred_element_type=F32)

    k_ref[...] = nn(0).astype(BF16)
    hq_ref[...] = nn(1)
    hf_ref[...] = nn(2)
    hi_ref[...] = nn(3).astype(BF16)
    hg_ref[...] = nn(4).astype(BF16)
    ik_ref[...] = nn(5, 2 * IDX_DIM).astype(BF16)
    t = lax.dot_general(wt_ref[...], xb, NT_DIMS, preferred_element_type=F32)
    for j in range(PROJ_ROWS // Q_TILE):
        cols = slice(j * Q_TILE, (j + 1) * Q_TILE)
        qt_ref[j] = (t[0:512, cols] * (LOG2E * DSA_HEAD_DIM ** -0.5)).astype(BF16)
        iqt_ref[j] = t[512:1024, cols].astype(BF16)
        for h in range(DSA_HEADS):
            lo = 1024 + h * DSA_HEAD_DIM
            vt_ref[j, h, :DSA_HEAD_DIM, :] = t[lo:lo + DSA_HEAD_DIM, cols].astype(BF16)
            vt_ref[j, h, DSA_HEAD_DIM:, :] = jnp.ones((V_ROWS - DSA_HEAD_DIM, Q_TILE), BF16)
        iwt_ref[j] = t[1536:1536 + IW_ROWS, cols]


def _project(x2, wn, wt, batch, seq):
    n = x2.shape[0]
    nb = seq // PROJ_ROWS
    tiles = PROJ_ROWS // Q_TILE
    n_tiles = seq // Q_TILE
    row_spec = lambda w: pl.BlockSpec((PROJ_ROWS, w), lambda i: (i, 0))
    t_shape = lambda rows, dt: jax.ShapeDtypeStruct((batch, n_tiles, rows, Q_TILE), dt)
    t_spec = lambda rows: pl.BlockSpec((None, tiles, rows, Q_TILE), lambda i: (i // nb, i % nb, 0, 0))
    out_shape = (
        jax.ShapeDtypeStruct((n, 512), BF16),
        jax.ShapeDtypeStruct((n, 2 * IDX_DIM), BF16),
        jax.ShapeDtypeStruct((n, 512), F32),
        jax.ShapeDtypeStruct((n, 512), F32),
        jax.ShapeDtypeStruct((n, 512), BF16),
        jax.ShapeDtypeStruct((n, 512), BF16),
        t_shape(512, BF16),
        t_shape(512, BF16),
        jax.ShapeDtypeStruct((batch, n_tiles, DSA_HEADS, V_ROWS, Q_TILE), BF16),
        t_shape(IW_ROWS, F32),
    )
    out_specs = (
        row_spec(512), row_spec(2 * IDX_DIM), row_spec(512), row_spec(512), row_spec(512),
        row_spec(512), t_spec(512), t_spec(512),
        pl.BlockSpec((None, tiles, DSA_HEADS, V_ROWS, Q_TILE), lambda i: (i // nb, i % nb, 0, 0, 0)),
        t_spec(IW_ROWS),
    )
    return pl.pallas_call(
        _proj_kernel,
        grid=(n // PROJ_ROWS,),
        in_specs=[
            pl.BlockSpec((PROJ_ROWS, D_MODEL), lambda i: (i, 0)),
            pl.BlockSpec(wn.shape, lambda i: (0, 0)),
            pl.BlockSpec(wt.shape, lambda i: (0, 0)),
        ],
        out_specs=out_specs,
        out_shape=out_shape,
        compiler_params=pltpu.CompilerParams(
            dimension_semantics=("arbitrary",), vmem_limit_bytes=VMEM_LIMIT_BYTES),
        name="proj",
    )(x2, wn, wt)


def _fold_rows(x, op):
    return op(x.reshape(x.shape[0] // SUBLANES, SUBLANES, x.shape[1]), axis=0)


def _split3(c):
    out = []
    for _ in range(3):
        t = float(np.asarray(c, dtype=BF16))
        out.append(t)
        c = c - t
    return out


def _dsa_kernel(qt_ref, iqt_ref, iwt_ref, k_ref, vt_ref, ik_ref, o_ref,
                score_ref, qm_ref, iqm_ref, pos_ref, s_ref, acc_ref, m_ref):
    i = pl.program_id(1)
    topk = TOPK_MAX
    idx_scale = (IDX_DIM ** -0.5) * (IDX_HEADS ** -0.5)
    slopes = [2.0 ** (-8.0 * (h + 1) / DSA_HEADS) for h in range(DSA_HEADS)]

    row_i = lax.broadcasted_iota(jnp.int32, (LANES, Q_TILE), 0)
    even_rows = row_i < DSA_HEAD_DIM
    for p in range(DSA_HEADS // 2):
        rows = slice(p * LANES, (p + 1) * LANES)
        qp = qt_ref[rows, :].astype(F32)
        iqp = iqt_ref[rows, :].astype(F32)
        qm_ref[2 * p, :LANES, :] = jnp.where(even_rows, qp, 0.0).astype(BF16)
        qm_ref[2 * p + 1, :LANES, :] = jnp.where(even_rows, 0.0, qp).astype(BF16)
        iqm_ref[2 * p] = jnp.where(even_rows, iqp, 0.0).astype(BF16)
        iqm_ref[2 * p + 1] = jnp.where(even_rows, 0.0, iqp).astype(BF16)
    for h in range(DSA_HEADS):
        c0, c1, c2 = _split3(LOG2E * slopes[h])
        coef = jnp.where(row_i == 0, c0, jnp.where(row_i == 1, c1, jnp.where(row_i == 2, c2, 0.0)))
        qm_ref[h, LANES:, :] = coef.astype(BF16)
    pos_ref[...] = jnp.where(
        lax.broadcasted_iota(jnp.int32, (KEY_TILE, LANES), 1) < 3,
        lax.broadcasted_iota(jnp.int32, (KEY_TILE, LANES), 0), 0).astype(F32).astype(BF16)

    w = iwt_ref[...]
    q_iota = lax.broadcasted_iota(jnp.int32, (1, Q_TILE), 1)
    key_iota = lax.broadcasted_iota(jnp.int32, (KEY_TILE, 1), 0)
    qpos = i * Q_TILE + q_iota
    limit = (qpos // CHUNK + 1) * CHUNK

    def score_tile(kt, carry, diagonal):
        rmax, rmin = carry
        koff = pl.multiple_of(kt * KEY_TILE, KEY_TILE)
        ik = ik_ref[pl.ds(koff, KEY_TILE), :]
        acc = jnp.zeros((KEY_TILE, Q_TILE), F32)
        for h in range(IDX_HEADS):
            s = jnp.dot(ik, iqm_ref[h], preferred_element_type=F32)
            acc = acc + w[h:h + 1, :] * jnp.maximum(s, 0.0)
        acc = acc * idx_scale
        if diagonal:
            adm = (kt * KEY_TILE + key_iota) < limit
            lo_fill = jnp.where(adm, acc, -jnp.inf)
            hi_fill = jnp.where(adm, acc, jnp.inf)
        else:
            lo_fill = hi_fill = acc
        score_ref[kt] = lo_fill
        return (jnp.maximum(rmax, _fold_rows(lo_fill, jnp.max)),
                jnp.minimum(rmin, _fold_rows(hi_fill, jnp.min)))

    carry = lax.fori_loop(
        0, i, lambda kt, c: score_tile(kt, c, False),
        (jnp.full((SUBLANES, Q_TILE), -jnp.inf, F32), jnp.full((SUBLANES, Q_TILE), jnp.inf, F32)))
    rmax, rmin = score_tile(i, carry, True)
    n_tiles = i + 1
    col_max = jnp.max(rmax, axis=0, keepdims=True)
    col_min = jnp.min(rmin, axis=0, keepdims=True)

    def to_key(v):
        b = lax.bitcast_convert_type(v, jnp.int32)
        return jnp.where(b < 0, b ^ 0x7FFFFFFF, b)

    def from_key(kv):
        return lax.bitcast_convert_type(jnp.where(kv < 0, kv ^ 0x7FFFFFFF, kv), F32)

    def count(pred):
        def body(kt, acc):
            return acc + _fold_rows(jnp.where(pred(score_ref[kt]), 1.0, 0.0), jnp.sum)
        acc = lax.fori_loop(0, n_tiles, body, jnp.zeros((SUBLANES, Q_TILE), F32))
        return jnp.sum(acc, axis=0, keepdims=True)

    need = limit > topk
    active0 = need.astype(jnp.int32)

    def bis_cond(c):
        return c[4] > 0

    def bis_body(c):
        lo, hi, c_lo, active, _, it = c
        key_mid = (lo & hi) + ((lo ^ hi) >> 1) + ((lo ^ hi) & 1)
        lo_f = from_key(lo)
        val_mid = to_key(lo_f + 0.5 * (from_key(hi) - lo_f))
        val_mid = jnp.minimum(jnp.maximum(val_mid, lo + 1), hi)
        mid = jnp.where(it < BISECT_VALUE_STEPS, val_mid, key_mid)
        mid_f = from_key(mid)
        cnt = count(lambda sc: sc >= mid_f)
        ge = jnp.logical_and(active > 0, cnt >= topk)
        lt = jnp.logical_and(active > 0, cnt < topk)
        lo = jnp.where(ge, mid, lo)
        c_lo = jnp.where(ge, cnt, c_lo)
        hi = jnp.where(lt, mid, hi)
        finished = jnp.logical_or(cnt == topk, hi == lo + 1)
        active = jnp.where(finished, 0, active)
        return lo, hi, c_lo, active, jnp.sum(active), it + 1

    lo, _, c_lo, _, _, _ = lax.while_loop(
        bis_cond, bis_body,
        (to_key(col_min), to_key(col_max) + 1, limit.astype(F32), active0, jnp.sum(active0),
         jnp.int32(0)))
    thr = jnp.where(need, from_key(lo), LOWEST)

    n_tied_rows = jnp.sum(jnp.logical_and(need, c_lo != topk).astype(jnp.int32))

    @pl.when(n_tied_rows > 0)
    def _():
        quota = topk - count(lambda sc: sc > thr)
        r_i = lax.broadcasted_iota(jnp.int32, (KEY_TILE, KEY_TILE), 0)
        c_i = lax.broadcasted_iota(jnp.int32, (KEY_TILE, KEY_TILE), 1)
        strict_lower = jnp.where(r_i > c_i, 1.0, 0.0).astype(BF16)

        def demote_tile(kt, before):
            sc = score_ref[kt]
            tie = sc == thr
            tie_b = jnp.where(tie, 1.0, 0.0).astype(BF16)
            rank = before + jnp.dot(strict_lower, tie_b, preferred_element_type=F32)
            score_ref[kt] = jnp.where(jnp.logical_and(tie, rank >= quota), -jnp.inf, sc)
            return before + jnp.sum(_fold_rows(jnp.where(tie, 1.0, 0.0), jnp.sum),
                                    axis=0, keepdims=True)

        lax.fori_loop(0, n_tiles, demote_tile, jnp.zeros((1, Q_TILE), F32))

    m_ref[...] = jnp.full(m_ref.shape, -jnp.inf, F32)
    acc_ref[...] = jnp.zeros(acc_ref.shape, F32)

    def attn_tile(kt, diagonal):
        koff = pl.multiple_of(kt * KEY_TILE, KEY_TILE)
        mask_bias = jnp.where(score_ref[kt] >= thr, 0.0, -MASK_BIG)
        if diagonal:
            ahead = (2.0 * LOG2E) * jnp.maximum(key_iota - q_iota, 0).astype(F32)
        rel_q = (qpos - kt * KEY_TILE).astype(F32)
        pos = pos_ref[...]
        alphas, shifts = [], []

        def logits(h):
            pair = slice((h // 2) * LANES, (h // 2 + 1) * LANES)
            lhs = jnp.concatenate([k_ref[pl.ds(koff, KEY_TILE), pair], pos], axis=1)
            s = jnp.dot(lhs, qm_ref[h], preferred_element_type=F32) + mask_bias
            if diagonal:
                s = s - slopes[h] * ahead
            s_ref[h] = s
            offset = (LOG2E * slopes[h]) * rel_q
            m_old = m_ref[h:h + 1, :]
            m_new = jnp.maximum(m_old, jnp.max(_fold_rows(s, jnp.max), axis=0, keepdims=True) - offset)
            alphas.append(jnp.exp2(m_old - m_new))
            shifts.append(m_new + offset)
            m_ref[h:h + 1, :] = m_new

        def weighted_values(h):
            p = jnp.exp2(s_ref[h] - shifts[h]).astype(BF16)
            pv = jnp.dot(vt_ref[kt, h], p, preferred_element_type=F32)
            acc_ref[h] = alphas[h] * acc_ref[h] + pv

        for h in range(DSA_HEADS):
            logits(h)
        for h in range(DSA_HEADS):
            weighted_values(h)

    def attn_body(kt, carry):
        attn_tile(kt, False)
        return carry

    lax.fori_loop(0, i, attn_body, 0)
    attn_tile(i, True)

    outs = [acc_ref[h, :DSA_HEAD_DIM, :] / acc_ref[h, DSA_HEAD_DIM:DSA_HEAD_DIM + 1, :]
            for h in range(DSA_HEADS)]
    o_ref[...] = jnp.concatenate(outs, axis=0).T.astype(o_ref.dtype)


def _dsa(qt, iqt, iwt, k3, vt, ik3, batch, seq):
    assert Q_TILE == KEY_TILE and Q_TILE % CHUNK == 0
    n = batch * seq
    nqb = seq // Q_TILE
    n_key_tiles = seq // KEY_TILE
    t_spec = lambda rows: pl.BlockSpec((None, None, rows, Q_TILE), lambda b, i: (b, i, 0, 0))
    return pl.pallas_call(
        _dsa_kernel,
        grid=(batch, nqb),
        in_specs=[
            t_spec(512), t_spec(512), t_spec(IW_ROWS),
            pl.BlockSpec((None, seq, 512), lambda b, i: (b, 0, 0)),
            pl.BlockSpec((None, n_key_tiles, DSA_HEADS, V_ROWS, KEY_TILE), lambda b, i: (b, 0, 0, 0, 0)),
            pl.BlockSpec((None, seq, 2 * IDX_DIM), lambda b, i: (b, 0, 0)),
        ],
        out_specs=pl.BlockSpec((Q_TILE, 512), lambda b, i: (b * nqb + i, 0)),
        out_shape=jax.ShapeDtypeStruct((n, 512), BF16),
        scratch_shapes=[
            pltpu.VMEM((n_key_tiles, KEY_TILE, Q_TILE), F32),
            pltpu.VMEM((DSA_HEADS, 2 * LANES, Q_TILE), BF16),
            pltpu.VMEM((IDX_HEADS, LANES, Q_TILE), BF16),
            pltpu.VMEM((KEY_TILE, LANES), BF16),
            pltpu.VMEM((DSA_HEADS, KEY_TILE, Q_TILE), F32),
            pltpu.VMEM((DSA_HEADS, V_ROWS, Q_TILE), F32),
            pltpu.VMEM((DSA_HEADS, Q_TILE), F32),
        ],
        compiler_params=pltpu.CompilerParams(
            dimension_semantics=("arbitrary", "arbitrary"), vmem_limit_bytes=VMEM_LIMIT_BYTES),
        name="dsa",
    )(qt, iqt, iwt, k3, vt, ik3)


def _hgrn_kernel(hq_ref, hf_ref, hi_ref, hg_ref, lb_ref, gain_ref, r_ref, st_ref):
    @pl.when(pl.program_id(1) == 0)
    def _():
        st_ref[...] = jnp.zeros(st_ref.shape, F32)

    r_i = lax.broadcasted_iota(jnp.int32, (CHUNK, CHUNK), 0)
    c_i = lax.broadcasted_iota(jnp.int32, (CHUNK, CHUNK), 1)
    tril = r_i >= c_i
    tril_b = jnp.where(tril, 1.0, 0.0).astype(BF16)
    lb = lb_ref[...]
    gain = gain_ref[...]

    def chunk(c, carry):
        off = pl.multiple_of(c * CHUNK, CHUNK)
        rows = pl.ds(off, CHUNK)
        f = lb + (1.0 - lb) * jax.nn.sigmoid(hf_ref[rows, :])
        logf = jnp.log(f)
        kk = 1.0 - f
        t0 = logf.astype(BF16)
        r1 = logf - t0.astype(F32)
        t1 = r1.astype(BF16)
        t2 = (r1 - t1.astype(F32)).astype(BF16)
        g = (jnp.dot(tril_b, t0, preferred_element_type=F32)
             + jnp.dot(tril_b, t1, preferred_element_type=F32)
             + jnp.dot(tril_b, t2, preferred_element_type=F32))
        g_last = g[CHUNK - 1:CHUNK, :]
        g_mid = g[CHUNK // 2 - 1:CHUNK // 2, :]
        q = hq_ref[rows, :]
        q_state = (q * jnp.exp(g)).astype(BF16)
        q_intra = (q * jnp.exp(g - g_mid)).astype(BF16)
        k_intra = (kk * jnp.exp(g_mid - g)).astype(BF16)
        k_state = (kk * jnp.exp(g_last - g)).astype(BF16)
        decay = jnp.exp(g_last)
        v = hi_ref[rows, :]
        for h in range(HG_HEADS):
            sl = slice(h * HG_DK, (h + 1) * HG_DK)
            a = lax.dot_general(q_intra[:, sl], k_intra[:, sl], NT_DIMS,
                                preferred_element_type=F32)
            a = jnp.where(tril, a, 0.0)
            st = st_ref[h]
            o = (jnp.dot(a.astype(BF16), v[:, sl], preferred_element_type=F32)
                 + lax.dot_general(q_state[:, sl], st.astype(BF16), NT_DIMS,
                                   preferred_element_type=F32))
            st_ref[h] = st * decay[:, sl] + lax.dot_general(
                v[:, sl], k_state[:, sl], TN_DIMS, preferred_element_type=F32)
            ms = jnp.mean(o * o, axis=1, keepdims=True)
            on = o * lax.rsqrt(ms + RMS_EPS)
            gate = hg_ref[rows, sl].astype(F32)
            r = on * gain[:, sl] * (gate * jax.nn.sigmoid(gate))
            r_ref[rows, sl] = r.astype(r_ref.dtype)
        return carry

    lax.fori_loop(0, HG_ROWS // CHUNK, chunk, 0)


def _hgrn(hq, hf, hi, hg, lb, gain, batch, seq):
    n = hq.shape[0]
    nb = seq // HG_ROWS
    row_spec = pl.BlockSpec((HG_ROWS, 512), lambda b, t: (b * nb + t, 0))
    vec_spec = pl.BlockSpec((1, 512), lambda b, t: (0, 0))
    return pl.pallas_call(
        _hgrn_kernel,
        grid=(batch, nb),
        in_specs=[row_spec, row_spec, row_spec, row_spec, vec_spec, vec_spec],
        out_specs=row_spec,
        out_shape=jax.ShapeDtypeStruct((n, 512), BF16),
        scratch_shapes=[pltpu.VMEM((HG_HEADS, HG_DK, HG_DK), F32)],
        compiler_params=pltpu.CompilerParams(
            dimension_semantics=("arbitrary", "arbitrary"), vmem_limit_bytes=VMEM_LIMIT_BYTES),
        name="hgrn",
    )(hq, hf, hi, hg, lb, gain)


def _layer_norm(y, g, b):
    mu = jnp.mean(y, axis=1, keepdims=True)
    d = y - mu
    var = jnp.mean(d * d, axis=1, keepdims=True)
    return d * lax.rsqrt(var + LN_EPS) * g + b


def _mix_kernel(a_ref, r_ref, x_ref, wo_ref, g_ref, b_ref, o_ref):
    mix = (jnp.dot(a_ref[...], wo_ref[:DSA_WIDTH, :], preferred_element_type=F32)
           + jnp.dot(r_ref[...], wo_ref[DSA_WIDTH:, :], preferred_element_type=F32))
    o_ref[...] = _layer_norm(ALPHA * x_ref[...] + mix, g_ref[...], b_ref[...])


def _mix(a, r, x2, wo, g, b):
    n = x2.shape[0]
    half = pl.BlockSpec((MIX_ROWS, 512), lambda i: (i, 0))
    full = pl.BlockSpec((MIX_ROWS, D_MODEL), lambda i: (i, 0))
    vec = pl.BlockSpec((1, D_MODEL), lambda i: (0, 0))
    return pl.pallas_call(
        _mix_kernel,
        grid=(n // MIX_ROWS,),
        in_specs=[half, half, full, pl.BlockSpec(wo.shape, lambda i: (0, 0)), vec, vec],
        out_specs=full,
        out_shape=jax.ShapeDtypeStruct((n, D_MODEL), F32),
        compiler_params=pltpu.CompilerParams(
            dimension_semantics=("arbitrary",), vmem_limit_bytes=VMEM_LIMIT_BYTES),
        name="mix_ln",
    )(a, r, x2, wo, g, b)


def _ffn_kernel(x_ref, wg_ref, wu_ref, wd_ref, g_ref, b_ref, o_ref):
    x = x_ref[...]
    xb = x.astype(BF16)
    gate = jnp.dot(xb, wg_ref[...], preferred_element_type=F32)
    up = jnp.dot(xb, wu_ref[...], preferred_element_type=F32)
    h = (gate * jax.nn.sigmoid(gate) * up).astype(BF16)
    ff = jnp.dot(h, wd_ref[...], preferred_element_type=F32)
    o_ref[...] = _layer_norm(ALPHA * x + ff, g_ref[...], b_ref[...])


def _ffn(x1, wg, wu, wd, g, b):
    n = x1.shape[0]
    full = pl.BlockSpec((FFN_ROWS, D_MODEL), lambda i: (i, 0))
    vec = pl.BlockSpec((1, D_MODEL), lambda i: (0, 0))
    const = lambda w: pl.BlockSpec(w.shape, lambda i: (0, 0), pipeline_mode=pl.Buffered(1))
    return pl.pallas_call(
        _ffn_kernel,
        grid=(n // FFN_ROWS,),
        in_specs=[full, const(wg), const(wu), const(wd), vec, vec],
        out_specs=full,
        out_shape=jax.ShapeDtypeStruct((n, D_MODEL), F32),
        compiler_params=pltpu.CompilerParams(
            dimension_semantics=("arbitrary",), vmem_limit_bytes=VMEM_LIMIT_BYTES),
        name="ffn_ln",
    )(x1, wg, wu, wd, g, b)


def _proj_weights(w):
    pts = [0]
    for s in SPLIT_SIZES:
        pts.append(pts[-1] + s)
    col = lambda j: w[:, pts[j]:pts[j + 1]]
    wn = jnp.concatenate([col(1), col(6), col(7), col(8), col(9), col(4), col(4)], axis=1)
    wt = jnp.concatenate(
        [col(0), col(3), col(2), jnp.pad(col(5), ((0, 0), (0, IW_ROWS - IDX_HEADS)))], axis=1).T
    return wn.astype(BF16), wt.astype(BF16)


def kernel(x, w_in, w_out, hg_lb_logits, hg_norm_g, ln1_g, ln1_b, w_gate, w_up, w_down, ln2_g, ln2_b):
    batch, seq, _ = x.shape
    n = batch * seq
    lb_all = jnp.cumsum(jax.nn.softmax(hg_lb_logits.astype(F32), axis=0), axis=0)

    x2 = x.reshape(n, D_MODEL)
    for l in range(DEPTH):
        wn, wt = _proj_weights(w_in[l])
        k, ik, hq, hf, hi, hg, qt, iqt, vt, iwt = _project(x2, wn, wt, batch, seq)
        a = _dsa(qt, iqt, iwt, k.reshape(batch, seq, 512), vt,
                 ik.reshape(batch, seq, 2 * IDX_DIM), batch, seq)
        r = _hgrn(hq, hf, hi, hg, lb_all[l].reshape(1, 512), hg_norm_g[l].reshape(1, 512).astype(F32),
                  batch, seq)
        x1 = _mix(a, r, x2, w_out[l].astype(BF16), ln1_g[l].reshape(1, D_MODEL),
                  ln1_b[l].reshape(1, D_MODEL))
        x2 = _ffn(x1, w_gate[l].astype(BF16), w_up[l].astype(BF16), w_down[l].astype(BF16),
                  ln2_g[l].reshape(1, D_MODEL), ln2_b[l].reshape(1, D_MODEL))
    return x2.reshape(batch, seq, D_MODEL)
```

```python
import numpy as np
import jax
import jax.numpy as jnp
from jax import lax
from jax.experimental import pallas as pl
from jax.experimental.pallas import tpu as pltpu

D_MODEL = 1024
CHUNK = 64
DSA_WIDTH = 512
DSA_HEAD_DIM = 64
DSA_HEADS = 8
IDX_HEADS = 8
IDX_DIM = 64
TOPK_MAX = 256
HG_WIDTH = 512
HG_DK = 128
HG_HEADS = 4
D_FF = 2816
DEPTH = 1
ALPHA = (2.0 * DEPTH) ** 0.25
LN_EPS = 1e-5
RMS_EPS = 1e-6
SPLIT_SIZES = (512, 512, 512, 512, 64, 8, 512, 512, 512, 512)

F32 = jnp.float32
BF16 = jnp.bfloat16

LANES = 128
SUBLANES = 8
VMEM_LIMIT_BYTES = 56 * 1024 * 1024

Q_TILE = 256
KEY_TILE = 256
IW_ROWS = 16
PROJ_ROWS = 512
HG_ROWS = 512
MIX_ROWS = 512
FFN_ROWS = 256
MASK_BIG = 1e30
LOWEST = -3.0e38
BISECT_VALUE_STEPS = 8
BISECT_STEPS_PER_CHECK = 3
LOG2E = 1.4426950408889634
V_ROWS = DSA_HEAD_DIM + 16

NT_DIMS = (((1,), (1,)), ((), ()))
TN_DIMS = (((0,), (0,)), ((), ()))


def _proj_kernel(x_ref, wn_ref, wt_ref, k_ref, ik_ref, hq_ref, hf_ref, hi_ref, hg_ref,
                 qt_ref, iqt_ref, vt_ref, iwt_ref):
    xb = x_ref[...].astype(BF16)

    def nn(j, width=512):
        return jnp.dot(xb, wn_ref[:, j * 512:j * 512 + width], preferred_element_type=F32)

    k_ref[...] = nn(0).astype(BF16)
    hq_ref[...] = nn(1)
    hf_ref[...] = nn(2)
    hi_ref[...] = nn(3).astype(BF16)
    hg_ref[...] = nn(4).astype(BF16)
    ik_ref[...] = nn(5, 2 * IDX_DIM).astype(BF16)
    t = lax.dot_general(wt_ref[...], xb, NT_DIMS, preferred_element_type=F32)
    for j in range(PROJ_ROWS // Q_TILE):
        cols = slice(j * Q_TILE, (j + 1) * Q_TILE)
        qt_ref[j] = (t[0:512, cols] * (LOG2E * DSA_HEAD_DIM ** -0.5)).astype(BF16)
        iqt_ref[j] = t[512:1024, cols].astype(BF16)
        for h in range(DSA_HEADS):
            lo = 1024 + h * DSA_HEAD_DIM
            vt_ref[j, h, :DSA_HEAD_DIM, :] = t[lo:lo + DSA_HEAD_DIM, cols].astype(BF16)
            vt_ref[j, h, DSA_HEAD_DIM:, :] = jnp.ones((V_ROWS - DSA_HEAD_DIM, Q_TILE), BF16)
        iwt_ref[j] = t[1536:1536 + IW_ROWS, cols]


def _project(x2, wn, wt, batch, seq):
    n = x2.shape[0]
    nb = seq // PROJ_ROWS
    tiles = PROJ_ROWS // Q_TILE
    n_tiles = seq // Q_TILE
    row_spec = lambda w: pl.BlockSpec((PROJ_ROWS, w), lambda i: (i, 0))
    t_shape = lambda rows, dt: jax.ShapeDtypeStruct((batch, n_tiles, rows, Q_TILE), dt)
    t_spec = lambda rows: pl.BlockSpec((None, tiles, rows, Q_TILE), lambda i: (i // nb, i % nb, 0, 0))
    out_shape = (
        jax.ShapeDtypeStruct((n, 512), BF16),
        jax.ShapeDtypeStruct((n, 2 * IDX_DIM), BF16),
        jax.ShapeDtypeStruct((n, 512), F32),
        jax.ShapeDtypeStruct((n, 512), F32),
        jax.ShapeDtypeStruct((n, 512), BF16),
        jax.ShapeDtypeStruct((n, 512), BF16),
        t_shape(512, BF16),
        t_shape(512, BF16),
        jax.ShapeDtypeStruct((batch, n_tiles, DSA_HEADS, V_ROWS, Q_TILE), BF16),
        t_shape(IW_ROWS, F32),
    )
    out_specs = (
        row_spec(512), row_spec(2 * IDX_DIM), row_spec(512), row_spec(512), row_spec(512),
        row_spec(512), t_spec(512), t_spec(512),
        pl.BlockSpec((None, tiles, DSA_HEADS, V_ROWS, Q_TILE), lambda i: (i // nb, i % nb, 0, 0, 0)),
        t_spec(IW_ROWS),
    )
    return pl.pallas_call(
        _proj_kernel,
        grid=(n // PROJ_ROWS,),
        in_specs=[
            pl.BlockSpec((PROJ_ROWS, D_MODEL), lambda i: (i, 0)),
            pl.BlockSpec(wn.shape, lambda i: (0, 0)),
            pl.BlockSpec(wt.shape, lambda i: (0, 0)),
        ],
        out_specs=out_specs,
        out_shape=out_shape,
        compiler_params=pltpu.CompilerParams(
            dimension_semantics=("arbitrary",), vmem_limit_bytes=VMEM_LIMIT_BYTES),
        name="proj",
    )(x2, wn, wt)


def _fold_rows(x, op):
    return op(x.reshape(x.shape[0] // SUBLANES, SUBLANES, x.shape[1]), axis=0)


def _split3(c):
    out = []
    for _ in range(3):
        t = float(np.asarray(c, dtype=BF16))
        out.append(t)
        c = c - t
    return out


def _dsa_kernel(qt_ref, iqt_ref, iwt_ref, k_ref, vt_ref, ik_ref, o_ref,
                score_ref, qm_ref, iqm_ref, pos_ref, s_ref, acc_ref, m_ref):
    i = pl.program_id(1)
    topk = TOPK_MAX
    idx_scale = (IDX_DIM ** -0.5) * (IDX_HEADS ** -0.5)
    slopes = [2.0 ** (-8.0 * (h + 1) / DSA_HEADS) for h in range(DSA_HEADS)]

    row_i = lax.broadcasted_iota(jnp.int32, (LANES, Q_TILE), 0)
    even_rows = row_i < DSA_HEAD_DIM
    for p in range(DSA_HEADS // 2):
        rows = slice(p * LANES, (p + 1) * LANES)
        qp = qt_ref[rows, :].astype(F32)
        iqp = iqt_ref[rows, :].astype(F32)
        qm_ref[2 * p, :LANES, :] = jnp.where(even_rows, qp, 0.0).astype(BF16)
        qm_ref[2 * p + 1, :LANES, :] = jnp.where(even_rows, 0.0, qp).astype(BF16)
        iqm_ref[2 * p] = jnp.where(even_rows, iqp, 0.0).astype(BF16)
        iqm_ref[2 * p + 1] = jnp.where(even_rows, 0.0, iqp).astype(BF16)
    for h in range(DSA_HEADS):
        c0, c1, c2 = _split3(LOG2E * slopes[h])
        coef = jnp.where(row_i == 0, c0, jnp.where(row_i == 1, c1, jnp.where(row_i == 2, c2, 0.0)))
        qm_ref[h, LANES:, :] = coef.astype(BF16)
    pos_ref[...] = jnp.where(
        lax.broadcasted_iota(jnp.int32, (KEY_TILE, LANES), 1) < 3,
        lax.broadcasted_iota(jnp.int32, (KEY_TILE, LANES), 0), 0).astype(F32).astype(BF16)

    w = iwt_ref[...]
    q_iota = lax.broadcasted_iota(jnp.int32, (1, Q_TILE), 1)
    key_iota = lax.broadcasted_iota(jnp.int32, (KEY_TILE, 1), 0)
    qpos = i * Q_TILE + q_iota
    limit = (qpos // CHUNK + 1) * CHUNK

    def score_tile(kt, carry, diagonal):
        rmax, rmin = carry
        koff = pl.multiple_of(kt * KEY_TILE, KEY_TILE)
        ik = ik_ref[pl.ds(koff, KEY_TILE), :]
        acc = jnp.zeros((KEY_TILE, Q_TILE), F32)
        for h in range(IDX_HEADS):
            s = jnp.dot(ik, iqm_ref[h], preferred_element_type=F32)
            acc = acc + w[h:h + 1, :] * jnp.maximum(s, 0.0)
        acc = acc * idx_scale
        if diagonal:
            adm = (kt * KEY_TILE + key_iota) < limit
            lo_fill = jnp.where(adm, acc, -jnp.inf)
            hi_fill = jnp.where(adm, acc, jnp.inf)
        else:
            lo_fill = hi_fill = acc
        score_ref[kt] = lo_fill
        return (jnp.maximum(rmax, _fold_rows(lo_fill, jnp.max)),
                jnp.minimum(rmin, _fold_rows(hi_fill, jnp.min)))

    carry = lax.fori_loop(
        0, i, lambda kt, c: score_tile(kt, c, False),
        (jnp.full((SUBLANES, Q_TILE), -jnp.inf, F32), jnp.full((SUBLANES, Q_TILE), jnp.inf, F32)))
    rmax, rmin = score_tile(i, carry, True)
    n_tiles = i + 1
    col_max = jnp.max(rmax, axis=0, keepdims=True)
    col_min = jnp.min(rmin, axis=0, keepdims=True)

    def to_key(v):
        b = lax.bitcast_convert_type(v, jnp.int32)
        return jnp.where(b < 0, b ^ 0x7FFFFFFF, b)

    def from_key(kv):
        return lax.bitcast_convert_type(jnp.where(kv < 0, kv ^ 0x7FFFFFFF, kv), F32)

    @pl.when(n_tiles < score_ref.shape[0])
    def _():
        score_ref[n_tiles] = jnp.full((KEY_TILE, Q_TILE), -jnp.inf, F32)

    def count(pred):
        def body(j, acc):
            for kt in (2 * j, 2 * j + 1):
                acc = acc + _fold_rows(jnp.where(pred(score_ref[kt]), 1.0, 0.0), jnp.sum)
            return acc
        acc = lax.fori_loop(0, (n_tiles + 1) // 2, body, jnp.zeros((SUBLANES, Q_TILE), F32))
        return jnp.sum(acc, axis=0, keepdims=True)

    need = limit > topk
    active0 = need.astype(jnp.int32)

    def bis_cond(c):
        return c[4] > 0

    def bis_step(lo, hi, c_lo, active, it):
        key_mid = (lo & hi) + ((lo ^ hi) >> 1) + ((lo ^ hi) & 1)
        lo_f = from_key(lo)
        val_mid = to_key(lo_f + 0.5 * (from_key(hi) - lo_f))
        val_mid = jnp.minimum(jnp.maximum(val_mid, lo + 1), hi)
        mid = jnp.where(it < BISECT_VALUE_STEPS, val_mid, key_mid)
        mid_f = from_key(mid)
        cnt = count(lambda sc: sc >= mid_f)
        ge = jnp.logical_and(active > 0, cnt >= topk)
        lt = jnp.logical_and(active > 0, cnt < topk)
        lo = jnp.where(ge, mid, lo)
        c_lo = jnp.where(ge, cnt, c_lo)
        hi = jnp.where(lt, mid, hi)
        finished = jnp.logical_or(cnt == topk, hi == lo + 1)
        active = jnp.where(finished, 0, active)
        return lo, hi, c_lo, active

    def bis_body(c):
        lo, hi, c_lo, active, _, it = c
        for j in range(BISECT_STEPS_PER_CHECK):
            lo, hi, c_lo, active = bis_step(lo, hi, c_lo, active, it + j)
        return lo, hi, c_lo, active, jnp.sum(active), it + BISECT_STEPS_PER_CHECK

    lo, _, c_lo, _, _, _ = lax.while_loop(
        bis_cond, bis_body,
        (to_key(col_min), to_key(col_max) + 1, limit.astype(F32), active0, jnp.sum(active0),
         jnp.int32(0)))
    thr = jnp.where(need, from_key(lo), LOWEST)

    n_tied_rows = jnp.sum(jnp.logical_and(need, c_lo != topk).astype(jnp.int32))

    @pl.when(n_tied_rows > 0)
    def _():
        quota = topk - count(lambda sc: sc > thr)
        r_i = lax.broadcasted_iota(jnp.int32, (KEY_TILE, KEY_TILE), 0)
        c_i = lax.broadcasted_iota(jnp.int32, (KEY_TILE, KEY_TILE), 1)
        strict_lower = jnp.where(r_i > c_i, 1.0, 0.0).astype(BF16)

        def demote_tile(kt, before):
            sc = score_ref[kt]
            tie = sc == thr
            tie_b = jnp.where(tie, 1.0, 0.0).astype(BF16)
            rank = before + jnp.dot(strict_lower, tie_b, preferred_element_type=F32)
            score_ref[kt] = jnp.where(jnp.logical_and(tie, rank >= quota), -jnp.inf, sc)
            return before + jnp.sum(_fold_rows(jnp.where(tie, 1.0, 0.0), jnp.sum),
                                    axis=0, keepdims=True)

        lax.fori_loop(0, n_tiles, demote_tile, jnp.zeros((1, Q_TILE), F32))

    m_ref[...] = jnp.full(m_ref.shape, -jnp.inf, F32)
    acc_ref[...] = jnp.zeros(acc_ref.shape, F32)

    def attn_tile(kt, diagonal):
        koff = pl.multiple_of(kt * KEY_TILE, KEY_TILE)
        mask_bias = jnp.where(score_ref[kt] >= thr, 0.0, -MASK_BIG)
        if diagonal:
            ahead = (2.0 * LOG2E) * jnp.maximum(key_iota - q_iota, 0).astype(F32)
        rel_q = (qpos - kt * KEY_TILE).astype(F32)
        pos = pos_ref[...]
        alphas, shifts = [], []

        def logits(h):
            pair = slice((h // 2) * LANES, (h // 2 + 1) * LANES)
            lhs = jnp.concatenate([k_ref[pl.ds(koff, KEY_TILE), pair], pos], axis=1)
            s = jnp.dot(lhs, qm_ref[h], preferred_element_type=F32) + mask_bias
            if diagonal:
                s = s - slopes[h] * ahead
            s_ref[h] = s
            offset = (LOG2E * slopes[h]) * rel_q
            m_old = m_ref[h:h + 1, :]
            m_new = jnp.maximum(m_old, jnp.max(_fold_rows(s, jnp.max), axis=0, keepdims=True) - offset)
            alphas.append(jnp.exp2(m_old - m_new))
            shifts.append(m_new + offset)
            m_ref[h:h + 1, :] = m_new

        def weighted_values(h):
            p = jnp.exp2(s_ref[h] - shifts[h]).astype(BF16)
            pv = jnp.dot(vt_ref[kt, h], p, preferred_element_type=F32)
            acc_ref[h] = alphas[h] * acc_ref[h] + pv

        for h in range(DSA_HEADS):
            logits(h)
        for h in range(DSA_HEADS):
            weighted_values(h)

    def attn_body(kt, carry):
        attn_tile(kt, False)
        return carry

    lax.fori_loop(0, i, attn_body, 0)
    attn_tile(i, True)

    outs = [acc_ref[h, :DSA_HEAD_DIM, :] / acc_ref[h, DSA_HEAD_DIM:DSA_HEAD_DIM + 1, :]
            for h in range(DSA_HEADS)]
    o_ref[...] = jnp.concatenate(outs, axis=0).T.astype(o_ref.dtype)


def _dsa(qt, iqt, iwt, k3, vt, ik3, batch, seq):
    assert Q_TILE == KEY_TILE and Q_TILE % CHUNK == 0
    n = batch * seq
    nqb = seq // Q_TILE
    n_key_tiles = seq // KEY_TILE
    t_spec = lambda rows: pl.BlockSpec((None, None, rows, Q_TILE), lambda b, i: (b, i, 0, 0))
    return pl.pallas_call(
        _dsa_kernel,
        grid=(batch, nqb),
        in_specs=[
            t_spec(512), t_spec(512), t_spec(IW_ROWS),
            pl.BlockSpec((None, seq, 512), lambda b, i: (b, 0, 0)),
            pl.BlockSpec((None, n_key_tiles, DSA_HEADS, V_ROWS, KEY_TILE), lambda b, i: (b, 0, 0, 0, 0)),
            pl.BlockSpec((None, seq, 2 * IDX_DIM), lambda b, i: (b, 0, 0)),
        ],
        out_specs=pl.BlockSpec((Q_TILE, 512), lambda b, i: (b * nqb + i, 0)),
        out_shape=jax.ShapeDtypeStruct((n, 512), BF16),
        scratch_shapes=[
            pltpu.VMEM((n_key_tiles, KEY_TILE, Q_TILE), F32),
            pltpu.VMEM((DSA_HEADS, 2 * LANES, Q_TILE), BF16),
            pltpu.VMEM((IDX_HEADS, LANES, Q_TILE), BF16),
            pltpu.VMEM((KEY_TILE, LANES), BF16),
            pltpu.VMEM((DSA_HEADS, KEY_TILE, Q_TILE), F32),
            pltpu.VMEM((DSA_HEADS, V_ROWS, Q_TILE), F32),
            pltpu.VMEM((DSA_HEADS, Q_TILE), F32),
        ],
        compiler_params=pltpu.CompilerParams(
            dimension_semantics=("arbitrary", "arbitrary"), vmem_limit_bytes=VMEM_LIMIT_BYTES),
        name="dsa",
    )(qt, iqt, iwt, k3, vt, ik3)


def _hgrn_kernel(hq_ref, hf_ref, hi_ref, hg_ref, lb_ref, gain_ref, r_ref, st_ref):
    @pl.when(pl.program_id(1) == 0)
    def _():
        st_ref[...] = jnp.zeros(st_ref.shape, F32)

    r_i = lax.broadcasted_iota(jnp.int32, (CHUNK, CHUNK), 0)
    c_i = lax.broadcasted_iota(jnp.int32, (CHUNK, CHUNK), 1)
    tril = r_i >= c_i
    tril_b = jnp.where(tril, 1.0, 0.0).astype(BF16)
    lb = lb_ref[...]
    gain = gain_ref[...]

    def chunk(c, carry):
        off = pl.multiple_of(c * CHUNK, CHUNK)
        rows = pl.ds(off, CHUNK)
        f = lb + (1.0 - lb) * jax.nn.sigmoid(hf_ref[rows, :])
        logf = jnp.log(f)
        kk = 1.0 - f
        t0 = logf.astype(BF16)
        r1 = logf - t0.astype(F32)
        t1 = r1.astype(BF16)
        t2 = (r1 - t1.astype(F32)).astype(BF16)
        g = (jnp.dot(tril_b, t0, preferred_element_type=F32)
             + jnp.dot(tril_b, t1, preferred_element_type=F32)
             + jnp.dot(tril_b, t2, preferred_element_type=F32))
        g_last = g[CHUNK - 1:CHUNK, :]
        g_mid = g[CHUNK // 2 - 1:CHUNK // 2, :]
        q = hq_ref[rows, :]
        q_state = (q * jnp.exp(g)).astype(BF16)
        q_intra = (q * jnp.exp(g - g_mid)).astype(BF16)
        k_intra = (kk * jnp.exp(g_mid - g)).astype(BF16)
        k_state = (kk * jnp.exp(g_last - g)).astype(BF16)
        decay = jnp.exp(g_last)
        v = hi_ref[rows, :]
        for h in range(HG_HEADS):
            sl = slice(h * HG_DK, (h + 1) * HG_DK)
            a = lax.dot_general(q_intra[:, sl], k_intra[:, sl], NT_DIMS,
                                preferred_element_type=F32)
            a = jnp.where(tril, a, 0.0)
            st = st_ref[h]
            o = (jnp.dot(a.astype(BF16), v[:, sl], preferred_element_type=F32)
                 + lax.dot_general(q_state[:, sl], st.astype(BF16), NT_DIMS,
                                   preferred_element_type=F32))
            st_ref[h] = st * decay[:, sl] + lax.dot_general(
                v[:, sl], k_state[:, sl], TN_DIMS, preferred_element_type=F32)
            ms = jnp.mean(o * o, axis=1, keepdims=True)
            on = o * lax.rsqrt(ms + RMS_EPS)
            gate = hg_ref[rows, sl].astype(F32)
            r = on * gain[:, sl] * (gate * jax.nn.sigmoid(gate))
            r_ref[rows, sl] = r.astype(r_ref.dtype)
        return carry

    lax.fori_loop(0, HG_ROWS // CHUNK, chunk, 0)


def _hgrn(hq, hf, hi, hg, lb, gain, batch, seq):
    n = hq.shape[0]
    nb = seq // HG_ROWS
    row_spec = pl.BlockSpec((HG_ROWS, 512), lambda b, t: (b * nb + t, 0))
    vec_spec = pl.BlockSpec((1, 512), lambda b, t: (0, 0))
    return pl.pallas_call(
        _hgrn_kernel,
        grid=(batch, nb),
        in_specs=[row_spec, row_spec, row_spec, row_spec, vec_spec, vec_spec],
        out_specs=row_spec,
        out_shape=jax.ShapeDtypeStruct((n, 512), BF16),
        scratch_shapes=[pltpu.VMEM((HG_HEADS, HG_DK, HG_DK), F32)],
        compiler_params=pltpu.CompilerParams(
            dimension_semantics=("arbitrary", "arbitrary"), vmem_limit_bytes=VMEM_LIMIT_BYTES),
        name="hgrn",
    )(hq, hf, hi, hg, lb, gain)


def _layer_norm(y, g, b):
    mu = jnp.mean(y, axis=1, keepdims=True)
    d = y - mu
    var = jnp.mean(d * d, axis=1, keepdims=True)
    return d * lax.rsqrt(var + LN_EPS) * g + b


def _mix_kernel(a_ref, r_ref, x_ref, wo_ref, g_ref, b_ref, o_ref):
    mix = (jnp.dot(a_ref[...], wo_ref[:DSA_WIDTH, :], preferred_element_type=F32)
           + jnp.dot(r_ref[...], wo_ref[DSA_WIDTH:, :], preferred_element_type=F32))
    o_ref[...] = _layer_norm(ALPHA * x_ref[...] + mix, g_ref[...], b_ref[...])


def _mix(a, r, x2, wo, g, b):
    n = x2.shape[0]
    half = pl.BlockSpec((MIX_ROWS, 512), lambda i: (i, 0))
    full = pl.BlockSpec((MIX_ROWS, D_MODEL), lambda i: (i, 0))
    vec = pl.BlockSpec((1, D_MODEL), lambda i: (0, 0))
    return pl.pallas_call(
        _mix_kernel,
        grid=(n // MIX_ROWS,),
        in_specs=[half, half, full, pl.BlockSpec(wo.shape, lambda i: (0, 0)), vec, vec],
        out_specs=full,
        out_shape=jax.ShapeDtypeStruct((n, D_MODEL), F32),
        compiler_params=pltpu.CompilerParams(
            dimension_semantics=("arbitrary",), vmem_limit_bytes=VMEM_LIMIT_BYTES),
        name="mix_ln",
    )(a, r, x2, wo, g, b)


def _ffn_kernel(x_ref, wg_ref, wu_ref, wd_ref, g_ref, b_ref, o_ref):
    x = x_ref[...]
    xb = x.astype(BF16)
    gate = jnp.dot(xb, wg_ref[...], preferred_element_type=F32)
    up = jnp.dot(xb, wu_ref[...], preferred_element_type=F32)
    h = (gate * jax.nn.sigmoid(gate) * up).astype(BF16)
    ff = jnp.dot(h, wd_ref[...], preferred_element_type=F32)
    o_ref[...] = _layer_norm(ALPHA * x + ff, g_ref[...], b_ref[...])


def _ffn(x1, wg, wu, wd, g, b):
    n = x1.shape[0]
    full = pl.BlockSpec((FFN_ROWS, D_MODEL), lambda i: (i, 0))
    vec = pl.BlockSpec((1, D_MODEL), lambda i: (0, 0))
    const = lambda w: pl.BlockSpec(w.shape, lambda i: (0, 0), pipeline_mode=pl.Buffered(1))
    return pl.pallas_call(
        _ffn_kernel,
        grid=(n // FFN_ROWS,),
        in_specs=[full, const(wg), const(wu), const(wd), vec, vec],
        out_specs=full,
        out_shape=jax.ShapeDtypeStruct((n, D_MODEL), F32),
        compiler_params=pltpu.CompilerParams(
            dimension_semantics=("arbitrary",), vmem_limit_bytes=VMEM_LIMIT_BYTES),
        name="ffn_ln",
    )(x1, wg, wu, wd, g, b)


def _proj_weights(w):
    pts = [0]
    for s in SPLIT_SIZES:
        pts.append(pts[-1] + s)
    col = lambda j: w[:, pts[j]:pts[j + 1]]
    wn = jnp.concatenate([col(1), col(6), col(7), col(8), col(9), col(4), col(4)], axis=1)
    wt = jnp.concatenate(
        [col(0), col(3), col(2), jnp.pad(col(5), ((0, 0), (0, IW_ROWS - IDX_HEADS)))], axis=1).T
    return wn.astype(BF16), wt.astype(BF16)


def kernel(x, w_in, w_out, hg_lb_logits, hg_norm_g, ln1_g, ln1_b, w_gate, w_up, w_down, ln2_g, ln2_b):
    batch, seq, _ = x.shape
    n = batch * seq
    lb_all = jnp.cumsum(jax.nn.softmax(hg_lb_logits.astype(F32), axis=0), axis=0)

    x2 = x.reshape(n, D_MODEL)
    for l in range(DEPTH):
        wn, wt = _proj_weights(w_in[l])
        k, ik, hq, hf, hi, hg, qt, iqt, vt, iwt = _project(x2, wn, wt, batch, seq)
        a = _dsa(qt, iqt, iwt, k.reshape(batch, seq, 512), vt,
                 ik.reshape(batch, seq, 2 * IDX_DIM), batch, seq)
        r = _hgrn(hq, hf, hi, hg, lb_all[l].reshape(1, 512), hg_norm_g[l].reshape(1, 512).astype(F32),
                  batch, seq)
        x1 = _mix(a, r, x2, w_out[l].astype(BF16), ln1_g[l].reshape(1, D_MODEL),
                  ln1_b[l].reshape(1, D_MODEL))
        x2 = _ffn(x1, w_gate[l].astype(BF16), w_up[l].astype(BF16), w_down[l].astype(BF16),
                  ln2_g[l].reshape(1, D_MODEL), ln2_b[l].reshape(1, D_MODEL))
    return x2.reshape(batch, seq, D_MODEL)
```

```python
import numpy as np
import jax
import jax.numpy as jnp
from jax import lax
from jax.experimental import pallas as pl
from jax.experimental.pallas import tpu as pltpu

D_MODEL = 1024
CHUNK = 64
DSA_WIDTH = 512
DSA_HEAD_DIM = 64
DSA_HEADS = 8
IDX_HEADS = 8
IDX_DIM = 64
TOPK_MAX = 256
HG_WIDTH = 512
HG_DK = 128
HG_HEADS = 4
D_FF = 2816
DEPTH = 1
ALPHA = (2.0 * DEPTH) ** 0.25
LN_EPS = 1e-5
RMS_EPS = 1e-6
SPLIT_SIZES = (512, 512, 512, 512, 64, 8, 512, 512, 512, 512)

F32 = jnp.float32
BF16 = jnp.bfloat16

LANES = 128
SUBLANES = 8
VMEM_LIMIT_BYTES = 56 * 1024 * 1024

Q_TILE = 256
KEY_TILE = 256
IW_ROWS = 16
PROJ_ROWS = 512
HG_ROWS = 512
MIX_ROWS = 512
FFN_ROWS = 256
MASK_BIG = 1e30
LOWEST = -3.0e38
BISECT_VALUE_STEPS = 8
BISECT_STEPS_PER_CHECK = 3
LOG2E = 1.4426950408889634
V_ROWS = DSA_HEAD_DIM + 16

NT_DIMS = (((1,), (1,)), ((), ()))
TN_DIMS = (((0,), (0,)), ((), ()))


def _proj_kernel(x_ref, wn_ref, wt_ref, k_ref, ik_ref, hq_ref, hf_ref, hi_ref, hg_ref,
                 qt_ref, iqt_ref, vt_ref, iwt_ref):
    xb = x_ref[...].astype(BF16)

    def nn(j, width=512):
        return jnp.dot(xb, wn_ref[:, j * 512:j * 512 + width], preferred_element_type=F32)

    k_ref[...] = nn(0).astype(BF16)
    hq_ref[...] = nn(1)
    hf_ref[...] = nn(2)
    hi_ref[...] = nn(3).astype(BF16)
    hg_ref[...] = nn(4).astype(BF16)
    ik_ref[...] = nn(5, 2 * IDX_DIM).astype(BF16)
    t = lax.dot_general(wt_ref[...], xb, NT_DIMS, preferred_element_type=F32)
    for j in range(PROJ_ROWS // Q_TILE):
        cols = slice(j * Q_TILE, (j + 1) * Q_TILE)
        qt_ref[j] = (t[0:512, cols] * (LOG2E * DSA_HEAD_DIM ** -0.5)).astype(BF16)
        iqt_ref[j] = t[512:1024, cols].astype(BF16)
        for h in range(DSA_HEADS):
            lo = 1024 + h * DSA_HEAD_DIM
            vt_ref[j, h, :DSA_HEAD_DIM, :] = t[lo:lo + DSA_HEAD_DIM, cols].astype(BF16)
            vt_ref[j, h, DSA_HEAD_DIM:, :] = jnp.ones((V_ROWS - DSA_HEAD_DIM, Q_TILE), BF16)
        iwt_ref[j] = t[1536:1536 + IW_ROWS, cols]


def _project(x2, wn, wt, batch, seq):
    n = x2.shape[0]
    nb = seq // PROJ_ROWS
    tiles = PROJ_ROWS // Q_TILE
    n_tiles = seq // Q_TILE
    row_spec = lambda w: pl.BlockSpec((PROJ_ROWS, w), lambda i: (i, 0))
    t_shape = lambda rows, dt: jax.ShapeDtypeStruct((batch, n_tiles, rows, Q_TILE), dt)
    t_spec = lambda rows: pl.BlockSpec((None, tiles, rows, Q_TILE), lambda i: (i // nb, i % nb, 0, 0))
    out_shape = (
        jax.ShapeDtypeStruct((n, 512), BF16),
        jax.ShapeDtypeStruct((n, 2 * IDX_DIM), BF16),
        jax.ShapeDtypeStruct((n, 512), F32),
        jax.ShapeDtypeStruct((n, 512), F32),
        jax.ShapeDtypeStruct((n, 512), BF16),
        jax.ShapeDtypeStruct((n, 512), BF16),
        t_shape(512, BF16),
        t_shape(512, BF16),
        jax.ShapeDtypeStruct((batch, n_tiles, DSA_HEADS, V_ROWS, Q_TILE), BF16),
        t_shape(IW_ROWS, F32),
    )
    out_specs = (
        row_spec(512), row_spec(2 * IDX_DIM), row_spec(512), row_spec(512), row_spec(512),
        row_spec(512), t_spec(512), t_spec(512),
        pl.BlockSpec((None, tiles, DSA_HEADS, V_ROWS, Q_TILE), lambda i: (i // nb, i % nb, 0, 0, 0)),
        t_spec(IW_ROWS),
    )
    return pl.pallas_call(
        _proj_kernel,
        grid=(n // PROJ_ROWS,),
        in_specs=[
            pl.BlockSpec((PROJ_ROWS, D_MODEL), lambda i: (i, 0)),
            pl.BlockSpec(wn.shape, lambda i: (0, 0)),
            pl.BlockSpec(wt.shape, lambda i: (0, 0)),
        ],
        out_specs=out_specs,
        out_shape=out_shape,
        compiler_params=pltpu.CompilerParams(
            dimension_semantics=("arbitrary",), vmem_limit_bytes=VMEM_LIMIT_BYTES),
        name="proj",
    )(x2, wn, wt)


def _fold_rows(x, op):
    return op(x.reshape(x.shape[0] // SUBLANES, SUBLANES, x.shape[1]), axis=0)


def _split3(c):
    out = []
    for _ in range(3):
        t = float(np.asarray(c, dtype=BF16))
        out.append(t)
        c = c - t
    return out


def _dsa_kernel(qt_ref, iqt_ref, iwt_ref, k_ref, vt_ref, ik_ref, o_ref,
                score_ref, qm_ref, iqm_ref, pos_ref, s_ref, p_ref, acc_ref, m_ref, alpha_ref, shift_ref):
    i = pl.program_id(1)
    topk = TOPK_MAX
    idx_scale = (IDX_DIM ** -0.5) * (IDX_HEADS ** -0.5)
    slopes = [2.0 ** (-8.0 * (h + 1) / DSA_HEADS) for h in range(DSA_HEADS)]

    row_i = lax.broadcasted_iota(jnp.int32, (LANES, Q_TILE), 0)
    even_rows = row_i < DSA_HEAD_DIM
    for p in range(DSA_HEADS // 2):
        rows = slice(p * LANES, (p + 1) * LANES)
        qp = qt_ref[rows, :].astype(F32)
        iqp = iqt_ref[rows, :].astype(F32)
        qm_ref[2 * p, :LANES, :] = jnp.where(even_rows, qp, 0.0).astype(BF16)
        qm_ref[2 * p + 1, :LANES, :] = jnp.where(even_rows, 0.0, qp).astype(BF16)
        iqm_ref[2 * p] = jnp.where(even_rows, iqp, 0.0).astype(BF16)
        iqm_ref[2 * p + 1] = jnp.where(even_rows, 0.0, iqp).astype(BF16)
    for h in range(DSA_HEADS):
        c0, c1, c2 = _split3(LOG2E * slopes[h])
        coef = jnp.where(row_i == 0, c0, jnp.where(row_i == 1, c1, jnp.where(row_i == 2, c2, 0.0)))
        qm_ref[h, LANES:, :] = coef.astype(BF16)
    pos_ref[...] = jnp.where(
        lax.broadcasted_iota(jnp.int32, (KEY_TILE, LANES), 1) < 3,
        lax.broadcasted_iota(jnp.int32, (KEY_TILE, LANES), 0), 0).astype(F32).astype(BF16)

    w = iwt_ref[...]
    q_iota = lax.broadcasted_iota(jnp.int32, (1, Q_TILE), 1)
    key_iota = lax.broadcasted_iota(jnp.int32, (KEY_TILE, 1), 0)
    qpos = i * Q_TILE + q_iota
    limit = (qpos // CHUNK + 1) * CHUNK

    def score_tile(kt, carry, diagonal):
        rmax, rmin = carry
        koff = pl.multiple_of(kt * KEY_TILE, KEY_TILE)
        ik = ik_ref[pl.ds(koff, KEY_TILE), :]
        acc = jnp.zeros((KEY_TILE, Q_TILE), F32)
        for h in range(IDX_HEADS):
            s = jnp.dot(ik, iqm_ref[h], preferred_element_type=F32)
            acc = acc + w[h:h + 1, :] * jnp.maximum(s, 0.0)
        acc = acc * idx_scale
        if diagonal:
            adm = (kt * KEY_TILE + key_iota) < limit
            lo_fill = jnp.where(adm, acc, -jnp.inf)
            hi_fill = jnp.where(adm, acc, jnp.inf)
        else:
            lo_fill = hi_fill = acc
        score_ref[kt] = lo_fill
        return (jnp.maximum(rmax, _fold_rows(lo_fill, jnp.max)),
                jnp.minimum(rmin, _fold_rows(hi_fill, jnp.min)))

    carry = lax.fori_loop(
        0, i, lambda kt, c: score_tile(kt, c, False),
        (jnp.full((SUBLANES, Q_TILE), -jnp.inf, F32), jnp.full((SUBLANES, Q_TILE), jnp.inf, F32)))
    rmax, rmin = score_tile(i, carry, True)
    n_tiles = i + 1
    col_max = jnp.max(rmax, axis=0, keepdims=True)
    col_min = jnp.min(rmin, axis=0, keepdims=True)

    def to_key(v):
        b = lax.bitcast_convert_type(v, jnp.int32)
        return jnp.where(b < 0, b ^ 0x7FFFFFFF, b)

    def from_key(kv):
        return lax.bitcast_convert_type(jnp.where(kv < 0, kv ^ 0x7FFFFFFF, kv), F32)

    @pl.when(n_tiles < score_ref.shape[0])
    def _():
        score_ref[n_tiles] = jnp.full((KEY_TILE, Q_TILE), -jnp.inf, F32)

    def count(pred):
        def body(j, acc):
            for kt in (2 * j, 2 * j + 1):
                acc = acc + _fold_rows(jnp.where(pred(score_ref[kt]), 1.0, 0.0), jnp.sum)
            return acc
        acc = lax.fori_loop(0, (n_tiles + 1) // 2, body, jnp.zeros((SUBLANES, Q_TILE), F32))
        return jnp.sum(acc, axis=0, keepdims=True)

    need = limit > topk
    active0 = need.astype(jnp.int32)

    def bis_cond(c):
        return c[4] > 0

    def bis_step(lo, hi, c_lo, active, it):
        key_mid = (lo & hi) + ((lo ^ hi) >> 1) + ((lo ^ hi) & 1)
        lo_f = from_key(lo)
        val_mid = to_key(lo_f + 0.5 * (from_key(hi) - lo_f))
        val_mid = jnp.minimum(jnp.maximum(val_mid, lo + 1), hi)
        mid = jnp.where(it < BISECT_VALUE_STEPS, val_mid, key_mid)
        mid_f = from_key(mid)
        cnt = count(lambda sc: sc >= mid_f)
        ge = jnp.logical_and(active > 0, cnt >= topk)
        lt = jnp.logical_and(active > 0, cnt < topk)
        lo = jnp.where(ge, mid, lo)
        c_lo = jnp.where(ge, cnt, c_lo)
        hi = jnp.where(lt, mid, hi)
        finished = jnp.logical_or(cnt == topk, hi == lo + 1)
        active = jnp.where(finished, 0, active)
        return lo, hi, c_lo, active

    def bis_body(c):
        lo, hi, c_lo, active, _, it = c
        for j in range(BISECT_STEPS_PER_CHECK):
            lo, hi, c_lo, active = bis_step(lo, hi, c_lo, active, it + j)
        return lo, hi, c_lo, active, jnp.sum(active), it + BISECT_STEPS_PER_CHECK

    lo, _, c_lo, _, _, _ = lax.while_loop(
        bis_cond, bis_body,
        (to_key(col_min), to_key(col_max) + 1, limit.astype(F32), active0, jnp.sum(active0),
         jnp.int32(0)))
    thr = jnp.where(need, from_key(lo), LOWEST)

    n_tied_rows = jnp.sum(jnp.logical_and(need, c_lo != topk).astype(jnp.int32))

    @pl.when(n_tied_rows > 0)
    def _():
        quota = topk - count(lambda sc: sc > thr)
        r_i = lax.broadcasted_iota(jnp.int32, (KEY_TILE, KEY_TILE), 0)
        c_i = lax.broadcasted_iota(jnp.int32, (KEY_TILE, KEY_TILE), 1)
        strict_lower = jnp.where(r_i > c_i, 1.0, 0.0).astype(BF16)

        def demote_tile(kt, before):
            sc = score_ref[kt]
            tie = sc == thr
            tie_b = jnp.where(tie, 1.0, 0.0).astype(BF16)
            rank = before + jnp.dot(strict_lower, tie_b, preferred_element_type=F32)
            score_ref[kt] = jnp.where(jnp.logical_and(tie, rank >= quota), -jnp.inf, sc)
            return before + jnp.sum(_fold_rows(jnp.where(tie, 1.0, 0.0), jnp.sum),
                                    axis=0, keepdims=True)

        lax.fori_loop(0, n_tiles, demote_tile, jnp.zeros((1, Q_TILE), F32))

    half = DSA_HEADS // 2
    m_ref[...] = jnp.full(m_ref.shape, -jnp.inf, F32)
    acc_ref[...] = jnp.zeros(acc_ref.shape, F32)
    alpha_ref[...] = jnp.ones(alpha_ref.shape, F32)
    shift_ref[...] = jnp.zeros(shift_ref.shape, F32)
    p_ref[:half] = jnp.zeros((half, KEY_TILE, Q_TILE), BF16)
    s_ref[half:] = jnp.full((half, KEY_TILE, Q_TILE), -MASK_BIG, F32)

    def late_probabilities():
        for h in range(half, DSA_HEADS):
            p_ref[h] = jnp.exp2(s_ref[h] - shift_ref[h:h + 1, :]).astype(BF16)

    def weighted_values(kt, heads, alphas):
        for h in heads:
            pv = jnp.dot(vt_ref[kt, h], p_ref[h], preferred_element_type=F32)
            acc_ref[h] = alphas[h] * acc_ref[h] + pv

    def attn_step(kt, diagonal):
        prev = jnp.maximum(kt - 1, 0)
        alphas_prev = [alpha_ref[h:h + 1, :] for h in range(DSA_HEADS)]
        late_probabilities()
        weighted_values(prev, range(DSA_HEADS), alphas_prev)

        koff = pl.multiple_of(kt * KEY_TILE, KEY_TILE)
        mask_bias = jnp.where(score_ref[kt] >= thr, 0.0, -MASK_BIG)
        if diagonal:
            ahead = (2.0 * LOG2E) * jnp.maximum(key_iota - q_iota, 0).astype(F32)
        rel_q = (qpos - kt * KEY_TILE).astype(F32)
        pos = pos_ref[...]
        shifts = []
        for h in range(DSA_HEADS):
            pair = slice((h // 2) * LANES, (h // 2 + 1) * LANES)
            lhs = jnp.concatenate([k_ref[pl.ds(koff, KEY_TILE), pair], pos], axis=1)
            s = jnp.dot(lhs, qm_ref[h], preferred_element_type=F32) + mask_bias
            if diagonal:
                s = s - slopes[h] * ahead
            s_ref[h] = s
            offset = (LOG2E * slopes[h]) * rel_q
            m_old = m_ref[h:h + 1, :]
            m_new = jnp.maximum(m_old, jnp.max(_fold_rows(s, jnp.max), axis=0, keepdims=True) - offset)
            m_ref[h:h + 1, :] = m_new
            alpha_ref[h:h + 1, :] = jnp.exp2(m_old - m_new)
            shift_ref[h:h + 1, :] = m_new + offset
            shifts.append(m_new + offset)

        for h in range(half):
            p_ref[h] = jnp.exp2(s_ref[h] - shifts[h]).astype(BF16)

    def attn_body(kt, carry):
        attn_step(kt, False)
        return carry

    lax.fori_loop(0, i, attn_body, 0)
    attn_step(i, True)
    late_probabilities()
    weighted_values(i, range(DSA_HEADS), [alpha_ref[h:h + 1, :] for h in range(DSA_HEADS)])

    outs = [acc_ref[h, :DSA_HEAD_DIM, :] / acc_ref[h, DSA_HEAD_DIM:DSA_HEAD_DIM + 1, :]
            for h in range(DSA_HEADS)]
    o_ref[...] = jnp.concatenate(outs, axis=0).T.astype(o_ref.dtype)


def _dsa(qt, iqt, iwt, k3, vt, ik3, batch, seq):
    assert Q_TILE == KEY_TILE and Q_TILE % CHUNK == 0
    n = batch * seq
    nqb = seq // Q_TILE
    n_key_tiles = seq // KEY_TILE
    t_spec = lambda rows: pl.BlockSpec((None, None, rows, Q_TILE), lambda b, i: (b, i, 0, 0))
    return pl.pallas_call(
        _dsa_kernel,
        grid=(batch, nqb),
        in_specs=[
            t_spec(512), t_spec(512), t_spec(IW_ROWS),
            pl.BlockSpec((None, seq, 512), lambda b, i: (b, 0, 0)),
            pl.BlockSpec((None, n_key_tiles, DSA_HEADS, V_ROWS, KEY_TILE), lambda b, i: (b, 0, 0, 0, 0)),
            pl.BlockSpec((None, seq, 2 * IDX_DIM), lambda b, i: (b, 0, 0)),
        ],
        out_specs=pl.BlockSpec((Q_TILE, 512), lambda b, i: (b * nqb + i, 0)),
        out_shape=jax.ShapeDtypeStruct((n, 512), BF16),
        scratch_shapes=[
            pltpu.VMEM((n_key_tiles, KEY_TILE, Q_TILE), F32),
            pltpu.VMEM((DSA_HEADS, 2 * LANES, Q_TILE), BF16),
            pltpu.VMEM((IDX_HEADS, LANES, Q_TILE), BF16),
            pltpu.VMEM((KEY_TILE, LANES), BF16),
            pltpu.VMEM((DSA_HEADS, KEY_TILE, Q_TILE), F32),
            pltpu.VMEM((DSA_HEADS, KEY_TILE, Q_TILE), BF16),
            pltpu.VMEM((DSA_HEADS, V_ROWS, Q_TILE), F32),
            pltpu.VMEM((DSA_HEADS, Q_TILE), F32),
            pltpu.VMEM((DSA_HEADS, Q_TILE), F32),
            pltpu.VMEM((DSA_HEADS, Q_TILE), F32),
        ],
        compiler_params=pltpu.CompilerParams(
            dimension_semantics=("arbitrary", "arbitrary"), vmem_limit_bytes=VMEM_LIMIT_BYTES),
        name="dsa",
    )(qt, iqt, iwt, k3, vt, ik3)


def _hgrn_kernel(hq_ref, hf_ref, hi_ref, hg_ref, lb_ref, gain_ref, r_ref,
                 st_ref, qs_ref, qi_ref, ki_ref, ks_ref, dec_ref, a_ref, oi_ref, kv_ref, sp_ref):
    @pl.when(pl.program_id(1) == 0)
    def _():
        st_ref[...] = jnp.zeros(st_ref.shape, F32)

    n_chunks = HG_ROWS // CHUNK
    r_i = lax.broadcasted_iota(jnp.int32, (CHUNK, CHUNK), 0)
    c_i = lax.broadcasted_iota(jnp.int32, (CHUNK, CHUNK), 1)
    tril = r_i >= c_i
    tril_b = jnp.where(tril, 1.0, 0.0).astype(BF16)
    lb = lb_ref[...]
    gain = gain_ref[...]
    chunk_rows = lambda c: slice(c * CHUNK, (c + 1) * CHUNK)
    head_cols = lambda h: slice(h * HG_DK, (h + 1) * HG_DK)

    for c in range(n_chunks):
        rows = chunk_rows(c)
        f = lb + (1.0 - lb) * jax.nn.sigmoid(hf_ref[rows, :])
        logf = jnp.log(f)
        kk = 1.0 - f
        t0 = logf.astype(BF16)
        r1 = logf - t0.astype(F32)
        t1 = r1.astype(BF16)
        t2 = (r1 - t1.astype(F32)).astype(BF16)
        g = (jnp.dot(tril_b, t0, preferred_element_type=F32)
             + jnp.dot(tril_b, t1, preferred_element_type=F32)
             + jnp.dot(tril_b, t2, preferred_element_type=F32))
        g_last = g[CHUNK - 1:CHUNK, :]
        g_mid = g[CHUNK // 2 - 1:CHUNK // 2, :]
        q_intra = hq_ref[rows, :] * jnp.exp(g - g_mid)
        k_intra = kk * jnp.exp(g_mid - g)
        qi_ref[rows, :] = q_intra.astype(BF16)
        ki_ref[rows, :] = k_intra.astype(BF16)
        qs_ref[rows, :] = (q_intra * jnp.exp(g_mid)).astype(BF16)
        ks_ref[rows, :] = (k_intra * jnp.exp(g_last - g_mid)).astype(BF16)
        dec_ref[c:c + 1, :] = jnp.exp(g_last)

    items = [(c, h) for c in range(n_chunks) for h in range(HG_HEADS)]
    for c, h in items:
        a = lax.dot_general(qi_ref[chunk_rows(c), head_cols(h)], ki_ref[chunk_rows(c), head_cols(h)],
                            NT_DIMS, preferred_element_type=F32)
        a_ref[c, h] = jnp.where(tril, a, 0.0).astype(BF16)
    for c, h in items:
        kv_ref[c, h] = lax.dot_general(hi_ref[chunk_rows(c), head_cols(h)],
                                       ks_ref[chunk_rows(c), head_cols(h)], TN_DIMS,
                                       preferred_element_type=F32)
    for c, h in items:
        oi_ref[chunk_rows(c), head_cols(h)] = jnp.dot(
            a_ref[c, h], hi_ref[chunk_rows(c), head_cols(h)], preferred_element_type=F32)

    for h in range(HG_HEADS):
        st = st_ref[h]
        for c in range(n_chunks):
            sp_ref[c, h] = st.astype(BF16)
            st = st * dec_ref[c:c + 1, head_cols(h)] + kv_ref[c, h]
        st_ref[h] = st

    for c in range(n_chunks):
        rows = chunk_rows(c)
        for h in range(HG_HEADS):
            sl = head_cols(h)
            o = oi_ref[rows, sl] + lax.dot_general(qs_ref[rows, sl], sp_ref[c, h], NT_DIMS,
                                                   preferred_element_type=F32)
            ms = jnp.mean(o * o, axis=1, keepdims=True)
            on = o * lax.rsqrt(ms + RMS_EPS)
            gate = hg_ref[rows, sl].astype(F32)
            r = on * gain[:, sl] * (gate * jax.nn.sigmoid(gate))
            r_ref[rows, sl] = r.astype(r_ref.dtype)


def _hgrn(hq, hf, hi, hg, lb, gain, batch, seq):
    n = hq.shape[0]
    nb = seq // HG_ROWS
    row_spec = pl.BlockSpec((HG_ROWS, 512), lambda b, t: (b * nb + t, 0))
    vec_spec = pl.BlockSpec((1, 512), lambda b, t: (0, 0))
    return pl.pallas_call(
        _hgrn_kernel,
        grid=(batch, nb),
        in_specs=[row_spec, row_spec, row_spec, row_spec, vec_spec, vec_spec],
        out_specs=row_spec,
        out_shape=jax.ShapeDtypeStruct((n, 512), BF16),
        scratch_shapes=[
            pltpu.VMEM((HG_HEADS, HG_DK, HG_DK), F32),
            pltpu.VMEM((HG_ROWS, 512), BF16),
            pltpu.VMEM((HG_ROWS, 512), BF16),
            pltpu.VMEM((HG_ROWS, 512), BF16),
            pltpu.VMEM((HG_ROWS, 512), BF16),
            pltpu.VMEM((HG_ROWS // CHUNK, 512), F32),
            pltpu.VMEM((HG_ROWS // CHUNK, HG_HEADS, CHUNK, CHUNK), BF16),
            pltpu.VMEM((HG_ROWS, 512), F32),
            pltpu.VMEM((HG_ROWS // CHUNK, HG_HEADS, HG_DK, HG_DK), F32),
            pltpu.VMEM((HG_ROWS // CHUNK, HG_HEADS, HG_DK, HG_DK), BF16),
        ],
        compiler_params=pltpu.CompilerParams(
            dimension_semantics=("arbitrary", "arbitrary"), vmem_limit_bytes=VMEM_LIMIT_BYTES),
        name="hgrn",
    )(hq, hf, hi, hg, lb, gain)


def _layer_norm(y, g, b):
    mu = jnp.mean(y, axis=1, keepdims=True)
    d = y - mu
    var = jnp.mean(d * d, axis=1, keepdims=True)
    return d * lax.rsqrt(var + LN_EPS) * g + b


def _mix_kernel(a_ref, r_ref, x_ref, wo_ref, g_ref, b_ref, o_ref):
    mix = (jnp.dot(a_ref[...], wo_ref[:DSA_WIDTH, :], preferred_element_type=F32)
           + jnp.dot(r_ref[...], wo_ref[DSA_WIDTH:, :], preferred_element_type=F32))
    o_ref[...] = _layer_norm(ALPHA * x_ref[...] + mix, g_ref[...], b_ref[...])


def _mix(a, r, x2, wo, g, b):
    n = x2.shape[0]
    half = pl.BlockSpec((MIX_ROWS, 512), lambda i: (i, 0))
    full = pl.BlockSpec((MIX_ROWS, D_MODEL), lambda i: (i, 0))
    vec = pl.BlockSpec((1, D_MODEL), lambda i: (0, 0))
    return pl.pallas_call(
        _mix_kernel,
        grid=(n // MIX_ROWS,),
        in_specs=[half, half, full, pl.BlockSpec(wo.shape, lambda i: (0, 0)), vec, vec],
        out_specs=full,
        out_shape=jax.ShapeDtypeStruct((n, D_MODEL), F32),
        compiler_params=pltpu.CompilerParams(
            dimension_semantics=("arbitrary",), vmem_limit_bytes=VMEM_LIMIT_BYTES),
        name="mix_ln",
    )(a, r, x2, wo, g, b)


def _ffn_kernel(x_ref, wg_ref, wu_ref, wd_ref, g_ref, b_ref, o_ref):
    x = x_ref[...]
    xb = x.astype(BF16)
    gate = jnp.dot(xb, wg_ref[...], preferred_element_type=F32)
    up = jnp.dot(xb, wu_ref[...], preferred_element_type=F32)
    h = (gate * jax.nn.sigmoid(gate) * up).astype(BF16)
    ff = jnp.dot(h, wd_ref[...], preferred_element_type=F32)
    o_ref[...] = _layer_norm(ALPHA * x + ff, g_ref[...], b_ref[...])


def _ffn(x1, wg, wu, wd, g, b):
    n = x1.shape[0]
    full = pl.BlockSpec((FFN_ROWS, D_MODEL), lambda i: (i, 0))
    vec = pl.BlockSpec((1, D_MODEL), lambda i: (0, 0))
    const = lambda w: pl.BlockSpec(w.shape, lambda i: (0, 0), pipeline_mode=pl.Buffered(1))
    return pl.pallas_call(
        _ffn_kernel,
        grid=(n // FFN_ROWS,),
        in_specs=[full, const(wg), const(wu), const(wd), vec, vec],
        out_specs=full,
        out_shape=jax.ShapeDtypeStruct((n, D_MODEL), F32),
        compiler_params=pltpu.CompilerParams(
            dimension_semantics=("arbitrary",), vmem_limit_bytes=VMEM_LIMIT_BYTES),
        name="ffn_ln",
    )(x1, wg, wu, wd, g, b)


def _proj_weights(w):
    pts = [0]
    for s in SPLIT_SIZES:
        pts.append(pts[-1] + s)
    col = lambda j: w[:, pts[j]:pts[j + 1]]
    wn = jnp.concatenate([col(1), col(6), col(7), col(8), col(9), col(4), col(4)], axis=1)
    wt = jnp.concatenate(
        [col(0), col(3), col(2), jnp.pad(col(5), ((0, 0), (0, IW_ROWS - IDX_HEADS)))], axis=1).T
    return wn.astype(BF16), wt.astype(BF16)


def kernel(x, w_in, w_out, hg_lb_logits, hg_norm_g, ln1_g, ln1_b, w_gate, w_up, w_down, ln2_g, ln2_b):
    batch, seq, _ = x.shape
    n = batch * seq
    lb_all = jnp.cumsum(jax.nn.softmax(hg_lb_logits.astype(F32), axis=0), axis=0)

    x2 = x.reshape(n, D_MODEL)
    for l in range(DEPTH):
        wn, wt = _proj_weights(w_in[l])
        k, ik, hq, hf, hi, hg, qt, iqt, vt, iwt = _project(x2, wn, wt, batch, seq)
        a = _dsa(qt, iqt, iwt, k.reshape(batch, seq, 512), vt,
                 ik.reshape(batch, seq, 2 * IDX_DIM), batch, seq)
        r = _hgrn(hq, hf, hi, hg, lb_all[l].reshape(1, 512), hg_norm_g[l].reshape(1, 512).astype(F32),
                  batch, seq)
        x1 = _mix(a, r, x2, w_out[l].astype(BF16), ln1_g[l].reshape(1, D_MODEL),
                  ln1_b[l].reshape(1, D_MODEL))
        x2 = _ffn(x1, w_gate[l].astype(BF16), w_up[l].astype(BF16), w_down[l].astype(BF16),
                  ln2_g[l].reshape(1, D_MODEL), ln2_b[l].reshape(1, D_MODEL))
    return x2.reshape(batch, seq, D_MODEL)
```

```python
import numpy as np
import jax
import jax.numpy as jnp
from jax import lax
from jax.experimental import pallas as pl
from jax.experimental.pallas import tpu as pltpu

D_MODEL = 1024
CHUNK = 64
DSA_WIDTH = 512
DSA_HEAD_DIM = 64
DSA_HEADS = 8
IDX_HEADS = 8
IDX_DIM = 64
TOPK_MAX = 256
HG_WIDTH = 512
HG_DK = 128
HG_HEADS = 4
D_FF = 2816
DEPTH = 1
ALPHA = (2.0 * DEPTH) ** 0.25
LN_EPS = 1e-5
RMS_EPS = 1e-6
SPLIT_SIZES = (512, 512, 512, 512, 64, 8, 512, 512, 512, 512)

F32 = jnp.float32
BF16 = jnp.bfloat16

LANES = 128
SUBLANES = 8
VMEM_LIMIT_BYTES = 56 * 1024 * 1024

Q_TILE = 256
KEY_TILE = 256
IW_ROWS = 16
PROJ_ROWS = 512
HG_ROWS = 512
MIX_ROWS = 512
FFN_ROWS = 256
MASK_BIG = 1e30
LOWEST = -3.0e38
BISECT_VALUE_STEPS = 8
BISECT_STEPS_PER_CHECK = 3
LOG2E = 1.4426950408889634
V_ROWS = DSA_HEAD_DIM + 16

NT_DIMS = (((1,), (1,)), ((), ()))
TN_DIMS = (((0,), (0,)), ((), ()))


def _proj_kernel(x_ref, wn_ref, wt_ref, k_ref, ik_ref, hq_ref, hf_ref, hi_ref, hg_ref,
                 qt_ref, iqt_ref, vt_ref, iwt_ref):
    xb = x_ref[...].astype(BF16)

    def nn(j, width=512):
        return jnp.dot(xb, wn_ref[:, j * 512:j * 512 + width], preferred_element_type=F32)

    k_ref[...] = nn(0).astype(BF16)
    hq_ref[...] = nn(1)
    hf_ref[...] = nn(2)
    hi_ref[...] = nn(3).astype(BF16)
    hg_ref[...] = nn(4).astype(BF16)
    ik_ref[...] = nn(5, 2 * IDX_DIM).astype(BF16)
    t = lax.dot_general(wt_ref[...], xb, NT_DIMS, preferred_element_type=F32)
    for j in range(PROJ_ROWS // Q_TILE):
        cols = slice(j * Q_TILE, (j + 1) * Q_TILE)
        qt_ref[j] = (t[0:512, cols] * (LOG2E * DSA_HEAD_DIM ** -0.5)).astype(BF16)
        iqt_ref[j] = t[512:1024, cols].astype(BF16)
        for h in range(DSA_HEADS):
            lo = 1024 + h * DSA_HEAD_DIM
            vt_ref[j, h, :DSA_HEAD_DIM, :] = t[lo:lo + DSA_HEAD_DIM, cols].astype(BF16)
            vt_ref[j, h, DSA_HEAD_DIM:, :] = jnp.ones((V_ROWS - DSA_HEAD_DIM, Q_TILE), BF16)
        iwt_ref[j] = t[1536:1536 + IW_ROWS, cols]


def _project(x2, wn, wt, batch, seq):
    n = x2.shape[0]
    nb = seq // PROJ_ROWS
    tiles = PROJ_ROWS // Q_TILE
    n_tiles = seq // Q_TILE
    row_spec = lambda w: pl.BlockSpec((PROJ_ROWS, w), lambda i: (i, 0))
    t_shape = lambda rows, dt: jax.ShapeDtypeStruct((batch, n_tiles, rows, Q_TILE), dt)
    t_spec = lambda rows: pl.BlockSpec((None, tiles, rows, Q_TILE), lambda i: (i // nb, i % nb, 0, 0))
    out_shape = (
        jax.ShapeDtypeStruct((n, 512), BF16),
        jax.ShapeDtypeStruct((n, 2 * IDX_DIM), BF16),
        jax.ShapeDtypeStruct((n, 512), F32),
        jax.ShapeDtypeStruct((n, 512), F32),
        jax.ShapeDtypeStruct((n, 512), BF16),
        jax.ShapeDtypeStruct((n, 512), BF16),
        t_shape(512, BF16),
        t_shape(512, BF16),
        jax.ShapeDtypeStruct((batch, n_tiles, DSA_HEADS, V_ROWS, Q_TILE), BF16),
        t_shape(IW_ROWS, F32),
    )
    out_specs = (
        row_spec(512), row_spec(2 * IDX_DIM), row_spec(512), row_spec(512), row_spec(512),
        row_spec(512), t_spec(512), t_spec(512),
        pl.BlockSpec((None, tiles, DSA_HEADS, V_ROWS, Q_TILE), lambda i: (i // nb, i % nb, 0, 0, 0)),
        t_spec(IW_ROWS),
    )
    return pl.pallas_call(
        _proj_kernel,
        grid=(n // PROJ_ROWS,),
        in_specs=[
            pl.BlockSpec((PROJ_ROWS, D_MODEL), lambda i: (i, 0)),
            pl.BlockSpec(wn.shape, lambda i: (0, 0)),
            pl.BlockSpec(wt.shape, lambda i: (0, 0)),
        ],
        out_specs=out_specs,
        out_shape=out_shape,
        compiler_params=pltpu.CompilerParams(
            dimension_semantics=("arbitrary",), vmem_limit_bytes=VMEM_LIMIT_BYTES),
        name="proj",
    )(x2, wn, wt)


def _fold_rows(x, op):
    return op(x.reshape(x.shape[0] // SUBLANES, SUBLANES, x.shape[1]), axis=0)


def _split3(c):
    out = []
    for _ in range(3):
        t = float(np.asarray(c, dtype=BF16))
        out.append(t)
        c = c - t
    return out


def _dsa_kernel(qt_ref, iqt_ref, iwt_ref, k_ref, vt_ref, ik_ref, o_ref,
                score_ref, qm_ref, iqm_ref, pos_ref, s_ref, acc_ref, m_ref):
    i = pl.program_id(1)
    topk = TOPK_MAX
    idx_scale = (IDX_DIM ** -0.5) * (IDX_HEADS ** -0.5)
    slopes = [2.0 ** (-8.0 * (h + 1) / DSA_HEADS) for h in range(DSA_HEADS)]

    row_i = lax.broadcasted_iota(jnp.int32, (LANES, Q_TILE), 0)
    even_rows = row_i < DSA_HEAD_DIM
    for p in range(DSA_HEADS // 2):
        rows = slice(p * LANES, (p + 1) * LANES)
        qp = qt_ref[rows, :].astype(F32)
        iqp = iqt_ref[rows, :].astype(F32)
        qm_ref[2 * p, :LANES, :] = jnp.where(even_rows, qp, 0.0).astype(BF16)
        qm_ref[2 * p + 1, :LANES, :] = jnp.where(even_rows, 0.0, qp).astype(BF16)
        iqm_ref[2 * p] = jnp.where(even_rows, iqp, 0.0).astype(BF16)
        iqm_ref[2 * p + 1] = jnp.where(even_rows, 0.0, iqp).astype(BF16)
    for h in range(DSA_HEADS):
        c0, c1, c2 = _split3(LOG2E * slopes[h])
        coef = jnp.where(row_i == 0, c0, jnp.where(row_i == 1, c1, jnp.where(row_i == 2, c2, 0.0)))
        qm_ref[h, LANES:, :] = coef.astype(BF16)
    pos_ref[...] = jnp.where(
        lax.broadcasted_iota(jnp.int32, (KEY_TILE, LANES), 1) < 3,
        lax.broadcasted_iota(jnp.int32, (KEY_TILE, LANES), 0), 0).astype(F32).astype(BF16)

    w = iwt_ref[...]
    q_iota = lax.broadcasted_iota(jnp.int32, (1, Q_TILE), 1)
    key_iota = lax.broadcasted_iota(jnp.int32, (KEY_TILE, 1), 0)
    qpos = i * Q_TILE + q_iota
    limit = (qpos // CHUNK + 1) * CHUNK

    def score_tile(kt, carry, diagonal):
        rmax, rmin = carry
        koff = pl.multiple_of(kt * KEY_TILE, KEY_TILE)
        ik = ik_ref[pl.ds(koff, KEY_TILE), :]
        acc = jnp.zeros((KEY_TILE, Q_TILE), F32)
        for h in range(IDX_HEADS):
            s = jnp.dot(ik, iqm_ref[h], preferred_element_type=F32)
            acc = acc + w[h:h + 1, :] * jnp.maximum(s, 0.0)
        acc = acc * idx_scale
        if diagonal:
            adm = (kt * KEY_TILE + key_iota) < limit
            lo_fill = jnp.where(adm, acc, -jnp.inf)
            hi_fill = jnp.where(adm, acc, jnp.inf)
        else:
            lo_fill = hi_fill = acc
        score_ref[kt] = lo_fill
        return (jnp.maximum(rmax, _fold_rows(lo_fill, jnp.max)),
                jnp.minimum(rmin, _fold_rows(hi_fill, jnp.min)))

    carry = lax.fori_loop(
        0, i, lambda kt, c: score_tile(kt, c, False),
        (jnp.full((SUBLANES, Q_TILE), -jnp.inf, F32), jnp.full((SUBLANES, Q_TILE), jnp.inf, F32)))
    rmax, rmin = score_tile(i, carry, True)
    n_tiles = i + 1
    col_max = jnp.max(rmax, axis=0, keepdims=True)
    col_min = jnp.min(rmin, axis=0, keepdims=True)

    def to_key(v):
        b = lax.bitcast_convert_type(v, jnp.int32)
        return jnp.where(b < 0, b ^ 0x7FFFFFFF, b)

    def from_key(kv):
        return lax.bitcast_convert_type(jnp.where(kv < 0, kv ^ 0x7FFFFFFF, kv), F32)

    @pl.when(n_tiles < score_ref.shape[0])
    def _():
        score_ref[n_tiles] = jnp.full((KEY_TILE, Q_TILE), -jnp.inf, F32)

    def count(pred):
        def body(j, acc):
            for kt in (2 * j, 2 * j + 1):
                acc = acc + _fold_rows(jnp.where(pred(score_ref[kt]), 1.0, 0.0), jnp.sum)
            return acc
        acc = lax.fori_loop(0, (n_tiles + 1) // 2, body, jnp.zeros((SUBLANES, Q_TILE), F32))
        return jnp.sum(acc, axis=0, keepdims=True)

    need = limit > topk
    active0 = need.astype(jnp.int32)

    def bis_cond(c):
        return c[4] > 0

    def bis_step(lo, hi, c_lo, active, it):
        key_mid = (lo & hi) + ((lo ^ hi) >> 1) + ((lo ^ hi) & 1)
        lo_f = from_key(lo)
        val_mid = to_key(lo_f + 0.5 * (from_key(hi) - lo_f))
        val_mid = jnp.minimum(jnp.maximum(val_mid, lo + 1), hi)
        mid = jnp.where(it < BISECT_VALUE_STEPS, val_mid, key_mid)
        mid_f = from_key(mid)
        cnt = count(lambda sc: sc >= mid_f)
        ge = jnp.logical_and(active > 0, cnt >= topk)
        lt = jnp.logical_and(active > 0, cnt < topk)
        lo = jnp.where(ge, mid, lo)
        c_lo = jnp.where(ge, cnt, c_lo)
        hi = jnp.where(lt, mid, hi)
        finished = jnp.logical_or(cnt == topk, hi == lo + 1)
        active = jnp.where(finished, 0, active)
        return lo, hi, c_lo, active

    def bis_body(c):
        lo, hi, c_lo, active, _, it = c
        for j in range(BISECT_STEPS_PER_CHECK):
            lo, hi, c_lo, active = bis_step(lo, hi, c_lo, active, it + j)
        return lo, hi, c_lo, active, jnp.sum(active), it + BISECT_STEPS_PER_CHECK

    lo, _, c_lo, _, _, _ = lax.while_loop(
        bis_cond, bis_body,
        (to_key(col_min), to_key(col_max) + 1, limit.astype(F32), active0, jnp.sum(active0),
         jnp.int32(0)))
    thr = jnp.where(need, from_key(lo), LOWEST)

    n_tied_rows = jnp.sum(jnp.logical_and(need, c_lo != topk).astype(jnp.int32))

    @pl.when(n_tied_rows > 0)
    def _():
        quota = topk - count(lambda sc: sc > thr)
        r_i = lax.broadcasted_iota(jnp.int32, (KEY_TILE, KEY_TILE), 0)
        c_i = lax.broadcasted_iota(jnp.int32, (KEY_TILE, KEY_TILE), 1)
        strict_lower = jnp.where(r_i > c_i, 1.0, 0.0).astype(BF16)

        def demote_tile(kt, before):
            sc = score_ref[kt]
            tie = sc == thr
            tie_b = jnp.where(tie, 1.0, 0.0).astype(BF16)
            rank = before + jnp.dot(strict_lower, tie_b, preferred_element_type=F32)
            score_ref[kt] = jnp.where(jnp.logical_and(tie, rank >= quota), -jnp.inf, sc)
            return before + jnp.sum(_fold_rows(jnp.where(tie, 1.0, 0.0), jnp.sum),
                                    axis=0, keepdims=True)

        lax.fori_loop(0, n_tiles, demote_tile, jnp.zeros((1, Q_TILE), F32))

    m_ref[...] = jnp.full(m_ref.shape, -jnp.inf, F32)
    acc_ref[...] = jnp.zeros(acc_ref.shape, F32)

    def attn_tiles(tiles):
        pos = pos_ref[...]
        alphas, shifts = [], []
        for slot, (kt, diagonal) in enumerate(tiles):
            koff = pl.multiple_of(kt * KEY_TILE, KEY_TILE)
            mask_bias = jnp.where(score_ref[kt] >= thr, 0.0, -MASK_BIG)
            if diagonal:
                ahead = (2.0 * LOG2E) * jnp.maximum(key_iota - q_iota, 0).astype(F32)
            rel_q = (qpos - kt * KEY_TILE).astype(F32)
            for h in range(DSA_HEADS):
                pair = slice((h // 2) * LANES, (h // 2 + 1) * LANES)
                lhs = jnp.concatenate([k_ref[pl.ds(koff, KEY_TILE), pair], pos], axis=1)
                s = jnp.dot(lhs, qm_ref[h], preferred_element_type=F32) + mask_bias
                if diagonal:
                    s = s - slopes[h] * ahead
                s_ref[slot, h] = s
                offset = (LOG2E * slopes[h]) * rel_q
                m_old = m_ref[h:h + 1, :]
                m_new = jnp.maximum(
                    m_old, jnp.max(_fold_rows(s, jnp.max), axis=0, keepdims=True) - offset)
                m_ref[h:h + 1, :] = m_new
                alphas.append(jnp.exp2(m_old - m_new))
                shifts.append(m_new + offset)
        for slot, (kt, _) in enumerate(tiles):
            for h in range(DSA_HEADS):
                j = slot * DSA_HEADS + h
                p = jnp.exp2(s_ref[slot, h] - shifts[j]).astype(BF16)
                pv = jnp.dot(vt_ref[kt, h], p, preferred_element_type=F32)
                acc_ref[h] = alphas[j] * acc_ref[h] + pv

    def attn_body(j, carry):
        attn_tiles([(2 * j, False), (2 * j + 1, False)])
        return carry

    lax.fori_loop(0, i // 2, attn_body, 0)

    @pl.when(i % 2 == 1)
    def _():
        attn_tiles([(i - 1, False), (i, True)])

    @pl.when(i % 2 == 0)
    def _():
        attn_tiles([(i, True)])

    outs = [acc_ref[h, :DSA_HEAD_DIM, :] / acc_ref[h, DSA_HEAD_DIM:DSA_HEAD_DIM + 1, :]
            for h in range(DSA_HEADS)]
    o_ref[...] = jnp.concatenate(outs, axis=0).T.astype(o_ref.dtype)


def _dsa(qt, iqt, iwt, k3, vt, ik3, batch, seq):
    assert Q_TILE == KEY_TILE and Q_TILE % CHUNK == 0
    n = batch * seq
    nqb = seq // Q_TILE
    n_key_tiles = seq // KEY_TILE
    t_spec = lambda rows: pl.BlockSpec((None, None, rows, Q_TILE), lambda b, i: (b, i, 0, 0))
    return pl.pallas_call(
        _dsa_kernel,
        grid=(batch, nqb),
        in_specs=[
            t_spec(512), t_spec(512), t_spec(IW_ROWS),
            pl.BlockSpec((None, seq, 512), lambda b, i: (b, 0, 0)),
            pl.BlockSpec((None, n_key_tiles, DSA_HEADS, V_ROWS, KEY_TILE), lambda b, i: (b, 0, 0, 0, 0)),
            pl.BlockSpec((None, seq, 2 * IDX_DIM), lambda b, i: (b, 0, 0)),
        ],
        out_specs=pl.BlockSpec((Q_TILE, 512), lambda b, i: (b * nqb + i, 0)),
        out_shape=jax.ShapeDtypeStruct((n, 512), BF16),
        scratch_shapes=[
            pltpu.VMEM((n_key_tiles, KEY_TILE, Q_TILE), F32),
            pltpu.VMEM((DSA_HEADS, 2 * LANES, Q_TILE), BF16),
            pltpu.VMEM((IDX_HEADS, LANES, Q_TILE), BF16),
            pltpu.VMEM((KEY_TILE, LANES), BF16),
            pltpu.VMEM((2, DSA_HEADS, KEY_TILE, Q_TILE), F32),
            pltpu.VMEM((DSA_HEADS, V_ROWS, Q_TILE), F32),
            pltpu.VMEM((DSA_HEADS, Q_TILE), F32),
        ],
        compiler_params=pltpu.CompilerParams(
            dimension_semantics=("arbitrary", "arbitrary"), vmem_limit_bytes=VMEM_LIMIT_BYTES),
        name="dsa",
    )(qt, iqt, iwt, k3, vt, ik3)


def _hgrn_kernel(hq_ref, hf_ref, hi_ref, hg_ref, lb_ref, gain_ref, r_ref,
                 st_ref, qs_ref, qi_ref, ki_ref, ks_ref, dec_ref, a_ref, oi_ref, kv_ref, sp_ref):
    @pl.when(pl.program_id(1) == 0)
    def _():
        st_ref[...] = jnp.zeros(st_ref.shape, F32)

    n_chunks = HG_ROWS // CHUNK
    r_i = lax.broadcasted_iota(jnp.int32, (CHUNK, CHUNK), 0)
    c_i = lax.broadcasted_iota(jnp.int32, (CHUNK, CHUNK), 1)
    tril = r_i >= c_i
    tril_b = jnp.where(tril, 1.0, 0.0).astype(BF16)
    lb = lb_ref[...]
    gain = gain_ref[...]
    chunk_rows = lambda c: slice(c * CHUNK, (c + 1) * CHUNK)
    head_cols = lambda h: slice(h * HG_DK, (h + 1) * HG_DK)

    for c in range(n_chunks):
        rows = chunk_rows(c)
        f = lb + (1.0 - lb) * jax.nn.sigmoid(hf_ref[rows, :])
        logf = jnp.log(f)
        kk = 1.0 - f
        t0 = logf.astype(BF16)
        r1 = logf - t0.astype(F32)
        t1 = r1.astype(BF16)
        t2 = (r1 - t1.astype(F32)).astype(BF16)
        g = (jnp.dot(tril_b, t0, preferred_element_type=F32)
             + jnp.dot(tril_b, t1, preferred_element_type=F32)
             + jnp.dot(tril_b, t2, preferred_element_type=F32))
        g_last = g[CHUNK - 1:CHUNK, :]
        g_mid = g[CHUNK // 2 - 1:CHUNK // 2, :]
        q_intra = hq_ref[rows, :] * jnp.exp(g - g_mid)
        k_intra = kk * jnp.exp(g_mid - g)
        qi_ref[rows, :] = q_intra.astype(BF16)
        ki_ref[rows, :] = k_intra.astype(BF16)
        qs_ref[rows, :] = (q_intra * jnp.exp(g_mid)).astype(BF16)
        ks_ref[rows, :] = (k_intra * jnp.exp(g_last - g_mid)).astype(BF16)
        dec_ref[c:c + 1, :] = jnp.exp(g_last)

    items = [(c, h) for c in range(n_chunks) for h in range(HG_HEADS)]
    for c, h in items:
        a = lax.dot_general(qi_ref[chunk_rows(c), head_cols(h)], ki_ref[chunk_rows(c), head_cols(h)],
                            NT_DIMS, preferred_element_type=F32)
        a_ref[c, h] = jnp.where(tril, a, 0.0).astype(BF16)
    for c, h in items:
        kv_ref[c, h] = lax.dot_general(hi_ref[chunk_rows(c), head_cols(h)],
                                       ks_ref[chunk_rows(c), head_cols(h)], TN_DIMS,
                                       preferred_element_type=F32)
    for c, h in items:
        oi_ref[chunk_rows(c), head_cols(h)] = jnp.dot(
            a_ref[c, h], hi_ref[chunk_rows(c), head_cols(h)], preferred_element_type=F32)

    for h in range(HG_HEADS):
        st = st_ref[h]
        for c in range(n_chunks):
            sp_ref[c, h] = st.astype(BF16)
            st = st * dec_ref[c:c + 1, head_cols(h)] + kv_ref[c, h]
        st_ref[h] = st

    for c in range(n_chunks):
        rows = chunk_rows(c)
        for h in range(HG_HEADS):
            sl = head_cols(h)
            o = oi_ref[rows, sl] + lax.dot_general(qs_ref[rows, sl], sp_ref[c, h], NT_DIMS,
                                                   preferred_element_type=F32)
            ms = jnp.mean(o * o, axis=1, keepdims=True)
            on = o * lax.rsqrt(ms + RMS_EPS)
            gate = hg_ref[rows, sl].astype(F32)
            r = on * gain[:, sl] * (gate * jax.nn.sigmoid(gate))
            r_ref[rows, sl] = r.astype(r_ref.dtype)


def _hgrn(hq, hf, hi, hg, lb, gain, batch, seq):
    n = hq.shape[0]
    nb = seq // HG_ROWS
    row_spec = pl.BlockSpec((HG_ROWS, 512), lambda b, t: (b * nb + t, 0))
    vec_spec = pl.BlockSpec((1, 512), lambda b, t: (0, 0))
    return pl.pallas_call(
        _hgrn_kernel,
        grid=(batch, nb),
        in_specs=[row_spec, row_spec, row_spec, row_spec, vec_spec, vec_spec],
        out_specs=row_spec,
        out_shape=jax.ShapeDtypeStruct((n, 512), BF16),
        scratch_shapes=[
            pltpu.VMEM((HG_HEADS, HG_DK, HG_DK), F32),
            pltpu.VMEM((HG_ROWS, 512), BF16),
            pltpu.VMEM((HG_ROWS, 512), BF16),
            pltpu.VMEM((HG_ROWS, 512), BF16),
            pltpu.VMEM((HG_ROWS, 512), BF16),
            pltpu.VMEM((HG_ROWS // CHUNK, 512), F32),
            pltpu.VMEM((HG_ROWS // CHUNK, HG_HEADS, CHUNK, CHUNK), BF16),
            pltpu.VMEM((HG_ROWS, 512), F32),
            pltpu.VMEM((HG_ROWS // CHUNK, HG_HEADS, HG_DK, HG_DK), F32),
            pltpu.VMEM((HG_ROWS // CHUNK, HG_HEADS, HG_DK, HG_DK), BF16),
        ],
        compiler_params=pltpu.CompilerParams(
            dimension_semantics=("arbitrary", "arbitrary"), vmem_limit_bytes=VMEM_LIMIT_BYTES),
        name="hgrn",
    )(hq, hf, hi, hg, lb, gain)


def _layer_norm(y, g, b):
    mu = jnp.mean(y, axis=1, keepdims=True)
    d = y - mu
    var = jnp.mean(d * d, axis=1, keepdims=True)
    return d * lax.rsqrt(var + LN_EPS) * g + b


def _mix_kernel(a_ref, r_ref, x_ref, wo_ref, g_ref, b_ref, o_ref):
    mix = (jnp.dot(a_ref[...], wo_ref[:DSA_WIDTH, :], preferred_element_type=F32)
           + jnp.dot(r_ref[...], wo_ref[DSA_WIDTH:, :], preferred_element_type=F32))
    o_ref[...] = _layer_norm(ALPHA * x_ref[...] + mix, g_ref[...], b_ref[...])


def _mix(a, r, x2, wo, g, b):
    n = x2.shape[0]
    half = pl.BlockSpec((MIX_ROWS, 512), lambda i: (i, 0))
    full = pl.BlockSpec((MIX_ROWS, D_MODEL), lambda i: (i, 0))
    vec = pl.BlockSpec((1, D_MODEL), lambda i: (0, 0))
    return pl.pallas_call(
        _mix_kernel,
        grid=(n // MIX_ROWS,),
        in_specs=[half, half, full, pl.BlockSpec(wo.shape, lambda i: (0, 0)), vec, vec],
        out_specs=full,
        out_shape=jax.ShapeDtypeStruct((n, D_MODEL), F32),
        compiler_params=pltpu.CompilerParams(
            dimension_semantics=("arbitrary",), vmem_limit_bytes=VMEM_LIMIT_BYTES),
        name="mix_ln",
    )(a, r, x2, wo, g, b)


def _ffn_kernel(x_ref, wg_ref, wu_ref, wd_ref, g_ref, b_ref, o_ref):
    x = x_ref[...]
    xb = x.astype(BF16)
    gate = jnp.dot(xb, wg_ref[...], preferred_element_type=F32)
    up = jnp.dot(xb, wu_ref[...], preferred_element_type=F32)
    h = (gate * jax.nn.sigmoid(gate) * up).astype(BF16)
    ff = jnp.dot(h, wd_ref[...], preferred_element_type=F32)
    o_ref[...] = _layer_norm(ALPHA * x + ff, g_ref[...], b_ref[...])


def _ffn(x1, wg, wu, wd, g, b):
    n = x1.shape[0]
    full = pl.BlockSpec((FFN_ROWS, D_MODEL), lambda i: (i, 0))
    vec = pl.BlockSpec((1, D_MODEL), lambda i: (0, 0))
    const = lambda w: pl.BlockSpec(w.shape, lambda i: (0, 0), pipeline_mode=pl.Buffered(1))
    return pl.pallas_call(
        _ffn_kernel,
        grid=(n // FFN_ROWS,),
        in_specs=[full, const(wg), const(wu), const(wd), vec, vec],
        out_specs=full,
        out_shape=jax.ShapeDtypeStruct((n, D_MODEL), F32),
        compiler_params=pltpu.CompilerParams(
            dimension_semantics=("arbitrary",), vmem_limit_bytes=VMEM_LIMIT_BYTES),
        name="ffn_ln",
    )(x1, wg, wu, wd, g, b)


def _proj_weights(w):
    pts = [0]
    for s in SPLIT_SIZES:
        pts.append(pts[-1] + s)
    col = lambda j: w[:, pts[j]:pts[j + 1]]
    wn = jnp.concatenate([col(1), col(6), col(7), col(8), col(9), col(4), col(4)], axis=1)
    wt = jnp.concatenate(
        [col(0), col(3), col(2), jnp.pad(col(5), ((0, 0), (0, IW_ROWS - IDX_HEADS)))], axis=1).T
    return wn.astype(BF16), wt.astype(BF16)


def kernel(x, w_in, w_out, hg_lb_logits, hg_norm_g, ln1_g, ln1_b, w_gate, w_up, w_down, ln2_g, ln2_b):
    batch, seq, _ = x.shape
    n = batch * seq
    lb_all = jnp.cumsum(jax.nn.softmax(hg_lb_logits.astype(F32), axis=0), axis=0)

    x2 = x.reshape(n, D_MODEL)
    for l in range(DEPTH):
        wn, wt = _proj_weights(w_in[l])
        k, ik, hq, hf, hi, hg, qt, iqt, vt, iwt = _project(x2, wn, wt, batch, seq)
        a = _dsa(qt, iqt, iwt, k.reshape(batch, seq, 512), vt,
                 ik.reshape(batch, seq, 2 * IDX_DIM), batch, seq)
        r = _hgrn(hq, hf, hi, hg, lb_all[l].reshape(1, 512), hg_norm_g[l].reshape(1, 512).astype(F32),
                  batch, seq)
        x1 = _mix(a, r, x2, w_out[l].astype(BF16), ln1_g[l].reshape(1, D_MODEL),
                  ln1_b[l].reshape(1, D_MODEL))
        x2 = _ffn(x1, w_gate[l].astype(BF16), w_up[l].astype(BF16), w_down[l].astype(BF16),
                  ln2_g[l].reshape(1, D_MODEL), ln2_b[l].reshape(1, D_MODEL))
    return x2.reshape(batch, seq, D_MODEL)
```

```python
import numpy as np
import jax
import jax.numpy as jnp
from jax import lax
from jax.experimental import pallas as pl
from jax.experimental.pallas import tpu as pltpu

D_MODEL = 1024
CHUNK = 64
DSA_WIDTH = 512
DSA_HEAD_DIM = 64
DSA_HEADS = 8
IDX_HEADS = 8
IDX_DIM = 64
TOPK_MAX = 256
HG_WIDTH = 512
HG_DK = 128
HG_HEADS = 4
D_FF = 2816
DEPTH = 1
ALPHA = (2.0 * DEPTH) ** 0.25
LN_EPS = 1e-5
RMS_EPS = 1e-6
SPLIT_SIZES = (512, 512, 512, 512, 64, 8, 512, 512, 512, 512)

F32 = jnp.float32
BF16 = jnp.bfloat16

LANES = 128
SUBLANES = 8
VMEM_LIMIT_BYTES = 56 * 1024 * 1024

Q_TILE = 256
KEY_TILE = 256
IW_ROWS = 16
PROJ_ROWS = 512
HG_ROWS = 512
TAIL_ROWS = 256
MASK_BIG = 1e30
LOWEST = -3.0e38
BISECT_VALUE_STEPS = 8
BISECT_STEPS_PER_CHECK = 3
LOG2E = 1.4426950408889634
V_ROWS = DSA_HEAD_DIM + 16

NT_DIMS = (((1,), (1,)), ((), ()))
TN_DIMS = (((0,), (0,)), ((), ()))


def _proj_kernel(x_ref, wn_ref, wt_ref, k_ref, ik_ref, hq_ref, hf_ref, hi_ref, hg_ref,
                 qt_ref, iqt_ref, vt_ref, iwt_ref):
    xb = x_ref[...].astype(BF16)

    def nn(j, width=512):
        return jnp.dot(xb, wn_ref[:, j * 512:j * 512 + width], preferred_element_type=F32)

    k_ref[...] = nn(0).astype(BF16)
    hq_ref[...] = nn(1)
    hf_ref[...] = nn(2)
    hi_ref[...] = nn(3).astype(BF16)
    hg_ref[...] = nn(4).astype(BF16)
    ik_ref[...] = nn(5, 2 * IDX_DIM).astype(BF16)
    t = lax.dot_general(wt_ref[...], xb, NT_DIMS, preferred_element_type=F32)
    for j in range(PROJ_ROWS // Q_TILE):
        cols = slice(j * Q_TILE, (j + 1) * Q_TILE)
        qt_ref[j] = (t[0:512, cols] * (LOG2E * DSA_HEAD_DIM ** -0.5)).astype(BF16)
        iqt_ref[j] = t[512:1024, cols].astype(BF16)
        for h in range(DSA_HEADS):
            lo = 1024 + h * DSA_HEAD_DIM
            vt_ref[j, h, :DSA_HEAD_DIM, :] = t[lo:lo + DSA_HEAD_DIM, cols].astype(BF16)
            vt_ref[j, h, DSA_HEAD_DIM:, :] = jnp.ones((V_ROWS - DSA_HEAD_DIM, Q_TILE), BF16)
        iwt_ref[j] = t[1536:1536 + IW_ROWS, cols]


def _project(x2, wn, wt, batch, seq):
    n = x2.shape[0]
    nb = seq // PROJ_ROWS
    tiles = PROJ_ROWS // Q_TILE
    n_tiles = seq // Q_TILE
    row_spec = lambda w: pl.BlockSpec((PROJ_ROWS, w), lambda i: (i, 0))
    t_shape = lambda rows, dt: jax.ShapeDtypeStruct((batch, n_tiles, rows, Q_TILE), dt)
    t_spec = lambda rows: pl.BlockSpec((None, tiles, rows, Q_TILE), lambda i: (i // nb, i % nb, 0, 0))
    out_shape = (
        jax.ShapeDtypeStruct((n, 512), BF16),
        jax.ShapeDtypeStruct((n, 2 * IDX_DIM), BF16),
        jax.ShapeDtypeStruct((n, 512), F32),
        jax.ShapeDtypeStruct((n, 512), F32),
        jax.ShapeDtypeStruct((n, 512), BF16),
        jax.ShapeDtypeStruct((n, 512), BF16),
        t_shape(512, BF16),
        t_shape(512, BF16),
        jax.ShapeDtypeStruct((batch, n_tiles, DSA_HEADS, V_ROWS, Q_TILE), BF16),
        t_shape(IW_ROWS, F32),
    )
    out_specs = (
        row_spec(512), row_spec(2 * IDX_DIM), row_spec(512), row_spec(512), row_spec(512),
        row_spec(512), t_spec(512), t_spec(512),
        pl.BlockSpec((None, tiles, DSA_HEADS, V_ROWS, Q_TILE), lambda i: (i // nb, i % nb, 0, 0, 0)),
        t_spec(IW_ROWS),
    )
    return pl.pallas_call(
        _proj_kernel,
        grid=(n // PROJ_ROWS,),
        in_specs=[
            pl.BlockSpec((PROJ_ROWS, D_MODEL), lambda i: (i, 0)),
            pl.BlockSpec(wn.shape, lambda i: (0, 0)),
            pl.BlockSpec(wt.shape, lambda i: (0, 0)),
        ],
        out_specs=out_specs,
        out_shape=out_shape,
        compiler_params=pltpu.CompilerParams(
            dimension_semantics=("arbitrary",), vmem_limit_bytes=VMEM_LIMIT_BYTES),
        name="proj",
    )(x2, wn, wt)


def _fold_rows(x, op):
    return op(x.reshape(x.shape[0] // SUBLANES, SUBLANES, x.shape[1]), axis=0)


def _split3(c):
    out = []
    for _ in range(3):
        t = float(np.asarray(c, dtype=BF16))
        out.append(t)
        c = c - t
    return out


def _dsa_kernel(qt_ref, iqt_ref, iwt_ref, k_ref, vt_ref, ik_ref, o_ref,
                score_ref, qm_ref, iqm_ref, pos_ref, s_ref, acc_ref, m_ref, ext_ref):
    i = pl.program_id(1)
    topk = TOPK_MAX
    idx_scale = (IDX_DIM ** -0.5) * (IDX_HEADS ** -0.5)
    slopes = [2.0 ** (-8.0 * (h + 1) / DSA_HEADS) for h in range(DSA_HEADS)]

    row_i = lax.broadcasted_iota(jnp.int32, (LANES, Q_TILE), 0)
    even_rows = row_i < DSA_HEAD_DIM
    for p in range(DSA_HEADS // 2):
        rows = slice(p * LANES, (p + 1) * LANES)
        qp = qt_ref[rows, :].astype(F32)
        iqp = iqt_ref[rows, :].astype(F32)
        qm_ref[2 * p, :LANES, :] = jnp.where(even_rows, qp, 0.0).astype(BF16)
        qm_ref[2 * p + 1, :LANES, :] = jnp.where(even_rows, 0.0, qp).astype(BF16)
        iqm_ref[2 * p] = jnp.where(even_rows, iqp, 0.0).astype(BF16)
        iqm_ref[2 * p + 1] = jnp.where(even_rows, 0.0, iqp).astype(BF16)
    for h in range(DSA_HEADS):
        c0, c1, c2 = _split3(LOG2E * slopes[h])
        coef = jnp.where(row_i == 0, c0, jnp.where(row_i == 1, c1, jnp.where(row_i == 2, c2, 0.0)))
        qm_ref[h, LANES:, :] = coef.astype(BF16)
    pos_ref[...] = jnp.where(
        lax.broadcasted_iota(jnp.int32, (KEY_TILE, LANES), 1) < 3,
        lax.broadcasted_iota(jnp.int32, (KEY_TILE, LANES), 0), 0).astype(F32).astype(BF16)

    w = iwt_ref[...]
    q_iota = lax.broadcasted_iota(jnp.int32, (1, Q_TILE), 1)
    key_iota = lax.broadcasted_iota(jnp.int32, (KEY_TILE, 1), 0)
    qpos = i * Q_TILE + q_iota
    limit = (qpos // CHUNK + 1) * CHUNK

    def score_tile(kt, carry, diagonal):
        rmax, rmin = carry
        koff = pl.multiple_of(kt * KEY_TILE, KEY_TILE)
        ik = ik_ref[pl.ds(koff, KEY_TILE), :]
        acc = jnp.zeros((KEY_TILE, Q_TILE), F32)
        for h in range(IDX_HEADS):
            s = jnp.dot(ik, iqm_ref[h], preferred_element_type=F32)
            acc = acc + w[h:h + 1, :] * jnp.maximum(s, 0.0)
        acc = acc * idx_scale
        if diagonal:
            adm = (kt * KEY_TILE + key_iota) < limit
            lo_fill = jnp.where(adm, acc, -jnp.inf)
            hi_fill = jnp.where(adm, acc, jnp.inf)
        else:
            lo_fill = hi_fill = acc
        score_ref[kt] = lo_fill
        return (jnp.maximum(rmax, _fold_rows(lo_fill, jnp.max)),
                jnp.minimum(rmin, _fold_rows(hi_fill, jnp.min)))

    def score_pair(j, carry):
        return score_tile(2 * j + 1, score_tile(2 * j, carry, False), False)

    ext_ref[0], ext_ref[1] = lax.fori_loop(
        0, i // 2, score_pair,
        (jnp.full((SUBLANES, Q_TILE), -jnp.inf, F32), jnp.full((SUBLANES, Q_TILE), jnp.inf, F32)))

    @pl.when(i % 2 == 1)
    def _():
        ext_ref[0], ext_ref[1] = score_tile(
            i, score_tile(i - 1, (ext_ref[0], ext_ref[1]), False), True)

    @pl.when(i % 2 == 0)
    def _():
        ext_ref[0], ext_ref[1] = score_tile(i, (ext_ref[0], ext_ref[1]), True)

    n_tiles = i + 1
    col_max = jnp.max(ext_ref[0], axis=0, keepdims=True)
    col_min = jnp.min(ext_ref[1], axis=0, keepdims=True)

    def to_key(v):
        b = lax.bitcast_convert_type(v, jnp.int32)
        return jnp.where(b < 0, b ^ 0x7FFFFFFF, b)

    def from_key(kv):
        return lax.bitcast_convert_type(jnp.where(kv < 0, kv ^ 0x7FFFFFFF, kv), F32)

    @pl.when(n_tiles < score_ref.shape[0])
    def _():
        score_ref[n_tiles] = jnp.full((KEY_TILE, Q_TILE), -jnp.inf, F32)

    def count(pred):
        def body(j, acc):
            for kt in (2 * j, 2 * j + 1):
                acc = acc + _fold_rows(jnp.where(pred(score_ref[kt]), 1.0, 0.0), jnp.sum)
            return acc
        acc = lax.fori_loop(0, (n_tiles + 1) // 2, body, jnp.zeros((SUBLANES, Q_TILE), F32))
        return jnp.sum(acc, axis=0, keepdims=True)

    need = limit > topk
    active0 = need.astype(jnp.int32)

    def bis_cond(c):
        return c[4] > 0

    def bis_step(lo, hi, c_lo, active, it):
        key_mid = (lo & hi) + ((lo ^ hi) >> 1) + ((lo ^ hi) & 1)
        lo_f = from_key(lo)
        val_mid = to_key(lo_f + 0.5 * (from_key(hi) - lo_f))
        val_mid = jnp.minimum(jnp.maximum(val_mid, lo + 1), hi)
        mid = jnp.where(it < BISECT_VALUE_STEPS, val_mid, key_mid)
        mid_f = from_key(mid)
        cnt = count(lambda sc: sc >= mid_f)
        ge = jnp.logical_and(active > 0, cnt >= topk)
        lt = jnp.logical_and(active > 0, cnt < topk)
        lo = jnp.where(ge, mid, lo)
        c_lo = jnp.where(ge, cnt, c_lo)
        hi = jnp.where(lt, mid, hi)
        finished = jnp.logical_or(cnt == topk, hi == lo + 1)
        active = jnp.where(finished, 0, active)
        return lo, hi, c_lo, active

    def bis_body(c):
        lo, hi, c_lo, active, _, it = c
        for j in range(BISECT_STEPS_PER_CHECK):
            lo, hi, c_lo, active = bis_step(lo, hi, c_lo, active, it + j)
        return lo, hi, c_lo, active, jnp.sum(active), it + BISECT_STEPS_PER_CHECK

    lo, _, c_lo, _, _, _ = lax.while_loop(
        bis_cond, bis_body,
        (to_key(col_min), to_key(col_max) + 1, limit.astype(F32), active0, jnp.sum(active0),
         jnp.int32(0)))
    thr = jnp.where(need, from_key(lo), LOWEST)

    n_tied_rows = jnp.sum(jnp.logical_and(need, c_lo != topk).astype(jnp.int32))

    @pl.when(n_tied_rows > 0)
    def _():
        quota = topk - count(lambda sc: sc > thr)
        r_i = lax.broadcasted_iota(jnp.int32, (KEY_TILE, KEY_TILE), 0)
        c_i = lax.broadcasted_iota(jnp.int32, (KEY_TILE, KEY_TILE), 1)
        strict_lower = jnp.where(r_i > c_i, 1.0, 0.0).astype(BF16)

        def demote_tile(kt, before):
            sc = score_ref[kt]
            tie = sc == thr
            tie_b = jnp.where(tie, 1.0, 0.0).astype(BF16)
            rank = before + jnp.dot(strict_lower, tie_b, preferred_element_type=F32)
            score_ref[kt] = jnp.where(jnp.logical_and(tie, rank >= quota), -jnp.inf, sc)
            return before + jnp.sum(_fold_rows(jnp.where(tie, 1.0, 0.0), jnp.sum),
                                    axis=0, keepdims=True)

        lax.fori_loop(0, n_tiles, demote_tile, jnp.zeros((1, Q_TILE), F32))

    m_ref[...] = jnp.full(m_ref.shape, -jnp.inf, F32)
    acc_ref[...] = jnp.zeros(acc_ref.shape, F32)

    def attn_tiles(tiles):
        pos = pos_ref[...]
        alphas, shifts = [], []
        for slot, (kt, diagonal) in enumerate(tiles):
            koff = pl.multiple_of(kt * KEY_TILE, KEY_TILE)
            mask_bias = jnp.where(score_ref[kt] >= thr, 0.0, -MASK_BIG)
            if diagonal:
                ahead = (2.0 * LOG2E) * jnp.maximum(key_iota - q_iota, 0).astype(F32)
            rel_q = (qpos - kt * KEY_TILE).astype(F32)
            for h in range(DSA_HEADS):
                pair = slice((h // 2) * LANES, (h // 2 + 1) * LANES)
                lhs = jnp.concatenate([k_ref[pl.ds(koff, KEY_TILE), pair], pos], axis=1)
                s = jnp.dot(lhs, qm_ref[h], preferred_element_type=F32) + mask_bias
                if diagonal:
                    s = s - slopes[h] * ahead
                s_ref[slot, h] = s
                offset = (LOG2E * slopes[h]) * rel_q
                m_old = m_ref[h:h + 1, :]
                m_new = jnp.maximum(
                    m_old, jnp.max(_fold_rows(s, jnp.max), axis=0, keepdims=True) - offset)
                m_ref[h:h + 1, :] = m_new
                alphas.append(jnp.exp2(m_old - m_new))
                shifts.append(m_new + offset)
        for slot, (kt, _) in enumerate(tiles):
            for h in range(DSA_HEADS):
                j = slot * DSA_HEADS + h
                p = jnp.exp2(s_ref[slot, h] - shifts[j]).astype(BF16)
                pv = jnp.dot(vt_ref[kt, h], p, preferred_element_type=F32)
                acc_ref[h] = alphas[j] * acc_ref[h] + pv

    def attn_body(j, carry):
        attn_tiles([(2 * j, False), (2 * j + 1, False)])
        return carry

    lax.fori_loop(0, i // 2, attn_body, 0)

    @pl.when(i % 2 == 1)
    def _():
        attn_tiles([(i - 1, False), (i, True)])

    @pl.when(i % 2 == 0)
    def _():
        attn_tiles([(i, True)])

    outs = [acc_ref[h, :DSA_HEAD_DIM, :] / acc_ref[h, DSA_HEAD_DIM:DSA_HEAD_DIM + 1, :]
            for h in range(DSA_HEADS)]
    o_ref[...] = jnp.concatenate(outs, axis=0).T.astype(o_ref.dtype)


def _dsa(qt, iqt, iwt, k3, vt, ik3, batch, seq):
    assert Q_TILE == KEY_TILE and Q_TILE % CHUNK == 0
    n = batch * seq
    nqb = seq // Q_TILE
    n_key_tiles = seq // KEY_TILE
    t_spec = lambda rows: pl.BlockSpec((None, None, rows, Q_TILE), lambda b, i: (b, i, 0, 0))
    return pl.pallas_call(
        _dsa_kernel,
        grid=(batch, nqb),
        in_specs=[
            t_spec(512), t_spec(512), t_spec(IW_ROWS),
            pl.BlockSpec((None, seq, 512), lambda b, i: (b, 0, 0)),
            pl.BlockSpec((None, n_key_tiles, DSA_HEADS, V_ROWS, KEY_TILE), lambda b, i: (b, 0, 0, 0, 0)),
            pl.BlockSpec((None, seq, 2 * IDX_DIM), lambda b, i: (b, 0, 0)),
        ],
        out_specs=pl.BlockSpec((Q_TILE, 512), lambda b, i: (b * nqb + i, 0)),
        out_shape=jax.ShapeDtypeStruct((n, 512), BF16),
        scratch_shapes=[
            pltpu.VMEM((n_key_tiles, KEY_TILE, Q_TILE), F32),
            pltpu.VMEM((DSA_HEADS, 2 * LANES, Q_TILE), BF16),
            pltpu.VMEM((IDX_HEADS, LANES, Q_TILE), BF16),
            pltpu.VMEM((KEY_TILE, LANES), BF16),
            pltpu.VMEM((2, DSA_HEADS, KEY_TILE, Q_TILE), F32),
            pltpu.VMEM((DSA_HEADS, V_ROWS, Q_TILE), F32),
            pltpu.VMEM((DSA_HEADS, Q_TILE), F32),
            pltpu.VMEM((2, SUBLANES, Q_TILE), F32),
        ],
        compiler_params=pltpu.CompilerParams(
            dimension_semantics=("arbitrary", "arbitrary"), vmem_limit_bytes=VMEM_LIMIT_BYTES),
        name="dsa",
    )(qt, iqt, iwt, k3, vt, ik3)


def _hgrn_kernel(hq_ref, hf_ref, hi_ref, hg_ref, lb_ref, gain_ref, r_ref,
                 st_ref, qs_ref, qi_ref, ki_ref, ks_ref, dec_ref, a_ref, oi_ref, kv_ref, sp_ref):
    @pl.when(pl.program_id(1) == 0)
    def _():
        st_ref[...] = jnp.zeros(st_ref.shape, F32)

    n_chunks = HG_ROWS // CHUNK
    r_i = lax.broadcasted_iota(jnp.int32, (CHUNK, CHUNK), 0)
    c_i = lax.broadcasted_iota(jnp.int32, (CHUNK, CHUNK), 1)
    tril = r_i >= c_i
    tril_b = jnp.where(tril, 1.0, 0.0).astype(BF16)
    lb = lb_ref[...]
    gain = gain_ref[...]
    chunk_rows = lambda c: slice(c * CHUNK, (c + 1) * CHUNK)
    head_cols = lambda h: slice(h * HG_DK, (h + 1) * HG_DK)

    for c in range(n_chunks):
        rows = chunk_rows(c)
        f = lb + (1.0 - lb) * jax.nn.sigmoid(hf_ref[rows, :])
        logf = jnp.log(f)
        kk = 1.0 - f
        t0 = logf.astype(BF16)
        r1 = logf - t0.astype(F32)
        t1 = r1.astype(BF16)
        t2 = (r1 - t1.astype(F32)).astype(BF16)
        g = (jnp.dot(tril_b, t0, preferred_element_type=F32)
             + jnp.dot(tril_b, t1, preferred_element_type=F32)
             + jnp.dot(tril_b, t2, preferred_element_type=F32))
        g_last = g[CHUNK - 1:CHUNK, :]
        g_mid = g[CHUNK // 2 - 1:CHUNK // 2, :]
        q_intra = hq_ref[rows, :] * jnp.exp(g - g_mid)
        k_intra = kk * jnp.exp(g_mid - g)
        qi_ref[rows, :] = q_intra.astype(BF16)
        ki_ref[rows, :] = k_intra.astype(BF16)
        qs_ref[rows, :] = (q_intra * jnp.exp(g_mid)).astype(BF16)
        ks_ref[rows, :] = (k_intra * jnp.exp(g_last - g_mid)).astype(BF16)
        dec_ref[c:c + 1, :] = jnp.exp(g_last)

    items = [(c, h) for c in range(n_chunks) for h in range(HG_HEADS)]
    for c, h in items:
        a = lax.dot_general(qi_ref[chunk_rows(c), head_cols(h)], ki_ref[chunk_rows(c), head_cols(h)],
                            NT_DIMS, preferred_element_type=F32)
        a_ref[c, h] = jnp.where(tril, a, 0.0).astype(BF16)
    for c, h in items:
        kv_ref[c, h] = lax.dot_general(hi_ref[chunk_rows(c), head_cols(h)],
                                       ks_ref[chunk_rows(c), head_cols(h)], TN_DIMS,
                                       preferred_element_type=F32)
    for c, h in items:
        oi_ref[chunk_rows(c), head_cols(h)] = jnp.dot(
            a_ref[c, h], hi_ref[chunk_rows(c), head_cols(h)], preferred_element_type=F32)

    for h in range(HG_HEADS):
        st = st_ref[h]
        for c in range(n_chunks):
            sp_ref[c, h] = st.astype(BF16)
            st = st * dec_ref[c:c + 1, head_cols(h)] + kv_ref[c, h]
        st_ref[h] = st

    for c in range(n_chunks):
        rows = chunk_rows(c)
        for h in range(HG_HEADS):
            sl = head_cols(h)
            o = oi_ref[rows, sl] + lax.dot_general(qs_ref[rows, sl], sp_ref[c, h], NT_DIMS,
                                                   preferred_element_type=F32)
            ms = jnp.mean(o * o, axis=1, keepdims=True)
            on = o * lax.rsqrt(ms + RMS_EPS)
            gate = hg_ref[rows, sl].astype(F32)
            r = on * gain[:, sl] * (gate * jax.nn.sigmoid(gate))
            r_ref[rows, sl] = r.astype(r_ref.dtype)


def _hgrn(hq, hf, hi, hg, lb, gain, batch, seq):
    n = hq.shape[0]
    nb = seq // HG_ROWS
    row_spec = pl.BlockSpec((HG_ROWS, 512), lambda b, t: (b * nb + t, 0))
    vec_spec = pl.BlockSpec((1, 512), lambda b, t: (0, 0))
    return pl.pallas_call(
        _hgrn_kernel,
        grid=(batch, nb),
        in_specs=[row_spec, row_spec, row_spec, row_spec, vec_spec, vec_spec],
        out_specs=row_spec,
        out_shape=jax.ShapeDtypeStruct((n, 512), BF16),
        scratch_shapes=[
            pltpu.VMEM((HG_HEADS, HG_DK, HG_DK), F32),
            pltpu.VMEM((HG_ROWS, 512), BF16),
            pltpu.VMEM((HG_ROWS, 512), BF16),
            pltpu.VMEM((HG_ROWS, 512), BF16),
            pltpu.VMEM((HG_ROWS, 512), BF16),
            pltpu.VMEM((HG_ROWS // CHUNK, 512), F32),
            pltpu.VMEM((HG_ROWS // CHUNK, HG_HEADS, CHUNK, CHUNK), BF16),
            pltpu.VMEM((HG_ROWS, 512), F32),
            pltpu.VMEM((HG_ROWS // CHUNK, HG_HEADS, HG_DK, HG_DK), F32),
            pltpu.VMEM((HG_ROWS // CHUNK, HG_HEADS, HG_DK, HG_DK), BF16),
        ],
        compiler_params=pltpu.CompilerParams(
            dimension_semantics=("arbitrary", "arbitrary"), vmem_limit_bytes=VMEM_LIMIT_BYTES),
        name="hgrn",
    )(hq, hf, hi, hg, lb, gain)


def _layer_norm(y, g, b):
    mu = jnp.mean(y, axis=1, keepdims=True)
    d = y - mu
    var = jnp.mean(d * d, axis=1, keepdims=True)
    return d * lax.rsqrt(var + LN_EPS) * g + b


def _tail_kernel(a_ref, r_ref, x_ref, wo_ref, g1_ref, b1_ref, wg_ref, wu_ref, wd_ref,
                 g2_ref, b2_ref, o_ref):
    mix = (jnp.dot(a_ref[...], wo_ref[:DSA_WIDTH, :], preferred_element_type=F32)
           + jnp.dot(r_ref[...], wo_ref[DSA_WIDTH:, :], preferred_element_type=F32))
    x1 = _layer_norm(ALPHA * x_ref[...] + mix, g1_ref[...], b1_ref[...])
    xb = x1.astype(BF16)
    gate = jnp.dot(xb, wg_ref[...], preferred_element_type=F32)
    up = jnp.dot(xb, wu_ref[...], preferred_element_type=F32)
    h = (gate * jax.nn.sigmoid(gate) * up).astype(BF16)
    ff = jnp.dot(h, wd_ref[...], preferred_element_type=F32)
    o_ref[...] = _layer_norm(ALPHA * x1 + ff, g2_ref[...], b2_ref[...])


def _tail(a, r, x2, wo, g1, b1, wg, wu, wd, g2, b2):
    n = x2.shape[0]
    half = pl.BlockSpec((TAIL_ROWS, 512), lambda i: (i, 0))
    full = pl.BlockSpec((TAIL_ROWS, D_MODEL), lambda i: (i, 0))
    vec = pl.BlockSpec((1, D_MODEL), lambda i: (0, 0))
    const = lambda w: pl.BlockSpec(w.shape, lambda i: (0, 0), pipeline_mode=pl.Buffered(1))
    return pl.pallas_call(
        _tail_kernel,
        grid=(n // TAIL_ROWS,),
        in_specs=[half, half, full, const(wo), vec, vec, const(wg), const(wu), const(wd), vec, vec],
        out_specs=full,
        out_shape=jax.ShapeDtypeStruct((n, D_MODEL), F32),
        compiler_params=pltpu.CompilerParams(
            dimension_semantics=("arbitrary",), vmem_limit_bytes=VMEM_LIMIT_BYTES),
        name="mix_ffn",
    )(a, r, x2, wo, g1, b1, wg, wu, wd, g2, b2)


def _proj_weights(w):
    pts = [0]
    for s in SPLIT_SIZES:
        pts.append(pts[-1] + s)
    col = lambda j: w[:, pts[j]:pts[j + 1]]
    wn = jnp.concatenate([col(1), col(6), col(7), col(8), col(9), col(4), col(4)], axis=1)
    wt = jnp.concatenate(
        [col(0), col(3), col(2), jnp.pad(col(5), ((0, 0), (0, IW_ROWS - IDX_HEADS)))], axis=1).T
    return wn.astype(BF16), wt.astype(BF16)


def kernel(x, w_in, w_out, hg_lb_logits, hg_norm_g, ln1_g, ln1_b, w_gate, w_up, w_down, ln2_g, ln2_b):
    batch, seq, _ = x.shape
    n = batch * seq
    lb_all = jnp.cumsum(jax.nn.softmax(hg_lb_logits.astype(F32), axis=0), axis=0)

    x2 = x.reshape(n, D_MODEL)
    for l in range(DEPTH):
        wn, wt = _proj_weights(w_in[l])
        k, ik, hq, hf, hi, hg, qt, iqt, vt, iwt = _project(x2, wn, wt, batch, seq)
        a = _dsa(qt, iqt, iwt, k.reshape(batch, seq, 512), vt,
                 ik.reshape(batch, seq, 2 * IDX_DIM), batch, seq)
        r = _hgrn(hq, hf, hi, hg, lb_all[l].reshape(1, 512), hg_norm_g[l].reshape(1, 512).astype(F32),
                  batch, seq)
        x2 = _tail(a, r, x2, w_out[l].astype(BF16), ln1_g[l].reshape(1, D_MODEL),
                   ln1_b[l].reshape(1, D_MODEL), w_gate[l].astype(BF16), w_up[l].astype(BF16),
                   w_down[l].astype(BF16), ln2_g[l].reshape(1, D_MODEL), ln2_b[l].reshape(1, D_MODEL))
    return x2.reshape(batch, seq, D_MODEL)
```

```python
import numpy as np
import jax
import jax.numpy as jnp
from jax import lax
from jax.experimental import pallas as pl
from jax.experimental.pallas import tpu as pltpu

D_MODEL = 1024
CHUNK = 64
DSA_WIDTH = 512
DSA_HEAD_DIM = 64
DSA_HEADS = 8
IDX_HEADS = 8
IDX_DIM = 64
TOPK_MAX = 256
HG_WIDTH = 512
HG_DK = 128
HG_HEADS = 4
D_FF = 2816
DEPTH = 1
ALPHA = (2.0 * DEPTH) ** 0.25
LN_EPS = 1e-5
RMS_EPS = 1e-6
SPLIT_SIZES = (512, 512, 512, 512, 64, 8, 512, 512, 512, 512)

F32 = jnp.float32
BF16 = jnp.bfloat16

LANES = 128
SUBLANES = 8
VMEM_LIMIT_BYTES = 56 * 1024 * 1024

Q_TILE = 256
KEY_TILE = 256
IW_ROWS = 16
PROJ_ROWS = 512
HG_ROWS = 512
TAIL_ROWS = 512
MASK_BIG = 1e30
LOWEST = -3.0e38
BISECT_VALUE_STEPS = 8
BISECT_STEPS_PER_CHECK = 3
LOG2E = 1.4426950408889634
V_ROWS = DSA_HEAD_DIM + 16

NT_DIMS = (((1,), (1,)), ((), ()))
TN_DIMS = (((0,), (0,)), ((), ()))


def _proj_kernel(x_ref, wn_ref, wt_ref, k_ref, ik_ref, hq_ref, hf_ref, hi_ref, hg_ref,
                 qt_ref, iqt_ref, vt_ref, iwt_ref):
    xb = x_ref[...].astype(BF16)

    def nn(j, width=512):
        return jnp.dot(xb, wn_ref[:, j * 512:j * 512 + width], preferred_element_type=F32)

    k_ref[...] = nn(0).astype(BF16)
    hq_ref[...] = nn(1)
    hf_ref[...] = nn(2)
    hi_ref[...] = nn(3).astype(BF16)
    hg_ref[...] = nn(4).astype(BF16)
    ik_ref[...] = nn(5, 2 * IDX_DIM).astype(BF16)
    t = lax.dot_general(wt_ref[...], xb, NT_DIMS, preferred_element_type=F32)
    for j in range(PROJ_ROWS // Q_TILE):
        cols = slice(j * Q_TILE, (j + 1) * Q_TILE)
        qt_ref[j] = (t[0:512, cols] * (LOG2E * DSA_HEAD_DIM ** -0.5)).astype(BF16)
        iqt_ref[j] = t[512:1024, cols].astype(BF16)
        for h in range(DSA_HEADS):
            lo = 1024 + h * DSA_HEAD_DIM
            vt_ref[j, h, :DSA_HEAD_DIM, :] = t[lo:lo + DSA_HEAD_DIM, cols].astype(BF16)
            vt_ref[j, h, DSA_HEAD_DIM:, :] = jnp.ones((V_ROWS - DSA_HEAD_DIM, Q_TILE), BF16)
        iwt_ref[j] = t[1536:1536 + IW_ROWS, cols]


def _project(x2, wn, wt, batch, seq):
    n = x2.shape[0]
    nb = seq // PROJ_ROWS
    tiles = PROJ_ROWS // Q_TILE
    n_tiles = seq // Q_TILE
    row_spec = lambda w: pl.BlockSpec((PROJ_ROWS, w), lambda i: (i, 0))
    t_shape = lambda rows, dt: jax.ShapeDtypeStruct((batch, n_tiles, rows, Q_TILE), dt)
    t_spec = lambda rows: pl.BlockSpec((None, tiles, rows, Q_TILE), lambda i: (i // nb, i % nb, 0, 0))
    out_shape = (
        jax.ShapeDtypeStruct((n, 512), BF16),
        jax.ShapeDtypeStruct((n, 2 * IDX_DIM), BF16),
        jax.ShapeDtypeStruct((n, 512), F32),
        jax.ShapeDtypeStruct((n, 512), F32),
        jax.ShapeDtypeStruct((n, 512), BF16),
        jax.ShapeDtypeStruct((n, 512), BF16),
        t_shape(512, BF16),
        t_shape(512, BF16),
        jax.ShapeDtypeStruct((batch, n_tiles, DSA_HEADS, V_ROWS, Q_TILE), BF16),
        t_shape(IW_ROWS, F32),
    )
    out_specs = (
        row_spec(512), row_spec(2 * IDX_DIM), row_spec(512), row_spec(512), row_spec(512),
        row_spec(512), t_spec(512), t_spec(512),
        pl.BlockSpec((None, tiles, DSA_HEADS, V_ROWS, Q_TILE), lambda i: (i // nb, i % nb, 0, 0, 0)),
        t_spec(IW_ROWS),
    )
    return pl.pallas_call(
        _proj_kernel,
        grid=(n // PROJ_ROWS,),
        in_specs=[
            pl.BlockSpec((PROJ_ROWS, D_MODEL), lambda i: (i, 0)),
            pl.BlockSpec(wn.shape, lambda i: (0, 0)),
            pl.BlockSpec(wt.shape, lambda i: (0, 0)),
        ],
        out_specs=out_specs,
        out_shape=out_shape,
        compiler_params=pltpu.CompilerParams(
            dimension_semantics=("arbitrary",), vmem_limit_bytes=VMEM_LIMIT_BYTES),
        name="proj",
    )(x2, wn, wt)


def _fold_rows(x, op):
    return op(x.reshape(x.shape[0] // SUBLANES, SUBLANES, x.shape[1]), axis=0)


def _split3(c):
    out = []
    for _ in range(3):
        t = float(np.asarray(c, dtype=BF16))
        out.append(t)
        c = c - t
    return out


def _dsa_kernel(qt_ref, iqt_ref, iwt_ref, k_ref, vt_ref, ik_ref, o_ref,
                score_ref, qm_ref, iqm_ref, pos_ref, s_ref, acc_ref, m_ref, ext_ref):
    i = pl.program_id(1)
    topk = TOPK_MAX
    idx_scale = (IDX_DIM ** -0.5) * (IDX_HEADS ** -0.5)
    slopes = [2.0 ** (-8.0 * (h + 1) / DSA_HEADS) for h in range(DSA_HEADS)]

    row_i = lax.broadcasted_iota(jnp.int32, (LANES, Q_TILE), 0)
    even_rows = row_i < DSA_HEAD_DIM
    for p in range(DSA_HEADS // 2):
        rows = slice(p * LANES, (p + 1) * LANES)
        qp = qt_ref[rows, :].astype(F32)
        iqp = iqt_ref[rows, :].astype(F32)
        qm_ref[2 * p, :LANES, :] = jnp.where(even_rows, qp, 0.0).astype(BF16)
        qm_ref[2 * p + 1, :LANES, :] = jnp.where(even_rows, 0.0, qp).astype(BF16)
        iqm_ref[2 * p] = jnp.where(even_rows, iqp, 0.0).astype(BF16)
        iqm_ref[2 * p + 1] = jnp.where(even_rows, 0.0, iqp).astype(BF16)
    for h in range(DSA_HEADS):
        c0, c1, c2 = _split3(LOG2E * slopes[h])
        coef = jnp.where(row_i == 0, c0, jnp.where(row_i == 1, c1, jnp.where(row_i == 2, c2, 0.0)))
        qm_ref[h, LANES:, :] = coef.astype(BF16)
    pos_ref[...] = jnp.where(
        lax.broadcasted_iota(jnp.int32, (KEY_TILE, LANES), 1) < 3,
        lax.broadcasted_iota(jnp.int32, (KEY_TILE, LANES), 0), 0).astype(F32).astype(BF16)

    w = iwt_ref[...]
    q_iota = lax.broadcasted_iota(jnp.int32, (1, Q_TILE), 1)
    key_iota = lax.broadcasted_iota(jnp.int32, (KEY_TILE, 1), 0)
    qpos = i * Q_TILE + q_iota
    limit = (qpos // CHUNK + 1) * CHUNK

    def score_tile(kt, carry, diagonal):
        rmax, rmin = carry
        koff = pl.multiple_of(kt * KEY_TILE, KEY_TILE)
        ik = ik_ref[pl.ds(koff, KEY_TILE), :]
        acc = jnp.zeros((KEY_TILE, Q_TILE), F32)
        for h in range(IDX_HEADS):
            s = jnp.dot(ik, iqm_ref[h], preferred_element_type=F32)
            acc = acc + w[h:h + 1, :] * jnp.maximum(s, 0.0)
        acc = acc * idx_scale
        if diagonal:
            adm = (kt * KEY_TILE + key_iota) < limit
            lo_fill = jnp.where(adm, acc, -jnp.inf)
            hi_fill = jnp.where(adm, acc, jnp.inf)
        else:
            lo_fill = hi_fill = acc
        score_ref[kt] = lo_fill
        return (jnp.maximum(rmax, _fold_rows(lo_fill, jnp.max)),
                jnp.minimum(rmin, _fold_rows(hi_fill, jnp.min)))

    def score_pair(j, carry):
        return score_tile(2 * j + 1, score_tile(2 * j, carry, False), False)

    ext_ref[0], ext_ref[1] = lax.fori_loop(
        0, i // 2, score_pair,
        (jnp.full((SUBLANES, Q_TILE), -jnp.inf, F32), jnp.full((SUBLANES, Q_TILE), jnp.inf, F32)))

    @pl.when(i % 2 == 1)
    def _():
        ext_ref[0], ext_ref[1] = score_tile(
            i, score_tile(i - 1, (ext_ref[0], ext_ref[1]), False), True)

    @pl.when(i % 2 == 0)
    def _():
        ext_ref[0], ext_ref[1] = score_tile(i, (ext_ref[0], ext_ref[1]), True)

    n_tiles = i + 1
    col_max = jnp.max(ext_ref[0], axis=0, keepdims=True)
    col_min = jnp.min(ext_ref[1], axis=0, keepdims=True)

    def to_key(v):
        b = lax.bitcast_convert_type(v, jnp.int32)
        return jnp.where(b < 0, b ^ 0x7FFFFFFF, b)

    def from_key(kv):
        return lax.bitcast_convert_type(jnp.where(kv < 0, kv ^ 0x7FFFFFFF, kv), F32)

    @pl.when(n_tiles < score_ref.shape[0])
    def _():
        score_ref[n_tiles] = jnp.full((KEY_TILE, Q_TILE), -jnp.inf, F32)

    def count(pred):
        def body(j, acc):
            for kt in (2 * j, 2 * j + 1):
                acc = acc + _fold_rows(jnp.where(pred(score_ref[kt]), 1.0, 0.0), jnp.sum)
            return acc
        acc = lax.fori_loop(0, (n_tiles + 1) // 2, body, jnp.zeros((SUBLANES, Q_TILE), F32))
        return jnp.sum(acc, axis=0, keepdims=True)

    need = limit > topk
    active0 = need.astype(jnp.int32)

    def bis_cond(c):
        return c[4] > 0

    def bis_step(lo, hi, c_lo, active, it):
        key_mid = (lo & hi) + ((lo ^ hi) >> 1) + ((lo ^ hi) & 1)
        lo_f = from_key(lo)
        val_mid = to_key(lo_f + 0.5 * (from_key(hi) - lo_f))
        val_mid = jnp.minimum(jnp.maximum(val_mid, lo + 1), hi)
        mid = jnp.where(it < BISECT_VALUE_STEPS, val_mid, key_mid)
        mid_f = from_key(mid)
        cnt = count(lambda sc: sc >= mid_f)
        ge = jnp.logical_and(active > 0, cnt >= topk)
        lt = jnp.logical_and(active > 0, cnt < topk)
        lo = jnp.where(ge, mid, lo)
        c_lo = jnp.where(ge, cnt, c_lo)
        hi = jnp.where(lt, mid, hi)
        finished = jnp.logical_or(cnt == topk, hi == lo + 1)
        active = jnp.where(finished, 0, active)
        return lo, hi, c_lo, active

    def bis_body(c):
        lo, hi, c_lo, active, _, it = c
        for j in range(BISECT_STEPS_PER_CHECK):
            lo, hi, c_lo, active = bis_step(lo, hi, c_lo, active, it + j)
        return lo, hi, c_lo, active, jnp.sum(active), it + BISECT_STEPS_PER_CHECK

    lo, _, c_lo, _, _, _ = lax.while_loop(
        bis_cond, bis_body,
        (to_key(col_min), to_key(col_max) + 1, limit.astype(F32), active0, jnp.sum(active0),
         jnp.int32(0)))
    thr = jnp.where(need, from_key(lo), LOWEST)

    n_tied_rows = jnp.sum(jnp.logical_and(need, c_lo != topk).astype(jnp.int32))

    @pl.when(n_tied_rows > 0)
    def _():
        quota = topk - count(lambda sc: sc > thr)
        r_i = lax.broadcasted_iota(jnp.int32, (KEY_TILE, KEY_TILE), 0)
        c_i = lax.broadcasted_iota(jnp.int32, (KEY_TILE, KEY_TILE), 1)
        strict_lower = jnp.where(r_i > c_i, 1.0, 0.0).astype(BF16)

        def demote_tile(kt, before):
            sc = score_ref[kt]
            tie = sc == thr
            tie_b = jnp.where(tie, 1.0, 0.0).astype(BF16)
            rank = before + jnp.dot(strict_lower, tie_b, preferred_element_type=F32)
            score_ref[kt] = jnp.where(jnp.logical_and(tie, rank >= quota), -jnp.inf, sc)
            return before + jnp.sum(_fold_rows(jnp.where(tie, 1.0, 0.0), jnp.sum),
                                    axis=0, keepdims=True)

        lax.fori_loop(0, n_tiles, demote_tile, jnp.zeros((1, Q_TILE), F32))

    m_ref[...] = jnp.full(m_ref.shape, -jnp.inf, F32)
    acc_ref[...] = jnp.zeros(acc_ref.shape, F32)

    def attn_tiles(tiles):
        pos = pos_ref[...]
        alphas, shifts = [], []
        for slot, (kt, diagonal) in enumerate(tiles):
            koff = pl.multiple_of(kt * KEY_TILE, KEY_TILE)
            mask_bias = jnp.where(score_ref[kt] >= thr, 0.0, -MASK_BIG)
            if diagonal:
                ahead = (2.0 * LOG2E) * jnp.maximum(key_iota - q_iota, 0).astype(F32)
            rel_q = (qpos - kt * KEY_TILE).astype(F32)
            for h in range(DSA_HEADS):
                pair = slice((h // 2) * LANES, (h // 2 + 1) * LANES)
                lhs = jnp.concatenate([k_ref[pl.ds(koff, KEY_TILE), pair], pos], axis=1)
                s = jnp.dot(lhs, qm_ref[h], preferred_element_type=F32) + mask_bias
                if diagonal:
                    s = s - slopes[h] * ahead
                s_ref[slot, h] = s
                offset = (LOG2E * slopes[h]) * rel_q
                m_old = m_ref[h:h + 1, :]
                m_new = jnp.maximum(
                    m_old, jnp.max(_fold_rows(s, jnp.max), axis=0, keepdims=True) - offset)
                m_ref[h:h + 1, :] = m_new
                alphas.append(jnp.exp2(m_old - m_new))
                shifts.append(m_new + offset)
        for slot, (kt, _) in enumerate(tiles):
            for h in range(DSA_HEADS):
                j = slot * DSA_HEADS + h
                p = jnp.exp2(s_ref[slot, h] - shifts[j]).astype(BF16)
                pv = jnp.dot(vt_ref[kt, h], p, preferred_element_type=F32)
                acc_ref[h] = alphas[j] * acc_ref[h] + pv

    def attn_body(j, carry):
        attn_tiles([(2 * j, False), (2 * j + 1, False)])
        return carry

    lax.fori_loop(0, i // 2, attn_body, 0)

    @pl.when(i % 2 == 1)
    def _():
        attn_tiles([(i - 1, False), (i, True)])

    @pl.when(i % 2 == 0)
    def _():
        attn_tiles([(i, True)])

    outs = [acc_ref[h, :DSA_HEAD_DIM, :] / acc_ref[h, DSA_HEAD_DIM:DSA_HEAD_DIM + 1, :]
            for h in range(DSA_HEADS)]
    o_ref[...] = jnp.concatenate(outs, axis=0).T.astype(o_ref.dtype)


def _dsa(qt, iqt, iwt, k3, vt, ik3, batch, seq):
    assert Q_TILE == KEY_TILE and Q_TILE % CHUNK == 0
    n = batch * seq
    nqb = seq // Q_TILE
    n_key_tiles = seq // KEY_TILE
    t_spec = lambda rows: pl.BlockSpec((None, None, rows, Q_TILE), lambda b, i: (b, i, 0, 0))
    return pl.pallas_call(
        _dsa_kernel,
        grid=(batch, nqb),
        in_specs=[
            t_spec(512), t_spec(512), t_spec(IW_ROWS),
            pl.BlockSpec((None, seq, 512), lambda b, i: (b, 0, 0)),
            pl.BlockSpec((None, n_key_tiles, DSA_HEADS, V_ROWS, KEY_TILE), lambda b, i: (b, 0, 0, 0, 0)),
            pl.BlockSpec((None, seq, 2 * IDX_DIM), lambda b, i: (b, 0, 0)),
        ],
        out_specs=pl.BlockSpec((Q_TILE, 512), lambda b, i: (b * nqb + i, 0)),
        out_shape=jax.ShapeDtypeStruct((n, 512), BF16),
        scratch_shapes=[
            pltpu.VMEM((n_key_tiles, KEY_TILE, Q_TILE), F32),
            pltpu.VMEM((DSA_HEADS, 2 * LANES, Q_TILE), BF16),
            pltpu.VMEM((IDX_HEADS, LANES, Q_TILE), BF16),
            pltpu.VMEM((KEY_TILE, LANES), BF16),
            pltpu.VMEM((2, DSA_HEADS, KEY_TILE, Q_TILE), F32),
            pltpu.VMEM((DSA_HEADS, V_ROWS, Q_TILE), F32),
            pltpu.VMEM((DSA_HEADS, Q_TILE), F32),
            pltpu.VMEM((2, SUBLANES, Q_TILE), F32),
        ],
        compiler_params=pltpu.CompilerParams(
            dimension_semantics=("arbitrary", "arbitrary"), vmem_limit_bytes=VMEM_LIMIT_BYTES),
        name="dsa",
    )(qt, iqt, iwt, k3, vt, ik3)


def _hgrn_kernel(hq_ref, hf_ref, hi_ref, hg_ref, lb_ref, gain_ref, r_ref,
                 st_ref, qs_ref, qi_ref, ki_ref, ks_ref, dec_ref, a_ref, oi_ref, kv_ref, sp_ref):
    @pl.when(pl.program_id(1) == 0)
    def _():
        st_ref[...] = jnp.zeros(st_ref.shape, F32)

    n_chunks = HG_ROWS // CHUNK
    r_i = lax.broadcasted_iota(jnp.int32, (CHUNK, CHUNK), 0)
    c_i = lax.broadcasted_iota(jnp.int32, (CHUNK, CHUNK), 1)
    tril = r_i >= c_i
    tril_b = jnp.where(tril, 1.0, 0.0).astype(BF16)
    lb = lb_ref[...]
    gain = gain_ref[...]
    chunk_rows = lambda c: slice(c * CHUNK, (c + 1) * CHUNK)
    head_cols = lambda h: slice(h * HG_DK, (h + 1) * HG_DK)

    for c in range(n_chunks):
        rows = chunk_rows(c)
        f = lb + (1.0 - lb) * jax.nn.sigmoid(hf_ref[rows, :])
        logf = jnp.log(f)
        kk = 1.0 - f
        t0 = logf.astype(BF16)
        r1 = logf - t0.astype(F32)
        t1 = r1.astype(BF16)
        t2 = (r1 - t1.astype(F32)).astype(BF16)
        g = (jnp.dot(tril_b, t0, preferred_element_type=F32)
             + jnp.dot(tril_b, t1, preferred_element_type=F32)
             + jnp.dot(tril_b, t2, preferred_element_type=F32))
        g_last = g[CHUNK - 1:CHUNK, :]
        g_mid = g[CHUNK // 2 - 1:CHUNK // 2, :]
        q_intra = hq_ref[rows, :] * jnp.exp(g - g_mid)
        k_intra = kk * jnp.exp(g_mid - g)
        qi_ref[rows, :] = q_intra.astype(BF16)
        ki_ref[rows, :] = k_intra.astype(BF16)
        qs_ref[rows, :] = (q_intra * jnp.exp(g_mid)).astype(BF16)
        ks_ref[rows, :] = (k_intra * jnp.exp(g_last - g_mid)).astype(BF16)
        dec_ref[c:c + 1, :] = jnp.exp(g_last)

    items = [(c, h) for c in range(n_chunks) for h in range(HG_HEADS)]
    for c, h in items:
        a = lax.dot_general(qi_ref[chunk_rows(c), head_cols(h)], ki_ref[chunk_rows(c), head_cols(h)],
                            NT_DIMS, preferred_element_type=F32)
        a_ref[c, h] = jnp.where(tril, a, 0.0).astype(BF16)
    for c, h in items:
        kv_ref[c, h] = lax.dot_general(hi_ref[chunk_rows(c), head_cols(h)],
                                       ks_ref[chunk_rows(c), head_cols(h)], TN_DIMS,
                                       preferred_element_type=F32)
    for c, h in items:
        oi_ref[chunk_rows(c), head_cols(h)] = jnp.dot(
            a_ref[c, h], hi_ref[chunk_rows(c), head_cols(h)], preferred_element_type=F32)

    for h in range(HG_HEADS):
        st = st_ref[h]
        for c in range(n_chunks):
            sp_ref[c, h] = st.astype(BF16)
            st = st * dec_ref[c:c + 1, head_cols(h)] + kv_ref[c, h]
        st_ref[h] = st

    for c in range(n_chunks):
        rows = chunk_rows(c)
        for h in range(HG_HEADS):
            sl = head_cols(h)
            o = oi_ref[rows, sl] + lax.dot_general(qs_ref[rows, sl], sp_ref[c, h], NT_DIMS,
                                                   preferred_element_type=F32)
            ms = jnp.mean(o * o, axis=1, keepdims=True)
            on = o * lax.rsqrt(ms + RMS_EPS)
            gate = hg_ref[rows, sl].astype(F32)
            r = on * gain[:, sl] * (gate * jax.nn.sigmoid(gate))
            r_ref[rows, sl] = r.astype(r_ref.dtype)


def _hgrn(hq, hf, hi, hg, lb, gain, batch, seq):
    n = hq.shape[0]
    nb = seq // HG_ROWS
    row_spec = pl.BlockSpec((HG_ROWS, 512), lambda b, t: (b * nb + t, 0))
    vec_spec = pl.BlockSpec((1, 512), lambda b, t: (0, 0))
    return pl.pallas_call(
        _hgrn_kernel,
        grid=(batch, nb),
        in_specs=[row_spec, row_spec, row_spec, row_spec, vec_spec, vec_spec],
        out_specs=row_spec,
        out_shape=jax.ShapeDtypeStruct((n, 512), BF16),
        scratch_shapes=[
            pltpu.VMEM((HG_HEADS, HG_DK, HG_DK), F32),
            pltpu.VMEM((HG_ROWS, 512), BF16),
            pltpu.VMEM((HG_ROWS, 512), BF16),
            pltpu.VMEM((HG_ROWS, 512), BF16),
            pltpu.VMEM((HG_ROWS, 512), BF16),
            pltpu.VMEM((HG_ROWS // CHUNK, 512), F32),
            pltpu.VMEM((HG_ROWS // CHUNK, HG_HEADS, CHUNK, CHUNK), BF16),
            pltpu.VMEM((HG_ROWS, 512), F32),
            pltpu.VMEM((HG_ROWS // CHUNK, HG_HEADS, HG_DK, HG_DK), F32),
            pltpu.VMEM((HG_ROWS // CHUNK, HG_HEADS, HG_DK, HG_DK), BF16),
        ],
        compiler_params=pltpu.CompilerParams(
            dimension_semantics=("arbitrary", "arbitrary"), vmem_limit_bytes=VMEM_LIMIT_BYTES),
        name="hgrn",
    )(hq, hf, hi, hg, lb, gain)


def _layer_norm(y, g, b):
    mu = jnp.mean(y, axis=1, keepdims=True)
    d = y - mu
    var = jnp.mean(d * d, axis=1, keepdims=True)
    return d * lax.rsqrt(var + LN_EPS) * g + b


def _tail_kernel(a_ref, r_ref, x_ref, wo_ref, g1_ref, b1_ref, wg_ref, wu_ref, wd_ref,
                 g2_ref, b2_ref, o_ref):
    halves = [slice(j * (TAIL_ROWS // 2), (j + 1) * (TAIL_ROWS // 2)) for j in range(2)]
    mix = [jnp.dot(a_ref[rows, :], wo_ref[:DSA_WIDTH, :], preferred_element_type=F32)
           + jnp.dot(r_ref[rows, :], wo_ref[DSA_WIDTH:, :], preferred_element_type=F32)
           for rows in halves]
    x1, h = [], []
    for j, rows in enumerate(halves):
        x1.append(_layer_norm(ALPHA * x_ref[rows, :] + mix[j], g1_ref[...], b1_ref[...]))
        xb = x1[j].astype(BF16)
        gate = jnp.dot(xb, wg_ref[...], preferred_element_type=F32)
        up = jnp.dot(xb, wu_ref[...], preferred_element_type=F32)
        h.append((gate * jax.nn.sigmoid(gate) * up).astype(BF16))
    for j, rows in enumerate(halves):
        ff = jnp.dot(h[j], wd_ref[...], preferred_element_type=F32)
        o_ref[rows, :] = _layer_norm(ALPHA * x1[j] + ff, g2_ref[...], b2_ref[...])


def _tail(a, r, x2, wo, g1, b1, wg, wu, wd, g2, b2):
    n = x2.shape[0]
    half = pl.BlockSpec((TAIL_ROWS, 512), lambda i: (i, 0))
    full = pl.BlockSpec((TAIL_ROWS, D_MODEL), lambda i: (i, 0))
    vec = pl.BlockSpec((1, D_MODEL), lambda i: (0, 0))
    const = lambda w: pl.BlockSpec(w.shape, lambda i: (0, 0), pipeline_mode=pl.Buffered(1))
    return pl.pallas_call(
        _tail_kernel,
        grid=(n // TAIL_ROWS,),
        in_specs=[half, half, full, const(wo), vec, vec, const(wg), const(wu), const(wd), vec, vec],
        out_specs=full,
        out_shape=jax.ShapeDtypeStruct((n, D_MODEL), F32),
        compiler_params=pltpu.CompilerParams(
            dimension_semantics=("arbitrary",), vmem_limit_bytes=VMEM_LIMIT_BYTES),
        name="mix_ffn",
    )(a, r, x2, wo, g1, b1, wg, wu, wd, g2, b2)


def _proj_weights(w):
    pts = [0]
    for s in SPLIT_SIZES:
        pts.append(pts[-1] + s)
    col = lambda j: w[:, pts[j]:pts[j + 1]]
    wn = jnp.concatenate([col(1), col(6), col(7), col(8), col(9), col(4), col(4)], axis=1)
    wt = jnp.concatenate(
        [col(0), col(3), col(2), jnp.pad(col(5), ((0, 0), (0, IW_ROWS - IDX_HEADS)))], axis=1).T
    return wn.astype(BF16), wt.astype(BF16)


def kernel(x, w_in, w_out, hg_lb_logits, hg_norm_g, ln1_g, ln1_b, w_gate, w_up, w_down, ln2_g, ln2_b):
    batch, seq, _ = x.shape
    n = batch * seq
    lb_all = jnp.cumsum(jax.nn.softmax(hg_lb_logits.astype(F32), axis=0), axis=0)

    x2 = x.reshape(n, D_MODEL)
    for l in range(DEPTH):
        wn, wt = _proj_weights(w_in[l])
        k, ik, hq, hf, hi, hg, qt, iqt, vt, iwt = _project(x2, wn, wt, batch, seq)
        a = _dsa(qt, iqt, iwt, k.reshape(batch, seq, 512), vt,
                 ik.reshape(batch, seq, 2 * IDX_DIM), batch, seq)
        r = _hgrn(hq, hf, hi, hg, lb_all[l].reshape(1, 512), hg_norm_g[l].reshape(1, 512).astype(F32),
                  batch, seq)
        x2 = _tail(a, r, x2, w_out[l].astype(BF16), ln1_g[l].reshape(1, D_MODEL),
                   ln1_b[l].reshape(1, D_MODEL), w_gate[l].astype(BF16), w_up[l].astype(BF16),
                   w_down[l].astype(BF16), ln2_g[l].reshape(1, D_MODEL), ln2_b[l].reshape(1, D_MODEL))
    return x2.reshape(batch, seq, D_MODEL)
```

```python
import numpy as np
import jax
import jax.numpy as jnp
from jax import lax
from jax.experimental import pallas as pl
from jax.experimental.pallas import tpu as pltpu

D_MODEL = 1024
CHUNK = 64
DSA_WIDTH = 512
DSA_HEAD_DIM = 64
DSA_HEADS = 8
IDX_HEADS = 8
IDX_DIM = 64
TOPK_MAX = 256
HG_WIDTH = 512
HG_DK = 128
HG_HEADS = 4
D_FF = 2816
DEPTH = 1
ALPHA = (2.0 * DEPTH) ** 0.25
LN_EPS = 1e-5
RMS_EPS = 1e-6
SPLIT_SIZES = (512, 512, 512, 512, 64, 8, 512, 512, 512, 512)

F32 = jnp.float32
BF16 = jnp.bfloat16

LANES = 128
SUBLANES = 8
VMEM_LIMIT_BYTES = 56 * 1024 * 1024

Q_TILE = 256
KEY_TILE = 256
IW_ROWS = 16
PROJ_ROWS = 512
HG_ROWS = 512
TAIL_ROWS = 512
MASK_BIG = 1e30
LOWEST = -3.0e38
BISECT_VALUE_STEPS = 8
BISECT_STEPS_PER_CHECK = 3
ATTN_GROUP = 4
LOG2E = 1.4426950408889634
V_ROWS = DSA_HEAD_DIM + 16

NT_DIMS = (((1,), (1,)), ((), ()))
TN_DIMS = (((0,), (0,)), ((), ()))


def _proj_kernel(x_ref, wn_ref, wt_ref, k_ref, ik_ref, hq_ref, hf_ref, hi_ref, hg_ref,
                 qt_ref, iqt_ref, vt_ref, iwt_ref):
    xb = x_ref[...].astype(BF16)

    def nn(j, width=512):
        return jnp.dot(xb, wn_ref[:, j * 512:j * 512 + width], preferred_element_type=F32)

    k_ref[...] = nn(0).astype(BF16)
    hq_ref[...] = nn(1)
    hf_ref[...] = nn(2)
    hi_ref[...] = nn(3).astype(BF16)
    hg_ref[...] = nn(4).astype(BF16)
    ik_ref[...] = nn(5, 2 * IDX_DIM).astype(BF16)
    t = lax.dot_general(wt_ref[...], xb, NT_DIMS, preferred_element_type=F32)
    for j in range(PROJ_ROWS // Q_TILE):
        cols = slice(j * Q_TILE, (j + 1) * Q_TILE)
        qt_ref[j] = (t[0:512, cols] * (LOG2E * DSA_HEAD_DIM ** -0.5)).astype(BF16)
        iqt_ref[j] = t[512:1024, cols].astype(BF16)
        for h in range(DSA_HEADS):
            lo = 1024 + h * DSA_HEAD_DIM
            vt_ref[j, h, :DSA_HEAD_DIM, :] = t[lo:lo + DSA_HEAD_DIM, cols].astype(BF16)
            vt_ref[j, h, DSA_HEAD_DIM:, :] = jnp.ones((V_ROWS - DSA_HEAD_DIM, Q_TILE), BF16)
        iwt_ref[j] = t[1536:1536 + IW_ROWS, cols]


def _project(x2, wn, wt, batch, seq):
    n = x2.shape[0]
    nb = seq // PROJ_ROWS
    tiles = PROJ_ROWS // Q_TILE
    n_tiles = seq // Q_TILE
    row_spec = lambda w: pl.BlockSpec((PROJ_ROWS, w), lambda i: (i, 0))
    t_shape = lambda rows, dt: jax.ShapeDtypeStruct((batch, n_tiles, rows, Q_TILE), dt)
    t_spec = lambda rows: pl.BlockSpec((None, tiles, rows, Q_TILE), lambda i: (i // nb, i % nb, 0, 0))
    out_shape = (
        jax.ShapeDtypeStruct((n, 512), BF16),
        jax.ShapeDtypeStruct((n, 2 * IDX_DIM), BF16),
        jax.ShapeDtypeStruct((n, 512), F32),
        jax.ShapeDtypeStruct((n, 512), F32),
        jax.ShapeDtypeStruct((n, 512), BF16),
        jax.ShapeDtypeStruct((n, 512), BF16),
        t_shape(512, BF16),
        t_shape(512, BF16),
        jax.ShapeDtypeStruct((batch, n_tiles, DSA_HEADS, V_ROWS, Q_TILE), BF16),
        t_shape(IW_ROWS, F32),
    )
    out_specs = (
        row_spec(512), row_spec(2 * IDX_DIM), row_spec(512), row_spec(512), row_spec(512),
        row_spec(512), t_spec(512), t_spec(512),
        pl.BlockSpec((None, tiles, DSA_HEADS, V_ROWS, Q_TILE), lambda i: (i // nb, i % nb, 0, 0, 0)),
        t_spec(IW_ROWS),
    )
    return pl.pallas_call(
        _proj_kernel,
        grid=(n // PROJ_ROWS,),
        in_specs=[
            pl.BlockSpec((PROJ_ROWS, D_MODEL), lambda i: (i, 0)),
            pl.BlockSpec(wn.shape, lambda i: (0, 0)),
            pl.BlockSpec(wt.shape, lambda i: (0, 0)),
        ],
        out_specs=out_specs,
        out_shape=out_shape,
        compiler_params=pltpu.CompilerParams(
            dimension_semantics=("arbitrary",), vmem_limit_bytes=VMEM_LIMIT_BYTES),
        name="proj",
    )(x2, wn, wt)


def _fold_rows(x, op):
    return op(x.reshape(x.shape[0] // SUBLANES, SUBLANES, x.shape[1]), axis=0)


def _split3(c):
    out = []
    for _ in range(3):
        t = float(np.asarray(c, dtype=BF16))
        out.append(t)
        c = c - t
    return out


def _dsa_kernel(qt_ref, iqt_ref, iwt_ref, k_ref, vt_ref, ik_ref, o_ref,
                score_ref, qm_ref, iqm_ref, pos_ref, s_ref, acc_ref, m_ref, ext_ref):
    i = pl.program_id(1)
    topk = TOPK_MAX
    idx_scale = (IDX_DIM ** -0.5) * (IDX_HEADS ** -0.5)
    slopes = [2.0 ** (-8.0 * (h + 1) / DSA_HEADS) for h in range(DSA_HEADS)]

    row_i = lax.broadcasted_iota(jnp.int32, (LANES, Q_TILE), 0)
    even_rows = row_i < DSA_HEAD_DIM
    for p in range(DSA_HEADS // 2):
        rows = slice(p * LANES, (p + 1) * LANES)
        qp = qt_ref[rows, :].astype(F32)
        iqp = iqt_ref[rows, :].astype(F32)
        qm_ref[2 * p, :LANES, :] = jnp.where(even_rows, qp, 0.0).astype(BF16)
        qm_ref[2 * p + 1, :LANES, :] = jnp.where(even_rows, 0.0, qp).astype(BF16)
        iqm_ref[2 * p] = jnp.where(even_rows, iqp, 0.0).astype(BF16)
        iqm_ref[2 * p + 1] = jnp.where(even_rows, 0.0, iqp).astype(BF16)
    for h in range(DSA_HEADS):
        c0, c1, c2 = _split3(LOG2E * slopes[h])
        coef = jnp.where(row_i == 0, c0, jnp.where(row_i == 1, c1, jnp.where(row_i == 2, c2, 0.0)))
        qm_ref[h, LANES:, :] = coef.astype(BF16)
    pos_ref[...] = jnp.where(
        lax.broadcasted_iota(jnp.int32, (KEY_TILE, LANES), 1) < 3,
        lax.broadcasted_iota(jnp.int32, (KEY_TILE, LANES), 0), 0).astype(F32).astype(BF16)

    w = iwt_ref[...]
    q_iota = lax.broadcasted_iota(jnp.int32, (1, Q_TILE), 1)
    key_iota = lax.broadcasted_iota(jnp.int32, (KEY_TILE, 1), 0)
    qpos = i * Q_TILE + q_iota
    limit = (qpos // CHUNK + 1) * CHUNK

    def score_tile(kt, carry, diagonal):
        rmax, rmin = carry
        koff = pl.multiple_of(kt * KEY_TILE, KEY_TILE)
        ik = ik_ref[pl.ds(koff, KEY_TILE), :]
        acc = jnp.zeros((KEY_TILE, Q_TILE), F32)
        for h in range(IDX_HEADS):
            s = jnp.dot(ik, iqm_ref[h], preferred_element_type=F32)
            acc = acc + w[h:h + 1, :] * jnp.maximum(s, 0.0)
        acc = acc * idx_scale
        if diagonal:
            adm = (kt * KEY_TILE + key_iota) < limit
            lo_fill = jnp.where(adm, acc, -jnp.inf)
            hi_fill = jnp.where(adm, acc, jnp.inf)
        else:
            lo_fill = hi_fill = acc
        score_ref[kt] = lo_fill
        return (jnp.maximum(rmax, _fold_rows(lo_fill, jnp.max)),
                jnp.minimum(rmin, _fold_rows(hi_fill, jnp.min)))

    def score_pair(j, carry):
        return score_tile(2 * j + 1, score_tile(2 * j, carry, False), False)

    ext_ref[0], ext_ref[1] = lax.fori_loop(
        0, i // 2, score_pair,
        (jnp.full((SUBLANES, Q_TILE), -jnp.inf, F32), jnp.full((SUBLANES, Q_TILE), jnp.inf, F32)))

    @pl.when(i % 2 == 1)
    def _():
        ext_ref[0], ext_ref[1] = score_tile(
            i, score_tile(i - 1, (ext_ref[0], ext_ref[1]), False), True)

    @pl.when(i % 2 == 0)
    def _():
        ext_ref[0], ext_ref[1] = score_tile(i, (ext_ref[0], ext_ref[1]), True)

    n_tiles = i + 1
    col_max = jnp.max(ext_ref[0], axis=0, keepdims=True)
    col_min = jnp.min(ext_ref[1], axis=0, keepdims=True)

    def to_key(v):
        b = lax.bitcast_convert_type(v, jnp.int32)
        return jnp.where(b < 0, b ^ 0x7FFFFFFF, b)

    def from_key(kv):
        return lax.bitcast_convert_type(jnp.where(kv < 0, kv ^ 0x7FFFFFFF, kv), F32)

    @pl.when(n_tiles < score_ref.shape[0])
    def _():
        score_ref[n_tiles] = jnp.full((KEY_TILE, Q_TILE), -jnp.inf, F32)

    def count(pred):
        def body(j, acc):
            for kt in (2 * j, 2 * j + 1):
                acc = acc + _fold_rows(jnp.where(pred(score_ref[kt]), 1.0, 0.0), jnp.sum)
            return acc
        acc = lax.fori_loop(0, (n_tiles + 1) // 2, body, jnp.zeros((SUBLANES, Q_TILE), F32))
        return jnp.sum(acc, axis=0, keepdims=True)

    need = limit > topk
    active0 = need.astype(jnp.int32)

    def bis_cond(c):
        return c[4] > 0

    def bis_step(lo, hi, c_lo, active, it):
        key_mid = (lo & hi) + ((lo ^ hi) >> 1) + ((lo ^ hi) & 1)
        lo_f = from_key(lo)
        val_mid = to_key(lo_f + 0.5 * (from_key(hi) - lo_f))
        val_mid = jnp.minimum(jnp.maximum(val_mid, lo + 1), hi)
        mid = jnp.where(it < BISECT_VALUE_STEPS, val_mid, key_mid)
        mid_f = from_key(mid)
        cnt = count(lambda sc: sc >= mid_f)
        ge = jnp.logical_and(active > 0, cnt >= topk)
        lt = jnp.logical_and(active > 0, cnt < topk)
        lo = jnp.where(ge, mid, lo)
        c_lo = jnp.where(ge, cnt, c_lo)
        hi = jnp.where(lt, mid, hi)
        finished = jnp.logical_or(cnt == topk, hi == lo + 1)
        active = jnp.where(finished, 0, active)
        return lo, hi, c_lo, active

    def bis_body(c):
        lo, hi, c_lo, active, _, it = c
        for j in range(BISECT_STEPS_PER_CHECK):
            lo, hi, c_lo, active = bis_step(lo, hi, c_lo, active, it + j)
        return lo, hi, c_lo, active, jnp.sum(active), it + BISECT_STEPS_PER_CHECK

    lo, _, c_lo, _, _, _ = lax.while_loop(
        bis_cond, bis_body,
        (to_key(col_min), to_key(col_max) + 1, limit.astype(F32), active0, jnp.sum(active0),
         jnp.int32(0)))
    thr = jnp.where(need, from_key(lo), LOWEST)

    n_tied_rows = jnp.sum(jnp.logical_and(need, c_lo != topk).astype(jnp.int32))

    @pl.when(n_tied_rows > 0)
    def _():
        quota = topk - count(lambda sc: sc > thr)
        r_i = lax.broadcasted_iota(jnp.int32, (KEY_TILE, KEY_TILE), 0)
        c_i = lax.broadcasted_iota(jnp.int32, (KEY_TILE, KEY_TILE), 1)
        strict_lower = jnp.where(r_i > c_i, 1.0, 0.0).astype(BF16)

        def demote_tile(kt, before):
            sc = score_ref[kt]
            tie = sc == thr
            tie_b = jnp.where(tie, 1.0, 0.0).astype(BF16)
            rank = before + jnp.dot(strict_lower, tie_b, preferred_element_type=F32)
            score_ref[kt] = jnp.where(jnp.logical_and(tie, rank >= quota), -jnp.inf, sc)
            return before + jnp.sum(_fold_rows(jnp.where(tie, 1.0, 0.0), jnp.sum),
                                    axis=0, keepdims=True)

        lax.fori_loop(0, n_tiles, demote_tile, jnp.zeros((1, Q_TILE), F32))

    m_ref[...] = jnp.full(m_ref.shape, -jnp.inf, F32)
    acc_ref[...] = jnp.zeros(acc_ref.shape, F32)

    def attn_tiles(tiles):
        pos = pos_ref[...]
        alphas, shifts = [], []
        for slot, (kt, diagonal) in enumerate(tiles):
            koff = pl.multiple_of(kt * KEY_TILE, KEY_TILE)
            mask_bias = jnp.where(score_ref[kt] >= thr, 0.0, -MASK_BIG)
            if diagonal:
                ahead = (2.0 * LOG2E) * jnp.maximum(key_iota - q_iota, 0).astype(F32)
            rel_q = (qpos - kt * KEY_TILE).astype(F32)
            for h in range(DSA_HEADS):
                pair = slice((h // 2) * LANES, (h // 2 + 1) * LANES)
                lhs = jnp.concatenate([k_ref[pl.ds(koff, KEY_TILE), pair], pos], axis=1)
                s = jnp.dot(lhs, qm_ref[h], preferred_element_type=F32) + mask_bias
                if diagonal:
                    s = s - slopes[h] * ahead
                s_ref[slot, h] = s
                offset = (LOG2E * slopes[h]) * rel_q
                m_old = m_ref[h:h + 1, :]
                m_new = jnp.maximum(
                    m_old, jnp.max(_fold_rows(s, jnp.max), axis=0, keepdims=True) - offset)
                m_ref[h:h + 1, :] = m_new
                alphas.append(jnp.exp2(m_old - m_new))
                shifts.append(m_new + offset)
        for slot, (kt, _) in enumerate(tiles):
            for h in range(DSA_HEADS):
                j = slot * DSA_HEADS + h
                p = jnp.exp2(s_ref[slot, h] - shifts[j]).astype(BF16)
                pv = jnp.dot(vt_ref[kt, h], p, preferred_element_type=F32)
                acc_ref[h] = alphas[j] * acc_ref[h] + pv

    def attn_body(j, carry):
        attn_tiles([(ATTN_GROUP * j + t, False) for t in range(ATTN_GROUP)])
        return carry

    lax.fori_loop(0, i // ATTN_GROUP, attn_body, 0)

    for rem in range(ATTN_GROUP):
        @pl.when(i % ATTN_GROUP == rem)
        def _():
            attn_tiles([(i - rem + t, False) for t in range(rem)] + [(i, True)])

    outs = [acc_ref[h, :DSA_HEAD_DIM, :] / acc_ref[h, DSA_HEAD_DIM:DSA_HEAD_DIM + 1, :]
            for h in range(DSA_HEADS)]
    o_ref[...] = jnp.concatenate(outs, axis=0).T.astype(o_ref.dtype)


def _dsa(qt, iqt, iwt, k3, vt, ik3, batch, seq):
    assert Q_TILE == KEY_TILE and Q_TILE % CHUNK == 0
    n = batch * seq
    nqb = seq // Q_TILE
    n_key_tiles = seq // KEY_TILE
    t_spec = lambda rows: pl.BlockSpec((None, None, rows, Q_TILE), lambda b, i: (b, i, 0, 0))
    return pl.pallas_call(
        _dsa_kernel,
        grid=(batch, nqb),
        in_specs=[
            t_spec(512), t_spec(512), t_spec(IW_ROWS),
            pl.BlockSpec((None, seq, 512), lambda b, i: (b, 0, 0)),
            pl.BlockSpec((None, n_key_tiles, DSA_HEADS, V_ROWS, KEY_TILE), lambda b, i: (b, 0, 0, 0, 0)),
            pl.BlockSpec((None, seq, 2 * IDX_DIM), lambda b, i: (b, 0, 0)),
        ],
        out_specs=pl.BlockSpec((Q_TILE, 512), lambda b, i: (b * nqb + i, 0)),
        out_shape=jax.ShapeDtypeStruct((n, 512), BF16),
        scratch_shapes=[
            pltpu.VMEM((n_key_tiles, KEY_TILE, Q_TILE), F32),
            pltpu.VMEM((DSA_HEADS, 2 * LANES, Q_TILE), BF16),
            pltpu.VMEM((IDX_HEADS, LANES, Q_TILE), BF16),
            pltpu.VMEM((KEY_TILE, LANES), BF16),
            pltpu.VMEM((ATTN_GROUP, DSA_HEADS, KEY_TILE, Q_TILE), F32),
            pltpu.VMEM((DSA_HEADS, V_ROWS, Q_TILE), F32),
            pltpu.VMEM((DSA_HEADS, Q_TILE), F32),
            pltpu.VMEM((2, SUBLANES, Q_TILE), F32),
        ],
        compiler_params=pltpu.CompilerParams(
            dimension_semantics=("arbitrary", "arbitrary"), vmem_limit_bytes=VMEM_LIMIT_BYTES),
        name="dsa",
    )(qt, iqt, iwt, k3, vt, ik3)


def _hgrn_kernel(hq_ref, hf_ref, hi_ref, hg_ref, lb_ref, gain_ref, r_ref,
                 st_ref, qs_ref, qi_ref, ki_ref, ks_ref, dec_ref, a_ref, oi_ref, kv_ref, sp_ref):
    @pl.when(pl.program_id(1) == 0)
    def _():
        st_ref[...] = jnp.zeros(st_ref.shape, F32)

    n_chunks = HG_ROWS // CHUNK
    r_i = lax.broadcasted_iota(jnp.int32, (CHUNK, CHUNK), 0)
    c_i = lax.broadcasted_iota(jnp.int32, (CHUNK, CHUNK), 1)
    tril = r_i >= c_i
    tril_b = jnp.where(tril, 1.0, 0.0).astype(BF16)
    lb = lb_ref[...]
    gain = gain_ref[...]
    chunk_rows = lambda c: slice(c * CHUNK, (c + 1) * CHUNK)
    head_cols = lambda h: slice(h * HG_DK, (h + 1) * HG_DK)

    for c in range(n_chunks):
        rows = chunk_rows(c)
        f = lb + (1.0 - lb) * jax.nn.sigmoid(hf_ref[rows, :])
        logf = jnp.log(f)
        kk = 1.0 - f
        t0 = logf.astype(BF16)
        r1 = logf - t0.astype(F32)
        t1 = r1.astype(BF16)
        t2 = (r1 - t1.astype(F32)).astype(BF16)
        g = (jnp.dot(tril_b, t0, preferred_element_type=F32)
             + jnp.dot(tril_b, t1, preferred_element_type=F32)
             + jnp.dot(tril_b, t2, preferred_element_type=F32))
        g_last = g[CHUNK - 1:CHUNK, :]
        g_mid = g[CHUNK // 2 - 1:CHUNK // 2, :]
        q_intra = hq_ref[rows, :] * jnp.exp(g - g_mid)
        k_intra = kk * jnp.exp(g_mid - g)
        qi_ref[rows, :] = q_intra.astype(BF16)
        ki_ref[rows, :] = k_intra.astype(BF16)
        qs_ref[rows, :] = (q_intra * jnp.exp(g_mid)).astype(BF16)
        ks_ref[rows, :] = (k_intra * jnp.exp(g_last - g_mid)).astype(BF16)
        dec_ref[c:c + 1, :] = jnp.exp(g_last)

    items = [(c, h) for c in range(n_chunks) for h in range(HG_HEADS)]
    for c, h in items:
        a = lax.dot_general(qi_ref[chunk_rows(c), head_cols(h)], ki_ref[chunk_rows(c), head_cols(h)],
                            NT_DIMS, preferred_element_type=F32)
        a_ref[c, h] = jnp.where(tril, a, 0.0).astype(BF16)
    for c, h in items:
        kv_ref[c, h] = lax.dot_general(hi_ref[chunk_rows(c), head_cols(h)],
                                       ks_ref[chunk_rows(c), head_cols(h)], TN_DIMS,
                                       preferred_element_type=F32)
    for c, h in items:
        oi_ref[chunk_rows(c), head_cols(h)] = jnp.dot(
            a_ref[c, h], hi_ref[chunk_rows(c), head_cols(h)], preferred_element_type=F32)

    for h in range(HG_HEADS):
        st = st_ref[h]
        for c in range(n_chunks):
            sp_ref[c, h] = st.astype(BF16)
            st = st * dec_ref[c:c + 1, head_cols(h)] + kv_ref[c, h]
        st_ref[h] = st

    for c in range(n_chunks):
        rows = chunk_rows(c)
        for h in range(HG_HEADS):
            sl = head_cols(h)
            o = oi_ref[rows, sl] + lax.dot_general(qs_ref[rows, sl], sp_ref[c, h], NT_DIMS,
                                                   preferred_element_type=F32)
            ms = jnp.mean(o * o, axis=1, keepdims=True)
            on = o * lax.rsqrt(ms + RMS_EPS)
            gate = hg_ref[rows, sl].astype(F32)
            r = on * gain[:, sl] * (gate * jax.nn.sigmoid(gate))
            r_ref[rows, sl] = r.astype(r_ref.dtype)


def _hgrn(hq, hf, hi, hg, lb, gain, batch, seq):
    n = hq.shape[0]
    nb = seq // HG_ROWS
    row_spec = pl.BlockSpec((HG_ROWS, 512), lambda b, t: (b * nb + t, 0))
    vec_spec = pl.BlockSpec((1, 512), lambda b, t: (0, 0))
    return pl.pallas_call(
        _hgrn_kernel,
        grid=(batch, nb),
        in_specs=[row_spec, row_spec, row_spec, row_spec, vec_spec, vec_spec],
        out_specs=row_spec,
        out_shape=jax.ShapeDtypeStruct((n, 512), BF16),
        scratch_shapes=[
            pltpu.VMEM((HG_HEADS, HG_DK, HG_DK), F32),
            pltpu.VMEM((HG_ROWS, 512), BF16),
            pltpu.VMEM((HG_ROWS, 512), BF16),
            pltpu.VMEM((HG_ROWS, 512), BF16),
            pltpu.VMEM((HG_ROWS, 512), BF16),
            pltpu.VMEM((HG_ROWS // CHUNK, 512), F32),
            pltpu.VMEM((HG_ROWS // CHUNK, HG_HEADS, CHUNK, CHUNK), BF16),
            pltpu.VMEM((HG_ROWS, 512), F32),
            pltpu.VMEM((HG_ROWS // CHUNK, HG_HEADS, HG_DK, HG_DK), F32),
            pltpu.VMEM((HG_ROWS // CHUNK, HG_HEADS, HG_DK, HG_DK), BF16),
        ],
        compiler_params=pltpu.CompilerParams(
            dimension_semantics=("arbitrary", "arbitrary"), vmem_limit_bytes=VMEM_LIMIT_BYTES),
        name="hgrn",
    )(hq, hf, hi, hg, lb, gain)


def _layer_norm(y, g, b):
    mu = jnp.mean(y, axis=1, keepdims=True)
    d = y - mu
    var = jnp.mean(d * d, axis=1, keepdims=True)
    return d * lax.rsqrt(var + LN_EPS) * g + b


def _tail_kernel(a_ref, r_ref, x_ref, wo_ref, g1_ref, b1_ref, wg_ref, wu_ref, wd_ref,
                 g2_ref, b2_ref, o_ref):
    halves = [slice(j * (TAIL_ROWS // 2), (j + 1) * (TAIL_ROWS // 2)) for j in range(2)]
    mix = [jnp.dot(a_ref[rows, :], wo_ref[:DSA_WIDTH, :], preferred_element_type=F32)
           + jnp.dot(r_ref[rows, :], wo_ref[DSA_WIDTH:, :], preferred_element_type=F32)
           for rows in halves]
    x1, h = [], []
    for j, rows in enumerate(halves):
        x1.append(_layer_norm(ALPHA * x_ref[rows, :] + mix[j], g1_ref[...], b1_ref[...]))
        xb = x1[j].astype(BF16)
        gate = jnp.dot(xb, wg_ref[...], preferred_element_type=F32)
        up = jnp.dot(xb, wu_ref[...], preferred_element_type=F32)
        h.append((gate * jax.nn.sigmoid(gate) * up).astype(BF16))
    for j, rows in enumerate(halves):
        ff = jnp.dot(h[j], wd_ref[...], preferred_element_type=F32)
        o_ref[rows, :] = _layer_norm(ALPHA * x1[j] + ff, g2_ref[...], b2_ref[...])


def _tail(a, r, x2, wo, g1, b1, wg, wu, wd, g2, b2):
    n = x2.shape[0]
    half = pl.BlockSpec((TAIL_ROWS, 512), lambda i: (i, 0))
    full = pl.BlockSpec((TAIL_ROWS, D_MODEL), lambda i: (i, 0))
    vec = pl.BlockSpec((1, D_MODEL), lambda i: (0, 0))
    const = lambda w: pl.BlockSpec(w.shape, lambda i: (0, 0), pipeline_mode=pl.Buffered(1))
    return pl.pallas_call(
        _tail_kernel,
        grid=(n // TAIL_ROWS,),
        in_specs=[half, half, full, const(wo), vec, vec, const(wg), const(wu), const(wd), vec, vec],
        out_specs=full,
        out_shape=jax.ShapeDtypeStruct((n, D_MODEL), F32),
        compiler_params=pltpu.CompilerParams(
            dimension_semantics=("arbitrary",), vmem_limit_bytes=VMEM_LIMIT_BYTES),
        name="mix_ffn",
    )(a, r, x2, wo, g1, b1, wg, wu, wd, g2, b2)


def _proj_weights(w):
    pts = [0]
    for s in SPLIT_SIZES:
        pts.append(pts[-1] + s)
    col = lambda j: w[:, pts[j]:pts[j + 1]]
    wn = jnp.concatenate([col(1), col(6), col(7), col(8), col(9), col(4), col(4)], axis=1)
    wt = jnp.concatenate(
        [col(0), col(3), col(2), jnp.pad(col(5), ((0, 0), (0, IW_ROWS - IDX_HEADS)))], axis=1).T
    return wn.astype(BF16), wt.astype(BF16)


def kernel(x, w_in, w_out, hg_lb_logits, hg_norm_g, ln1_g, ln1_b, w_gate, w_up, w_down, ln2_g, ln2_b):
    batch, seq, _ = x.shape
    n = batch * seq
    lb_all = jnp.cumsum(jax.nn.softmax(hg_lb_logits.astype(F32), axis=0), axis=0)

    x2 = x.reshape(n, D_MODEL)
    for l in range(DEPTH):
        wn, wt = _proj_weights(w_in[l])
        k, ik, hq, hf, hi, hg, qt, iqt, vt, iwt = _project(x2, wn, wt, batch, seq)
        a = _dsa(qt, iqt, iwt, k.reshape(batch, seq, 512), vt,
                 ik.reshape(batch, seq, 2 * IDX_DIM), batch, seq)
        r = _hgrn(hq, hf, hi, hg, lb_all[l].reshape(1, 512), hg_norm_g[l].reshape(1, 512).astype(F32),
                  batch, seq)
        x2 = _tail(a, r, x2, w_out[l].astype(BF16), ln1_g[l].reshape(1, D_MODEL),
                   ln1_b[l].reshape(1, D_MODEL), w_gate[l].astype(BF16), w_up[l].astype(BF16),
                   w_down[l].astype(BF16), ln2_g[l].reshape(1, D_MODEL), ln2_b[l].reshape(1, D_MODEL))
    return x2.reshape(batch, seq, D_MODEL)
```

```python
import numpy as np
import jax
import jax.numpy as jnp
from jax import lax
from jax.experimental import pallas as pl
from jax.experimental.pallas import tpu as pltpu

D_MODEL = 1024
CHUNK = 64
DSA_WIDTH = 512
DSA_HEAD_DIM = 64
DSA_HEADS = 8
IDX_HEADS = 8
IDX_DIM = 64
TOPK_MAX = 256
HG_WIDTH = 512
HG_DK = 128
HG_HEADS = 4
D_FF = 2816
DEPTH = 1
ALPHA = (2.0 * DEPTH) ** 0.25
LN_EPS = 1e-5
RMS_EPS = 1e-6
SPLIT_SIZES = (512, 512, 512, 512, 64, 8, 512, 512, 512, 512)

F32 = jnp.float32
BF16 = jnp.bfloat16

LANES = 128
SUBLANES = 8
VMEM_LIMIT_BYTES = 56 * 1024 * 1024

Q_TILE = 256
KEY_TILE = 256
IW_ROWS = 16
PROJ_ROWS = 512
HG_ROWS = 512
TAIL_ROWS = 512
MASK_BIG = 1e30
LOWEST = -3.0e38
BISECT_VALUE_STEPS = 8
BISECT_FIRST_STEPS = 15
BISECT_STEPS_PER_CHECK = 2
SCORE_GROUP = 4
ATTN_GROUP = 4
LOG2E = 1.4426950408889634
V_ROWS = DSA_HEAD_DIM + 16

NT_DIMS = (((1,), (1,)), ((), ()))
TN_DIMS = (((0,), (0,)), ((), ()))


def _proj_kernel(x_ref, wn_ref, wt_ref, k_ref, ik_ref, hq_ref, hf_ref, hi_ref, hg_ref,
                 qt_ref, iqt_ref, vt_ref, iwt_ref):
    xb = x_ref[...].astype(BF16)

    def nn(j, width=512):
        return jnp.dot(xb, wn_ref[:, j * 512:j * 512 + width], preferred_element_type=F32)

    k_ref[...] = nn(0).astype(BF16)
    hq_ref[...] = nn(1)
    hf_ref[...] = nn(2)
    hi_ref[...] = nn(3).astype(BF16)
    hg_ref[...] = nn(4).astype(BF16)
    ik_ref[...] = nn(5, 2 * IDX_DIM).astype(BF16)
    t = lax.dot_general(wt_ref[...], xb, NT_DIMS, preferred_element_type=F32)
    for j in range(PROJ_ROWS // Q_TILE):
        cols = slice(j * Q_TILE, (j + 1) * Q_TILE)
        qt_ref[j] = (t[0:512, cols] * (LOG2E * DSA_HEAD_DIM ** -0.5)).astype(BF16)
        iqt_ref[j] = t[512:1024, cols].astype(BF16)
        for h in range(DSA_HEADS):
            lo = 1024 + h * DSA_HEAD_DIM
            vt_ref[j, h, :DSA_HEAD_DIM, :] = t[lo:lo + DSA_HEAD_DIM, cols].astype(BF16)
            vt_ref[j, h, DSA_HEAD_DIM:, :] = jnp.ones((V_ROWS - DSA_HEAD_DIM, Q_TILE), BF16)
        iwt_ref[j] = t[1536:1536 + IW_ROWS, cols]


def _project(x2, wn, wt, batch, seq):
    n = x2.shape[0]
    nb = seq // PROJ_ROWS
    tiles = PROJ_ROWS // Q_TILE
    n_tiles = seq // Q_TILE
    row_spec = lambda w: pl.BlockSpec((PROJ_ROWS, w), lambda i: (i, 0))
    t_shape = lambda rows, dt: jax.ShapeDtypeStruct((batch, n_tiles, rows, Q_TILE), dt)
    t_spec = lambda rows: pl.BlockSpec((None, tiles, rows, Q_TILE), lambda i: (i // nb, i % nb, 0, 0))
    out_shape = (
        jax.ShapeDtypeStruct((n, 512), BF16),
        jax.ShapeDtypeStruct((n, 2 * IDX_DIM), BF16),
        jax.ShapeDtypeStruct((n, 512), F32),
        jax.ShapeDtypeStruct((n, 512), F32),
        jax.ShapeDtypeStruct((n, 512), BF16),
        jax.ShapeDtypeStruct((n, 512), BF16),
        t_shape(512, BF16),
        t_shape(512, BF16),
        jax.ShapeDtypeStruct((batch, n_tiles, DSA_HEADS, V_ROWS, Q_TILE), BF16),
        t_shape(IW_ROWS, F32),
    )
    out_specs = (
        row_spec(512), row_spec(2 * IDX_DIM), row_spec(512), row_spec(512), row_spec(512),
        row_spec(512), t_spec(512), t_spec(512),
        pl.BlockSpec((None, tiles, DSA_HEADS, V_ROWS, Q_TILE), lambda i: (i // nb, i % nb, 0, 0, 0)),
        t_spec(IW_ROWS),
    )
    return pl.pallas_call(
        _proj_kernel,
        grid=(n // PROJ_ROWS,),
        in_specs=[
            pl.BlockSpec((PROJ_ROWS, D_MODEL), lambda i: (i, 0)),
            pl.BlockSpec(wn.shape, lambda i: (0, 0)),
            pl.BlockSpec(wt.shape, lambda i: (0, 0)),
        ],
        out_specs=out_specs,
        out_shape=out_shape,
        compiler_params=pltpu.CompilerParams(
            dimension_semantics=("arbitrary",), vmem_limit_bytes=VMEM_LIMIT_BYTES),
        name="proj",
    )(x2, wn, wt)


def _fold_rows(x, op):
    return op(x.reshape(x.shape[0] // SUBLANES, SUBLANES, x.shape[1]), axis=0)


def _split3(c):
    out = []
    for _ in range(3):
        t = float(np.asarray(c, dtype=BF16))
        out.append(t)
        c = c - t
    return out


def _dsa_kernel(qt_ref, iqt_ref, iwt_ref, k_ref, vt_ref, ik_ref, o_ref,
                score_ref, qm_ref, iqm_ref, pos_ref, s_ref, acc_ref, m_ref, ext_ref):
    i = pl.program_id(1)
    topk = TOPK_MAX
    idx_scale = (IDX_DIM ** -0.5) * (IDX_HEADS ** -0.5)
    slopes = [2.0 ** (-8.0 * (h + 1) / DSA_HEADS) for h in range(DSA_HEADS)]

    row_i = lax.broadcasted_iota(jnp.int32, (LANES, Q_TILE), 0)
    even_rows = row_i < DSA_HEAD_DIM
    for p in range(DSA_HEADS // 2):
        rows = slice(p * LANES, (p + 1) * LANES)
        qp = qt_ref[rows, :].astype(F32)
        iqp = iqt_ref[rows, :].astype(F32)
        qm_ref[2 * p, :LANES, :] = jnp.where(even_rows, qp, 0.0).astype(BF16)
        qm_ref[2 * p + 1, :LANES, :] = jnp.where(even_rows, 0.0, qp).astype(BF16)
        iqm_ref[2 * p] = jnp.where(even_rows, iqp, 0.0).astype(BF16)
        iqm_ref[2 * p + 1] = jnp.where(even_rows, 0.0, iqp).astype(BF16)
    for h in range(DSA_HEADS):
        c0, c1, c2 = _split3(LOG2E * slopes[h])
        coef = jnp.where(row_i == 0, c0, jnp.where(row_i == 1, c1, jnp.where(row_i == 2, c2, 0.0)))
        qm_ref[h, LANES:, :] = coef.astype(BF16)
    pos_ref[...] = jnp.where(
        lax.broadcasted_iota(jnp.int32, (KEY_TILE, LANES), 1) < 3,
        lax.broadcasted_iota(jnp.int32, (KEY_TILE, LANES), 0), 0).astype(F32).astype(BF16)

    w = iwt_ref[...]
    q_iota = lax.broadcasted_iota(jnp.int32, (1, Q_TILE), 1)
    key_iota = lax.broadcasted_iota(jnp.int32, (KEY_TILE, 1), 0)
    qpos = i * Q_TILE + q_iota
    limit = (qpos // CHUNK + 1) * CHUNK

    def score_tile(kt, carry, diagonal):
        rmax, rmin = carry
        koff = pl.multiple_of(kt * KEY_TILE, KEY_TILE)
        ik = ik_ref[pl.ds(koff, KEY_TILE), :]
        acc = jnp.zeros((KEY_TILE, Q_TILE), F32)
        for h in range(IDX_HEADS):
            s = jnp.dot(ik, iqm_ref[h], preferred_element_type=F32)
            acc = acc + w[h:h + 1, :] * jnp.maximum(s, 0.0)
        acc = acc * idx_scale
        if diagonal:
            adm = (kt * KEY_TILE + key_iota) < limit
            lo_fill = jnp.where(adm, acc, -jnp.inf)
            hi_fill = jnp.where(adm, acc, jnp.inf)
        else:
            lo_fill = hi_fill = acc
        score_ref[kt] = lo_fill
        return (jnp.maximum(rmax, _fold_rows(lo_fill, jnp.max)),
                jnp.minimum(rmin, _fold_rows(hi_fill, jnp.min)))

    def score_group(j, carry):
        for t in range(SCORE_GROUP):
            carry = score_tile(SCORE_GROUP * j + t, carry, False)
        return carry

    ext_ref[0], ext_ref[1] = lax.fori_loop(
        0, i // SCORE_GROUP, score_group,
        (jnp.full((SUBLANES, Q_TILE), -jnp.inf, F32), jnp.full((SUBLANES, Q_TILE), jnp.inf, F32)))

    for rem in range(SCORE_GROUP):
        @pl.when(i % SCORE_GROUP == rem)
        def _():
            carry = (ext_ref[0], ext_ref[1])
            for t in range(rem):
                carry = score_tile(i - rem + t, carry, False)
            ext_ref[0], ext_ref[1] = score_tile(i, carry, True)

    n_tiles = i + 1
    col_max = jnp.max(ext_ref[0], axis=0, keepdims=True)
    col_min = jnp.min(ext_ref[1], axis=0, keepdims=True)

    def to_key(v):
        b = lax.bitcast_convert_type(v, jnp.int32)
        return jnp.where(b < 0, b ^ 0x7FFFFFFF, b)

    def from_key(kv):
        return lax.bitcast_convert_type(jnp.where(kv < 0, kv ^ 0x7FFFFFFF, kv), F32)

    @pl.when(n_tiles < score_ref.shape[0])
    def _():
        score_ref[n_tiles] = jnp.full((KEY_TILE, Q_TILE), -jnp.inf, F32)

    def count(pred):
        def body(j, acc):
            for kt in (2 * j, 2 * j + 1):
                acc = acc + _fold_rows(jnp.where(pred(score_ref[kt]), 1.0, 0.0), jnp.sum)
            return acc
        acc = lax.fori_loop(0, (n_tiles + 1) // 2, body, jnp.zeros((SUBLANES, Q_TILE), F32))
        return jnp.sum(acc, axis=0, keepdims=True)

    need = limit > topk
    active0 = need.astype(jnp.int32)

    def bis_cond(c):
        return c[4] > 0

    def bis_step(lo, hi, c_lo, active, it):
        key_mid = (lo & hi) + ((lo ^ hi) >> 1) + ((lo ^ hi) & 1)
        lo_f = from_key(lo)
        val_mid = to_key(lo_f + 0.5 * (from_key(hi) - lo_f))
        val_mid = jnp.minimum(jnp.maximum(val_mid, lo + 1), hi)
        mid = jnp.where(it < BISECT_VALUE_STEPS, val_mid, key_mid)
        mid_f = from_key(mid)
        cnt = count(lambda sc: sc >= mid_f)
        ge = jnp.logical_and(active > 0, cnt >= topk)
        lt = jnp.logical_and(active > 0, cnt < topk)
        lo = jnp.where(ge, mid, lo)
        c_lo = jnp.where(ge, cnt, c_lo)
        hi = jnp.where(lt, mid, hi)
        finished = jnp.logical_or(cnt == topk, hi == lo + 1)
        active = jnp.where(finished, 0, active)
        return lo, hi, c_lo, active

    def bis_body(c):
        lo, hi, c_lo, active, _, it = c
        for j in range(BISECT_STEPS_PER_CHECK):
            lo, hi, c_lo, active = bis_step(lo, hi, c_lo, active, it + j)
        return lo, hi, c_lo, active, jnp.sum(active), it + BISECT_STEPS_PER_CHECK

    n_first = jnp.where(jnp.sum(active0) > 0, BISECT_FIRST_STEPS, 0)
    first = lax.fori_loop(
        0, n_first, lambda it, c: bis_step(*c, it),
        (to_key(col_min), to_key(col_max) + 1, limit.astype(F32), active0))
    lo, _, c_lo, _, _, _ = lax.while_loop(
        bis_cond, bis_body, first + (jnp.sum(first[3]), n_first))
    thr = jnp.where(need, from_key(lo), LOWEST)

    n_tied_rows = jnp.sum(jnp.logical_and(need, c_lo != topk).astype(jnp.int32))

    @pl.when(n_tied_rows > 0)
    def _():
        quota = topk - count(lambda sc: sc > thr)
        r_i = lax.broadcasted_iota(jnp.int32, (KEY_TILE, KEY_TILE), 0)
        c_i = lax.broadcasted_iota(jnp.int32, (KEY_TILE, KEY_TILE), 1)
        strict_lower = jnp.where(r_i > c_i, 1.0, 0.0).astype(BF16)

        def demote_tile(kt, before):
            sc = score_ref[kt]
            tie = sc == thr
            tie_b = jnp.where(tie, 1.0, 0.0).astype(BF16)
            rank = before + jnp.dot(strict_lower, tie_b, preferred_element_type=F32)
            score_ref[kt] = jnp.where(jnp.logical_and(tie, rank >= quota), -jnp.inf, sc)
            return before + jnp.sum(_fold_rows(jnp.where(tie, 1.0, 0.0), jnp.sum),
                                    axis=0, keepdims=True)

        lax.fori_loop(0, n_tiles, demote_tile, jnp.zeros((1, Q_TILE), F32))

    m_ref[...] = jnp.full(m_ref.shape, -jnp.inf, F32)
    acc_ref[...] = jnp.zeros(acc_ref.shape, F32)

    def attn_tiles(tiles):
        pos = pos_ref[...]
        alphas, shifts = [], []
        for slot, (kt, diagonal) in enumerate(tiles):
            koff = pl.multiple_of(kt * KEY_TILE, KEY_TILE)
            mask_bias = jnp.where(score_ref[kt] >= thr, 0.0, -MASK_BIG)
            if diagonal:
                ahead = (2.0 * LOG2E) * jnp.maximum(key_iota - q_iota, 0).astype(F32)
            rel_q = (qpos - kt * KEY_TILE).astype(F32)
            for h in range(DSA_HEADS):
                pair = slice((h // 2) * LANES, (h // 2 + 1) * LANES)
                lhs = jnp.concatenate([k_ref[pl.ds(koff, KEY_TILE), pair], pos], axis=1)
                s = jnp.dot(lhs, qm_ref[h], preferred_element_type=F32) + mask_bias
                if diagonal:
                    s = s - slopes[h] * ahead
                s_ref[slot, h] = s
                offset = (LOG2E * slopes[h]) * rel_q
                m_old = m_ref[h:h + 1, :]
                m_new = jnp.maximum(
                    m_old, jnp.max(_fold_rows(s, jnp.max), axis=0, keepdims=True) - offset)
                m_ref[h:h + 1, :] = m_new
                alphas.append(jnp.exp2(m_old - m_new))
                shifts.append(m_new + offset)
        for slot, (kt, _) in enumerate(tiles):
            for h in range(DSA_HEADS):
                j = slot * DSA_HEADS + h
                p = jnp.exp2(s_ref[slot, h] - shifts[j]).astype(BF16)
                pv = jnp.dot(vt_ref[kt, h], p, preferred_element_type=F32)
                acc_ref[h] = alphas[j] * acc_ref[h] + pv

    def attn_body(j, carry):
        attn_tiles([(ATTN_GROUP * j + t, False) for t in range(ATTN_GROUP)])
        return carry

    lax.fori_loop(0, i // ATTN_GROUP, attn_body, 0)

    for rem in range(ATTN_GROUP):
        @pl.when(i % ATTN_GROUP == rem)
        def _():
            attn_tiles([(i - rem + t, False) for t in range(rem)] + [(i, True)])

    outs = [acc_ref[h, :DSA_HEAD_DIM, :] / acc_ref[h, DSA_HEAD_DIM:DSA_HEAD_DIM + 1, :]
            for h in range(DSA_HEADS)]
    o_ref[...] = jnp.concatenate(outs, axis=0).T.astype(o_ref.dtype)


def _dsa(qt, iqt, iwt, k3, vt, ik3, batch, seq):
    assert Q_TILE == KEY_TILE and Q_TILE % CHUNK == 0
    n = batch * seq
    nqb = seq // Q_TILE
    n_key_tiles = seq // KEY_TILE
    t_spec = lambda rows: pl.BlockSpec((None, None, rows, Q_TILE), lambda b, i: (b, i, 0, 0))
    return pl.pallas_call(
        _dsa_kernel,
        grid=(batch, nqb),
        in_specs=[
            t_spec(512), t_spec(512), t_spec(IW_ROWS),
            pl.BlockSpec((None, seq, 512), lambda b, i: (b, 0, 0)),
            pl.BlockSpec((None, n_key_tiles, DSA_HEADS, V_ROWS, KEY_TILE), lambda b, i: (b, 0, 0, 0, 0)),
            pl.BlockSpec((None, seq, 2 * IDX_DIM), lambda b, i: (b, 0, 0)),
        ],
        out_specs=pl.BlockSpec((Q_TILE, 512), lambda b, i: (b * nqb + i, 0)),
        out_shape=jax.ShapeDtypeStruct((n, 512), BF16),
        scratch_shapes=[
            pltpu.VMEM((n_key_tiles, KEY_TILE, Q_TILE), F32),
            pltpu.VMEM((DSA_HEADS, 2 * LANES, Q_TILE), BF16),
            pltpu.VMEM((IDX_HEADS, LANES, Q_TILE), BF16),
            pltpu.VMEM((KEY_TILE, LANES), BF16),
            pltpu.VMEM((ATTN_GROUP, DSA_HEADS, KEY_TILE, Q_TILE), F32),
            pltpu.VMEM((DSA_HEADS, V_ROWS, Q_TILE), F32),
            pltpu.VMEM((DSA_HEADS, Q_TILE), F32),
            pltpu.VMEM((2, SUBLANES, Q_TILE), F32),
        ],
        compiler_params=pltpu.CompilerParams(
            dimension_semantics=("arbitrary", "arbitrary"), vmem_limit_bytes=VMEM_LIMIT_BYTES),
        name="dsa",
    )(qt, iqt, iwt, k3, vt, ik3)


def _hgrn_kernel(hq_ref, hf_ref, hi_ref, hg_ref, lb_ref, gain_ref, r_ref,
                 st_ref, qs_ref, qi_ref, ki_ref, ks_ref, dec_ref, a_ref, oi_ref, kv_ref, sp_ref):
    @pl.when(pl.program_id(1) == 0)
    def _():
        st_ref[...] = jnp.zeros(st_ref.shape, F32)

    n_chunks = HG_ROWS // CHUNK
    r_i = lax.broadcasted_iota(jnp.int32, (CHUNK, CHUNK), 0)
    c_i = lax.broadcasted_iota(jnp.int32, (CHUNK, CHUNK), 1)
    tril = r_i >= c_i
    tril_b = jnp.where(tril, 1.0, 0.0).astype(BF16)
    lb = lb_ref[...]
    gain = gain_ref[...]
    chunk_rows = lambda c: slice(c * CHUNK, (c + 1) * CHUNK)
    head_cols = lambda h: slice(h * HG_DK, (h + 1) * HG_DK)

    for c in range(n_chunks):
        rows = chunk_rows(c)
        f = lb + (1.0 - lb) * jax.nn.sigmoid(hf_ref[rows, :])
        logf = jnp.log(f)
        kk = 1.0 - f
        t0 = logf.astype(BF16)
        r1 = logf - t0.astype(F32)
        t1 = r1.astype(BF16)
        t2 = (r1 - t1.astype(F32)).astype(BF16)
        g = (jnp.dot(tril_b, t0, preferred_element_type=F32)
             + jnp.dot(tril_b, t1, preferred_element_type=F32)
             + jnp.dot(tril_b, t2, preferred_element_type=F32))
        g_last = g[CHUNK - 1:CHUNK, :]
        g_mid = g[CHUNK // 2 - 1:CHUNK // 2, :]
        q_intra = hq_ref[rows, :] * jnp.exp(g - g_mid)
        k_intra = kk * jnp.exp(g_mid - g)
        qi_ref[rows, :] = q_intra.astype(BF16)
        ki_ref[rows, :] = k_intra.astype(BF16)
        qs_ref[rows, :] = (q_intra * jnp.exp(g_mid)).astype(BF16)
        ks_ref[rows, :] = (k_intra * jnp.exp(g_last - g_mid)).astype(BF16)
        dec_ref[c:c + 1, :] = jnp.exp(g_last)

    items = [(c, h) for c in range(n_chunks) for h in range(HG_HEADS)]
    for c, h in items:
        a = lax.dot_general(qi_ref[chunk_rows(c), head_cols(h)], ki_ref[chunk_rows(c), head_cols(h)],
                            NT_DIMS, preferred_element_type=F32)
        a_ref[c, h] = jnp.where(tril, a, 0.0).astype(BF16)
    for c, h in items:
        kv_ref[c, h] = lax.dot_general(hi_ref[chunk_rows(c), head_cols(h)],
                                       ks_ref[chunk_rows(c), head_cols(h)], TN_DIMS,
                                       preferred_element_type=F32)
    for c, h in items:
        oi_ref[chunk_rows(c), head_cols(h)] = jnp.dot(
            a_ref[c, h], hi_ref[chunk_rows(c), head_cols(h)], preferred_element_type=F32)

    for h in range(HG_HEADS):
        st = st_ref[h]
        for c in range(n_chunks):
            sp_ref[c, h] = st.astype(BF16)
            st = st * dec_ref[c:c + 1, head_cols(h)] + kv_ref[c, h]
        st_ref[h] = st

    for c in range(n_chunks):
        rows = chunk_rows(c)
        for h in range(HG_HEADS):
            sl = head_cols(h)
            o = oi_ref[rows, sl] + lax.dot_general(qs_ref[rows, sl], sp_ref[c, h], NT_DIMS,
                                                   preferred_element_type=F32)
            ms = jnp.mean(o * o, axis=1, keepdims=True)
            on = o * lax.rsqrt(ms + RMS_EPS)
            gate = hg_ref[rows, sl].astype(F32)
            r = on * gain[:, sl] * (gate * jax.nn.sigmoid(gate))
            r_ref[rows, sl] = r.astype(r_ref.dtype)


def _hgrn(hq, hf, hi, hg, lb, gain, batch, seq):
    n = hq.shape[0]
    nb = seq // HG_ROWS
    row_spec = pl.BlockSpec((HG_ROWS, 512), lambda b, t: (b * nb + t, 0))
    vec_spec = pl.BlockSpec((1, 512), lambda b, t: (0, 0))
    return pl.pallas_call(
        _hgrn_kernel,
        grid=(batch, nb),
        in_specs=[row_spec, row_spec, row_spec, row_spec, vec_spec, vec_spec],
        out_specs=row_spec,
        out_shape=jax.ShapeDtypeStruct((n, 512), BF16),
        scratch_shapes=[
            pltpu.VMEM((HG_HEADS, HG_DK, HG_DK), F32),
            pltpu.VMEM((HG_ROWS, 512), BF16),
            pltpu.VMEM((HG_ROWS, 512), BF16),
            pltpu.VMEM((HG_ROWS, 512), BF16),
            pltpu.VMEM((HG_ROWS, 512), BF16),
            pltpu.VMEM((HG_ROWS // CHUNK, 512), F32),
            pltpu.VMEM((HG_ROWS // CHUNK, HG_HEADS, CHUNK, CHUNK), BF16),
            pltpu.VMEM((HG_ROWS, 512), F32),
            pltpu.VMEM((HG_ROWS // CHUNK, HG_HEADS, HG_DK, HG_DK), F32),
            pltpu.VMEM((HG_ROWS // CHUNK, HG_HEADS, HG_DK, HG_DK), BF16),
        ],
        compiler_params=pltpu.CompilerParams(
            dimension_semantics=("arbitrary", "arbitrary"), vmem_limit_bytes=VMEM_LIMIT_BYTES),
        name="hgrn",
    )(hq, hf, hi, hg, lb, gain)


def _layer_norm(y, g, b):
    mu = jnp.mean(y, axis=1, keepdims=True)
    d = y - mu
    var = jnp.mean(d * d, axis=1, keepdims=True)
    return d * lax.rsqrt(var + LN_EPS) * g + b


def _tail_kernel(a_ref, r_ref, x_ref, wo_ref, g1_ref, b1_ref, wg_ref, wu_ref, wd_ref,
                 g2_ref, b2_ref, o_ref):
    halves = [slice(j * (TAIL_ROWS // 2), (j + 1) * (TAIL_ROWS // 2)) for j in range(2)]
    mix = [jnp.dot(a_ref[rows, :], wo_ref[:DSA_WIDTH, :], preferred_element_type=F32)
           + jnp.dot(r_ref[rows, :], wo_ref[DSA_WIDTH:, :], preferred_element_type=F32)
           for rows in halves]
    x1, h = [], []
    for j, rows in enumerate(halves):
        x1.append(_layer_norm(ALPHA * x_ref[rows, :] + mix[j], g1_ref[...], b1_ref[...]))
        xb = x1[j].astype(BF16)
        gate = jnp.dot(xb, wg_ref[...], preferred_element_type=F32)
        up = jnp.dot(xb, wu_ref[...], preferred_element_type=F32)
        h.append((gate * jax.nn.sigmoid(gate) * up).astype(BF16))
    for j, rows in enumerate(halves):
        ff = jnp.dot(h[j], wd_ref[...], preferred_element_type=F32)
        o_ref[rows, :] = _layer_norm(ALPHA * x1[j] + ff, g2_ref[...], b2_ref[...])


def _tail(a, r, x2, wo, g1, b1, wg, wu, wd, g2, b2):
    n = x2.shape[0]
    half = pl.BlockSpec((TAIL_ROWS, 512), lambda i: (i, 0))
    full = pl.BlockSpec((TAIL_ROWS, D_MODEL), lambda i: (i, 0))
    vec = pl.BlockSpec((1, D_MODEL), lambda i: (0, 0))
    const = lambda w: pl.BlockSpec(w.shape, lambda i: (0, 0), pipeline_mode=pl.Buffered(1))
    return pl.pallas_call(
        _tail_kernel,
        grid=(n // TAIL_ROWS,),
        in_specs=[half, half, full, const(wo), vec, vec, const(wg), const(wu), const(wd), vec, vec],
        out_specs=full,
        out_shape=jax.ShapeDtypeStruct((n, D_MODEL), F32),
        compiler_params=pltpu.CompilerParams(
            dimension_semantics=("arbitrary",), vmem_limit_bytes=VMEM_LIMIT_BYTES),
        name="mix_ffn",
    )(a, r, x2, wo, g1, b1, wg, wu, wd, g2, b2)


def _proj_weights(w):
    pts = [0]
    for s in SPLIT_SIZES:
        pts.append(pts[-1] + s)
    col = lambda j: w[:, pts[j]:pts[j + 1]]
    wn = jnp.concatenate([col(1), col(6), col(7), col(8), col(9), col(4), col(4)], axis=1)
    wt = jnp.concatenate(
        [col(0), col(3), col(2), jnp.pad(col(5), ((0, 0), (0, IW_ROWS - IDX_HEADS)))], axis=1).T
    return wn.astype(BF16), wt.astype(BF16)


def kernel(x, w_in, w_out, hg_lb_logits, hg_norm_g, ln1_g, ln1_b, w_gate, w_up, w_down, ln2_g, ln2_b):
    batch, seq, _ = x.shape
    n = batch * seq
    lb_all = jnp.cumsum(jax.nn.softmax(hg_lb_logits.astype(F32), axis=0), axis=0)

    x2 = x.reshape(n, D_MODEL)
    for l in range(DEPTH):
        wn, wt = _proj_weights(w_in[l])
        k, ik, hq, hf, hi, hg, qt, iqt, vt, iwt = _project(x2, wn, wt, batch, seq)
        a = _dsa(qt, iqt, iwt, k.reshape(batch, seq, 512), vt,
                 ik.reshape(batch, seq, 2 * IDX_DIM), batch, seq)
        r = _hgrn(hq, hf, hi, hg, lb_all[l].reshape(1, 512), hg_norm_g[l].reshape(1, 512).astype(F32),
                  batch, seq)
        x2 = _tail(a, r, x2, w_out[l].astype(BF16), ln1_g[l].reshape(1, D_MODEL),
                   ln1_b[l].reshape(1, D_MODEL), w_gate[l].astype(BF16), w_up[l].astype(BF16),
                   w_down[l].astype(BF16), ln2_g[l].reshape(1, D_MODEL), ln2_b[l].reshape(1, D_MODEL))
    return x2.reshape(batch, seq, D_MODEL)
```

```python
import numpy as np
import jax
import jax.numpy as jnp
from jax import lax
from jax.experimental import pallas as pl
from jax.experimental.pallas import tpu as pltpu

D_MODEL = 1024
CHUNK = 64
DSA_WIDTH = 512
DSA_HEAD_DIM = 64
DSA_HEADS = 8
IDX_HEADS = 8
IDX_DIM = 64
TOPK_MAX = 256
HG_WIDTH = 512
HG_DK = 128
HG_HEADS = 4
D_FF = 2816
DEPTH = 1
ALPHA = (2.0 * DEPTH) ** 0.25
LN_EPS = 1e-5
RMS_EPS = 1e-6
SPLIT_SIZES = (512, 512, 512, 512, 64, 8, 512, 512, 512, 512)

F32 = jnp.float32
BF16 = jnp.bfloat16

LANES = 128
SUBLANES = 8
VMEM_LIMIT_BYTES = 56 * 1024 * 1024

Q_TILE = 256
KEY_TILE = 256
IW_ROWS = 16
PROJ_ROWS = 512
HG_ROWS = 512
KV_GROUP = 4
TAIL_ROWS = 512
MASK_BIG = 1e30
LOWEST = -3.0e38
BISECT_VALUE_STEPS = 8
BISECT_FIRST_STEPS = 16
BISECT_STEPS_PER_CHECK = 2
SCORE_GROUP = 4
ATTN_GROUP = 4
LOG2E = 1.4426950408889634
V_ROWS = DSA_HEAD_DIM + 16

NT_DIMS = (((1,), (1,)), ((), ()))
TN_DIMS = (((0,), (0,)), ((), ()))


def _proj_kernel(x_ref, wn_ref, wt_ref, k_ref, ik_ref, hq_ref, hf_ref, hi_ref, hg_ref,
                 qt_ref, iqt_ref, vt_ref, iwt_ref):
    xb = x_ref[...].astype(BF16)

    def nn(j, width=512):
        return jnp.dot(xb, wn_ref[:, j * 512:j * 512 + width], preferred_element_type=F32)

    k_ref[...] = nn(0).astype(BF16)
    hq_ref[...] = nn(1)
    hf_ref[...] = nn(2)
    hi_ref[...] = nn(3).astype(BF16)
    hg_ref[...] = nn(4).astype(BF16)
    ik_ref[...] = nn(5, 2 * IDX_DIM).astype(BF16)
    t = lax.dot_general(wt_ref[...], xb, NT_DIMS, preferred_element_type=F32)
    for j in range(PROJ_ROWS // Q_TILE):
        cols = slice(j * Q_TILE, (j + 1) * Q_TILE)
        qt_ref[j] = (t[0:512, cols] * (LOG2E * DSA_HEAD_DIM ** -0.5)).astype(BF16)
        iqt_ref[j] = t[512:1024, cols].astype(BF16)
        for h in range(DSA_HEADS):
            lo = 1024 + h * DSA_HEAD_DIM
            vt_ref[j, h, :DSA_HEAD_DIM, :] = t[lo:lo + DSA_HEAD_DIM, cols].astype(BF16)
            vt_ref[j, h, DSA_HEAD_DIM:, :] = jnp.ones((V_ROWS - DSA_HEAD_DIM, Q_TILE), BF16)
        iwt_ref[j] = t[1536:1536 + IW_ROWS, cols]


def _project(x2, wn, wt, batch, seq):
    n = x2.shape[0]
    nb = seq // PROJ_ROWS
    tiles = PROJ_ROWS // Q_TILE
    n_tiles = seq // Q_TILE
    row_spec = lambda w: pl.BlockSpec((PROJ_ROWS, w), lambda i: (i, 0))
    t_shape = lambda rows, dt: jax.ShapeDtypeStruct((batch, n_tiles, rows, Q_TILE), dt)
    t_spec = lambda rows: pl.BlockSpec((None, tiles, rows, Q_TILE), lambda i: (i // nb, i % nb, 0, 0))
    out_shape = (
        jax.ShapeDtypeStruct((n, 512), BF16),
        jax.ShapeDtypeStruct((n, 2 * IDX_DIM), BF16),
        jax.ShapeDtypeStruct((n, 512), F32),
        jax.ShapeDtypeStruct((n, 512), F32),
        jax.ShapeDtypeStruct((n, 512), BF16),
        jax.ShapeDtypeStruct((n, 512), BF16),
        t_shape(512, BF16),
        t_shape(512, BF16),
        jax.ShapeDtypeStruct((batch, n_tiles, DSA_HEADS, V_ROWS, Q_TILE), BF16),
        t_shape(IW_ROWS, F32),
    )
    out_specs = (
        row_spec(512), row_spec(2 * IDX_DIM), row_spec(512), row_spec(512), row_spec(512),
        row_spec(512), t_spec(512), t_spec(512),
        pl.BlockSpec((None, tiles, DSA_HEADS, V_ROWS, Q_TILE), lambda i: (i // nb, i % nb, 0, 0, 0)),
        t_spec(IW_ROWS),
    )
    return pl.pallas_call(
        _proj_kernel,
        grid=(n // PROJ_ROWS,),
        in_specs=[
            pl.BlockSpec((PROJ_ROWS, D_MODEL), lambda i: (i, 0)),
            pl.BlockSpec(wn.shape, lambda i: (0, 0)),
            pl.BlockSpec(wt.shape, lambda i: (0, 0)),
        ],
        out_specs=out_specs,
        out_shape=out_shape,
        compiler_params=pltpu.CompilerParams(
            dimension_semantics=("arbitrary",), vmem_limit_bytes=VMEM_LIMIT_BYTES),
        name="proj",
    )(x2, wn, wt)


def _fold_rows(x, op):
    return op(x.reshape(x.shape[0] // SUBLANES, SUBLANES, x.shape[1]), axis=0)


def _split3(c):
    out = []
    for _ in range(3):
        t = float(np.asarray(c, dtype=BF16))
        out.append(t)
        c = c - t
    return out


def _dsa_kernel(qt_ref, iqt_ref, iwt_ref, k_ref, vt_ref, ik_ref, o_ref,
                score_ref, qm_ref, iqm_ref, pos_ref, s_ref, acc_ref, m_ref, ext_ref):
    i = pl.program_id(1)
    topk = TOPK_MAX
    idx_scale = (IDX_DIM ** -0.5) * (IDX_HEADS ** -0.5)
    slopes = [2.0 ** (-8.0 * (h + 1) / DSA_HEADS) for h in range(DSA_HEADS)]

    row_i = lax.broadcasted_iota(jnp.int32, (LANES, Q_TILE), 0)
    even_rows = row_i < DSA_HEAD_DIM
    for p in range(DSA_HEADS // 2):
        rows = slice(p * LANES, (p + 1) * LANES)
        qp = qt_ref[rows, :].astype(F32)
        iqp = iqt_ref[rows, :].astype(F32)
        qm_ref[2 * p, :LANES, :] = jnp.where(even_rows, qp, 0.0).astype(BF16)
        qm_ref[2 * p + 1, :LANES, :] = jnp.where(even_rows, 0.0, qp).astype(BF16)
        iqm_ref[2 * p] = jnp.where(even_rows, iqp, 0.0).astype(BF16)
        iqm_ref[2 * p + 1] = jnp.where(even_rows, 0.0, iqp).astype(BF16)
    for h in range(DSA_HEADS):
        c0, c1, c2 = _split3(LOG2E * slopes[h])
        coef = jnp.where(row_i == 0, c0, jnp.where(row_i == 1, c1, jnp.where(row_i == 2, c2, 0.0)))
        qm_ref[h, LANES:, :] = coef.astype(BF16)
    pos_ref[...] = jnp.where(
        lax.broadcasted_iota(jnp.int32, (KEY_TILE, LANES), 1) < 3,
        lax.broadcasted_iota(jnp.int32, (KEY_TILE, LANES), 0), 0).astype(F32).astype(BF16)

    w = iwt_ref[...]
    q_iota = lax.broadcasted_iota(jnp.int32, (1, Q_TILE), 1)
    key_iota = lax.broadcasted_iota(jnp.int32, (KEY_TILE, 1), 0)
    qpos = i * Q_TILE + q_iota
    limit = (qpos // CHUNK + 1) * CHUNK

    def score_tile(kt, carry, diagonal):
        rmax, rmin = carry
        koff = pl.multiple_of(kt * KEY_TILE, KEY_TILE)
        ik = ik_ref[pl.ds(koff, KEY_TILE), :]
        acc = jnp.zeros((KEY_TILE, Q_TILE), F32)
        for h in range(IDX_HEADS):
            s = jnp.dot(ik, iqm_ref[h], preferred_element_type=F32)
            acc = acc + w[h:h + 1, :] * jnp.maximum(s, 0.0)
        acc = acc * idx_scale
        if diagonal:
            adm = (kt * KEY_TILE + key_iota) < limit
            lo_fill = jnp.where(adm, acc, -jnp.inf)
            hi_fill = jnp.where(adm, acc, jnp.inf)
        else:
            lo_fill = hi_fill = acc
        score_ref[kt] = lo_fill
        return (jnp.maximum(rmax, _fold_rows(lo_fill, jnp.max)),
                jnp.minimum(rmin, _fold_rows(hi_fill, jnp.min)))

    def score_group(j, carry):
        for t in range(SCORE_GROUP):
            carry = score_tile(SCORE_GROUP * j + t, carry, False)
        return carry

    ext_ref[0], ext_ref[1] = lax.fori_loop(
        0, i // SCORE_GROUP, score_group,
        (jnp.full((SUBLANES, Q_TILE), -jnp.inf, F32), jnp.full((SUBLANES, Q_TILE), jnp.inf, F32)))

    for rem in range(SCORE_GROUP):
        @pl.when(i % SCORE_GROUP == rem)
        def _():
            carry = (ext_ref[0], ext_ref[1])
            for t in range(rem):
                carry = score_tile(i - rem + t, carry, False)
            ext_ref[0], ext_ref[1] = score_tile(i, carry, True)

    n_tiles = i + 1
    col_max = jnp.max(ext_ref[0], axis=0, keepdims=True)
    col_min = jnp.min(ext_ref[1], axis=0, keepdims=True)

    def to_key(v):
        b = lax.bitcast_convert_type(v, jnp.int32)
        return jnp.where(b < 0, b ^ 0x7FFFFFFF, b)

    def from_key(kv):
        return lax.bitcast_convert_type(jnp.where(kv < 0, kv ^ 0x7FFFFFFF, kv), F32)

    @pl.when(n_tiles < score_ref.shape[0])
    def _():
        score_ref[n_tiles] = jnp.full((KEY_TILE, Q_TILE), -jnp.inf, F32)

    def count(pred, n_pairs=None):
        def body(j, acc):
            for kt in (2 * j, 2 * j + 1):
                acc = acc + _fold_rows(jnp.where(pred(score_ref[kt]), 1.0, 0.0), jnp.sum)
            return acc
        acc = lax.fori_loop(0, (n_tiles + 1) // 2 if n_pairs is None else n_pairs, body,
                            jnp.zeros((SUBLANES, Q_TILE), F32))
        return jnp.sum(acc, axis=0, keepdims=True)

    need = limit > topk
    any_need = jnp.sum(need.astype(jnp.int32)) > 0

    probe_pairs = jnp.where(any_need, (n_tiles + 1) // 2, 0)
    n_nonneg = count(lambda sc: sc >= 0.0, probe_pairs)
    n_pos = count(lambda sc: sc > 0.0, probe_pairs)
    above_zero = n_pos >= topk
    at_zero = jnp.logical_and(n_nonneg >= topk, n_pos < topk)
    key_min, key_max = to_key(col_min), to_key(col_max)
    lo0 = jnp.where(above_zero, 1, jnp.where(at_zero, 0, key_min))
    c_lo0 = jnp.where(above_zero, n_pos, jnp.where(at_zero, n_nonneg, limit.astype(F32)))
    hi0 = jnp.where(above_zero, key_max + 1, jnp.where(at_zero, 1, 0))
    active0 = jnp.logical_and(need, jnp.logical_not(at_zero)).astype(jnp.int32)

    def bis_cond(c):
        return c[4] > 0

    def bis_step(lo, hi, c_lo, active, it):
        key_mid = (lo & hi) + ((lo ^ hi) >> 1) + ((lo ^ hi) & 1)
        lo_f = from_key(lo)
        val_mid = to_key(lo_f + 0.5 * (from_key(hi) - lo_f))
        val_mid = jnp.minimum(jnp.maximum(val_mid, lo + 1), hi)
        mid = jnp.where(it < BISECT_VALUE_STEPS, val_mid, key_mid)
        mid_f = from_key(mid)
        cnt = count(lambda sc: sc >= mid_f)
        ge = jnp.logical_and(active > 0, cnt >= topk)
        lt = jnp.logical_and(active > 0, cnt < topk)
        lo = jnp.where(ge, mid, lo)
        c_lo = jnp.where(ge, cnt, c_lo)
        hi = jnp.where(lt, mid, hi)
        finished = jnp.logical_or(cnt == topk, hi == lo + 1)
        active = jnp.where(finished, 0, active)
        return lo, hi, c_lo, active

    def bis_body(c):
        lo, hi, c_lo, active, _, it = c
        for j in range(BISECT_STEPS_PER_CHECK):
            lo, hi, c_lo, active = bis_step(lo, hi, c_lo, active, it + j)
        return lo, hi, c_lo, active, jnp.sum(active), it + BISECT_STEPS_PER_CHECK

    n_first = jnp.where(jnp.sum(active0) > 0, BISECT_FIRST_STEPS, 0)
    first = lax.fori_loop(
        0, n_first, lambda it, c: bis_step(*c, it), (lo0, hi0, c_lo0, active0))
    lo, _, c_lo, _, _, _ = lax.while_loop(
        bis_cond, bis_body, first + (jnp.sum(first[3]), n_first))
    thr = jnp.where(need, from_key(lo), LOWEST)

    n_tied_rows = jnp.sum(jnp.logical_and(need, c_lo != topk).astype(jnp.int32))

    @pl.when(n_tied_rows > 0)
    def _():
        quota = topk - count(lambda sc: sc > thr)
        r_i = lax.broadcasted_iota(jnp.int32, (KEY_TILE, KEY_TILE), 0)
        c_i = lax.broadcasted_iota(jnp.int32, (KEY_TILE, KEY_TILE), 1)
        strict_lower = jnp.where(r_i > c_i, 1.0, 0.0).astype(BF16)

        def demote_tile(kt, before):
            sc = score_ref[kt]
            tie = sc == thr
            tie_b = jnp.where(tie, 1.0, 0.0).astype(BF16)
            rank = before + jnp.dot(strict_lower, tie_b, preferred_element_type=F32)
            score_ref[kt] = jnp.where(jnp.logical_and(tie, rank >= quota), -jnp.inf, sc)
            return before + jnp.sum(_fold_rows(jnp.where(tie, 1.0, 0.0), jnp.sum),
                                    axis=0, keepdims=True)

        lax.fori_loop(0, n_tiles, demote_tile, jnp.zeros((1, Q_TILE), F32))

    m_ref[...] = jnp.full(m_ref.shape, -jnp.inf, F32)
    acc_ref[...] = jnp.zeros(acc_ref.shape, F32)

    def attn_tiles(tiles):
        pos = pos_ref[...]
        alphas, shifts = [], []
        for slot, (kt, diagonal) in enumerate(tiles):
            koff = pl.multiple_of(kt * KEY_TILE, KEY_TILE)
            mask_bias = jnp.where(score_ref[kt] >= thr, 0.0, -MASK_BIG)
            if diagonal:
                ahead = (2.0 * LOG2E) * jnp.maximum(key_iota - q_iota, 0).astype(F32)
            rel_q = (qpos - kt * KEY_TILE).astype(F32)
            for h in range(DSA_HEADS):
                pair = slice((h // 2) * LANES, (h // 2 + 1) * LANES)
                lhs = jnp.concatenate([k_ref[pl.ds(koff, KEY_TILE), pair], pos], axis=1)
                s = jnp.dot(lhs, qm_ref[h], preferred_element_type=F32) + mask_bias
                if diagonal:
                    s = s - slopes[h] * ahead
                s_ref[slot, h] = s
                offset = (LOG2E * slopes[h]) * rel_q
                m_old = m_ref[h:h + 1, :]
                m_new = jnp.maximum(
                    m_old, jnp.max(_fold_rows(s, jnp.max), axis=0, keepdims=True) - offset)
                m_ref[h:h + 1, :] = m_new
                alphas.append(jnp.exp2(m_old - m_new))
                shifts.append(m_new + offset)
        for slot, (kt, _) in enumerate(tiles):
            for h in range(DSA_HEADS):
                j = slot * DSA_HEADS + h
                p = jnp.exp2(s_ref[slot, h] - shifts[j]).astype(BF16)
                pv = jnp.dot(vt_ref[kt, h], p, preferred_element_type=F32)
                acc_ref[h] = alphas[j] * acc_ref[h] + pv

    def attn_body(j, carry):
        attn_tiles([(ATTN_GROUP * j + t, False) for t in range(ATTN_GROUP)])
        return carry

    lax.fori_loop(0, i // ATTN_GROUP, attn_body, 0)

    for rem in range(ATTN_GROUP):
        @pl.when(i % ATTN_GROUP == rem)
        def _():
            attn_tiles([(i - rem + t, False) for t in range(rem)] + [(i, True)])

    outs = [acc_ref[h, :DSA_HEAD_DIM, :] / acc_ref[h, DSA_HEAD_DIM:DSA_HEAD_DIM + 1, :]
            for h in range(DSA_HEADS)]
    o_ref[...] = jnp.concatenate(outs, axis=0).T.astype(o_ref.dtype)


def _dsa(qt, iqt, iwt, k3, vt, ik3, batch, seq):
    assert Q_TILE == KEY_TILE and Q_TILE % CHUNK == 0
    n = batch * seq
    nqb = seq // Q_TILE
    n_key_tiles = seq // KEY_TILE
    t_spec = lambda rows: pl.BlockSpec((None, None, rows, Q_TILE), lambda b, i: (b, i, 0, 0))
    return pl.pallas_call(
        _dsa_kernel,
        grid=(batch, nqb),
        in_specs=[
            t_spec(512), t_spec(512), t_spec(IW_ROWS),
            pl.BlockSpec((None, seq, 512), lambda b, i: (b, 0, 0)),
            pl.BlockSpec((None, n_key_tiles, DSA_HEADS, V_ROWS, KEY_TILE), lambda b, i: (b, 0, 0, 0, 0)),
            pl.BlockSpec((None, seq, 2 * IDX_DIM), lambda b, i: (b, 0, 0)),
        ],
        out_specs=pl.BlockSpec((Q_TILE, 512), lambda b, i: (b * nqb + i, 0)),
        out_shape=jax.ShapeDtypeStruct((n, 512), BF16),
        scratch_shapes=[
            pltpu.VMEM((n_key_tiles, KEY_TILE, Q_TILE), F32),
            pltpu.VMEM((DSA_HEADS, 2 * LANES, Q_TILE), BF16),
            pltpu.VMEM((IDX_HEADS, LANES, Q_TILE), BF16),
            pltpu.VMEM((KEY_TILE, LANES), BF16),
            pltpu.VMEM((ATTN_GROUP, DSA_HEADS, KEY_TILE, Q_TILE), F32),
            pltpu.VMEM((DSA_HEADS, V_ROWS, Q_TILE), F32),
            pltpu.VMEM((DSA_HEADS, Q_TILE), F32),
            pltpu.VMEM((2, SUBLANES, Q_TILE), F32),
        ],
        compiler_params=pltpu.CompilerParams(
            dimension_semantics=("arbitrary", "arbitrary"), vmem_limit_bytes=VMEM_LIMIT_BYTES),
        name="dsa",
    )(qt, iqt, iwt, k3, vt, ik3)


def _sigmoid(x):
    return 0.5 * jnp.tanh(0.5 * x) + 0.5


def _hgrn_kernel(hq_ref, hf_ref, hi_ref, hg_ref, lb_ref, gain_ref, r_ref,
                 st_ref, qs_ref, qi_ref, ki_ref, ks_ref, dec_ref, a_ref, oi_ref, kv_ref, sp_ref):
    @pl.when(pl.program_id(1) == 0)
    def _():
        st_ref[...] = jnp.zeros(st_ref.shape, F32)

    n_chunks = HG_ROWS // CHUNK
    r_i = lax.broadcasted_iota(jnp.int32, (CHUNK, CHUNK), 0)
    c_i = lax.broadcasted_iota(jnp.int32, (CHUNK, CHUNK), 1)
    tril = r_i >= c_i
    tril_b = jnp.where(tril, 1.0, 0.0).astype(BF16)
    tril3_b = jnp.concatenate([tril_b, tril_b, tril_b], axis=1)
    lb = lb_ref[...]
    gain = gain_ref[...]
    chunk_rows = lambda c: slice(c * CHUNK, (c + 1) * CHUNK)
    head_cols = lambda h: slice(h * HG_DK, (h + 1) * HG_DK)

    for c in range(n_chunks):
        rows = chunk_rows(c)
        f = lb + (1.0 - lb) * _sigmoid(hf_ref[rows, :])
        logf = jnp.log2(f)
        kk = 1.0 - f
        t0 = logf.astype(BF16)
        r1 = logf - t0.astype(F32)
        t1 = r1.astype(BF16)
        t2 = (r1 - t1.astype(F32)).astype(BF16)
        g = jnp.dot(tril3_b, jnp.concatenate([t0, t1, t2], axis=0),
                    preferred_element_type=F32)
        g_last = g[CHUNK - 1:CHUNK, :]
        g_mid = g[CHUNK // 2 - 1:CHUNK // 2, :]
        q_intra = hq_ref[rows, :] * jnp.exp2(g - g_mid)
        k_intra = kk * jnp.exp2(g_mid - g)
        qi_ref[rows, :] = q_intra.astype(BF16)
        ki_ref[rows, :] = k_intra.astype(BF16)
        qs_ref[rows, :] = (q_intra * jnp.exp2(g_mid)).astype(BF16)
        ks_ref[rows, :] = (k_intra * jnp.exp2(g_last - g_mid)).astype(BF16)
        dec_ref[c:c + 1, :] = jnp.exp2(g_last)

    items = [(c, h) for c in range(n_chunks) for h in range(HG_HEADS)]
    for c, h in items:
        a = lax.dot_general(qi_ref[chunk_rows(c), head_cols(h)], ki_ref[chunk_rows(c), head_cols(h)],
                            NT_DIMS, preferred_element_type=F32)
        a_ref[c, h] = jnp.where(tril, a, 0.0).astype(BF16)
    zero_blk = jnp.zeros((CHUNK, HG_DK), BF16)
    for g0 in range(0, n_chunks, KV_GROUP):
        grp_rows = slice(g0 * CHUNK, (g0 + KV_GROUP) * CHUNK)
        for h in range(HG_HEADS):
            sl = head_cols(h)
            keys = jnp.concatenate(
                [jnp.concatenate([ks_ref[chunk_rows(g0 + c), sl] if c == cc else zero_blk
                                  for cc in range(KV_GROUP)], axis=1)
                 for c in range(KV_GROUP)], axis=0)
            kv = lax.dot_general(hi_ref[grp_rows, sl], keys, TN_DIMS,
                                 preferred_element_type=F32)
            for c in range(KV_GROUP):
                kv_ref[g0 + c, h] = kv[:, c * HG_DK:(c + 1) * HG_DK]
    for c, h in items:
        oi_ref[chunk_rows(c), head_cols(h)] = jnp.dot(
            a_ref[c, h], hi_ref[chunk_rows(c), head_cols(h)], preferred_element_type=F32)

    for h in range(HG_HEADS):
        st = st_ref[h]
        for c in range(n_chunks):
            sp_ref[c, h] = st.astype(BF16)
            st = st * dec_ref[c:c + 1, head_cols(h)] + kv_ref[c, h]
        st_ref[h] = st

    for c in range(n_chunks):
        rows = chunk_rows(c)
        for h in range(HG_HEADS):
            sl = head_cols(h)
            o = oi_ref[rows, sl] + lax.dot_general(qs_ref[rows, sl], sp_ref[c, h], NT_DIMS,
                                                   preferred_element_type=F32)
            ms = jnp.mean(o * o, axis=1, keepdims=True)
            on = o * lax.rsqrt(ms + RMS_EPS)
            half_gate = 0.5 * hg_ref[rows, sl].astype(F32)
            silu = half_gate + half_gate * jnp.tanh(half_gate)
            r = on * gain[:, sl] * silu
            r_ref[rows, sl] = r.astype(r_ref.dtype)


def _hgrn(hq, hf, hi, hg, lb, gain, batch, seq):
    n = hq.shape[0]
    nb = seq // HG_ROWS
    row_spec = pl.BlockSpec((HG_ROWS, 512), lambda b, t: (b * nb + t, 0))
    vec_spec = pl.BlockSpec((1, 512), lambda b, t: (0, 0))
    return pl.pallas_call(
        _hgrn_kernel,
        grid=(batch, nb),
        in_specs=[row_spec, row_spec, row_spec, row_spec, vec_spec, vec_spec],
        out_specs=row_spec,
        out_shape=jax.ShapeDtypeStruct((n, 512), BF16),
        scratch_shapes=[
            pltpu.VMEM((HG_HEADS, HG_DK, HG_DK), F32),
            pltpu.VMEM((HG_ROWS, 512), BF16),
            pltpu.VMEM((HG_ROWS, 512), BF16),
            pltpu.VMEM((HG_ROWS, 512), BF16),
            pltpu.VMEM((HG_ROWS, 512), BF16),
            pltpu.VMEM((HG_ROWS // CHUNK, 512), F32),
            pltpu.VMEM((HG_ROWS // CHUNK, HG_HEADS, CHUNK, CHUNK), BF16),
            pltpu.VMEM((HG_ROWS, 512), F32),
            pltpu.VMEM((HG_ROWS // CHUNK, HG_HEADS, HG_DK, HG_DK), F32),
            pltpu.VMEM((HG_ROWS // CHUNK, HG_HEADS, HG_DK, HG_DK), BF16),
        ],
        compiler_params=pltpu.CompilerParams(
            dimension_semantics=("arbitrary", "arbitrary"), vmem_limit_bytes=VMEM_LIMIT_BYTES),
        name="hgrn",
    )(hq, hf, hi, hg, lb, gain)


def _layer_norm(y, g, b):
    mu = jnp.mean(y, axis=1, keepdims=True)
    d = y - mu
    var = jnp.mean(d * d, axis=1, keepdims=True)
    return d * lax.rsqrt(var + LN_EPS) * g + b


def _tail_kernel(a_ref, r_ref, x_ref, wo_ref, g1_ref, b1_ref, wg_ref, wu_ref, wd_ref,
                 g2_ref, b2_ref, o_ref):
    halves = [slice(j * (TAIL_ROWS // 2), (j + 1) * (TAIL_ROWS // 2)) for j in range(2)]
    mix = [jnp.dot(a_ref[rows, :], wo_ref[:DSA_WIDTH, :], preferred_element_type=F32)
           + jnp.dot(r_ref[rows, :], wo_ref[DSA_WIDTH:, :], preferred_element_type=F32)
           for rows in halves]
    x1, h = [], []
    for j, rows in enumerate(halves):
        x1.append(_layer_norm(ALPHA * x_ref[rows, :] + mix[j], g1_ref[...], b1_ref[...]))
        xb = x1[j].astype(BF16)
        gate = jnp.dot(xb, wg_ref[...], preferred_element_type=F32)
        up = jnp.dot(xb, wu_ref[...], preferred_element_type=F32)
        h.append((gate * jax.nn.sigmoid(gate) * up).astype(BF16))
    for j, rows in enumerate(halves):
        ff = jnp.dot(h[j], wd_ref[...], preferred_element_type=F32)
        o_ref[rows, :] = _layer_norm(ALPHA * x1[j] + ff, g2_ref[...], b2_ref[...])


def _tail(a, r, x2, wo, g1, b1, wg, wu, wd, g2, b2):
    n = x2.shape[0]
    half = pl.BlockSpec((TAIL_ROWS, 512), lambda i: (i, 0))
    full = pl.BlockSpec((TAIL_ROWS, D_MODEL), lambda i: (i, 0))
    vec = pl.BlockSpec((1, D_MODEL), lambda i: (0, 0))
    const = lambda w: pl.BlockSpec(w.shape, lambda i: (0, 0), pipeline_mode=pl.Buffered(1))
    return pl.pallas_call(
        _tail_kernel,
        grid=(n // TAIL_ROWS,),
        in_specs=[half, half, full, const(wo), vec, vec, const(wg), const(wu), const(wd), vec, vec],
        out_specs=full,
        out_shape=jax.ShapeDtypeStruct((n, D_MODEL), F32),
        compiler_params=pltpu.CompilerParams(
            dimension_semantics=("arbitrary",), vmem_limit_bytes=VMEM_LIMIT_BYTES),
        name="mix_ffn",
    )(a, r, x2, wo, g1, b1, wg, wu, wd, g2, b2)


def _proj_weights(w):
    pts = [0]
    for s in SPLIT_SIZES:
        pts.append(pts[-1] + s)
    col = lambda j: w[:, pts[j]:pts[j + 1]]
    wn = jnp.concatenate([col(1), col(6), col(7), col(8), col(9), col(4), col(4)], axis=1)
    wt = jnp.concatenate(
        [col(0), col(3), col(2), jnp.pad(col(5), ((0, 0), (0, IW_ROWS - IDX_HEADS)))], axis=1).T
    return wn.astype(BF16), wt.astype(BF16)


def kernel(x, w_in, w_out, hg_lb_logits, hg_norm_g, ln1_g, ln1_b, w_gate, w_up, w_down, ln2_g, ln2_b):
    batch, seq, _ = x.shape
    n = batch * seq
    lb_all = jnp.cumsum(jax.nn.softmax(hg_lb_logits.astype(F32), axis=0), axis=0)

    x2 = x.reshape(n, D_MODEL)
    for l in range(DEPTH):
        wn, wt = _proj_weights(w_in[l])
        k, ik, hq, hf, hi, hg, qt, iqt, vt, iwt = _project(x2, wn, wt, batch, seq)
        a = _dsa(qt, iqt, iwt, k.reshape(batch, seq, 512), vt,
                 ik.reshape(batch, seq, 2 * IDX_DIM), batch, seq)
        r = _hgrn(hq, hf, hi, hg, lb_all[l].reshape(1, 512), hg_norm_g[l].reshape(1, 512).astype(F32),
                  batch, seq)
        x2 = _tail(a, r, x2, w_out[l].astype(BF16), ln1_g[l].reshape(1, D_MODEL),
                   ln1_b[l].reshape(1, D_MODEL), w_gate[l].astype(BF16), w_up[l].astype(BF16),
                   w_down[l].astype(BF16), ln2_g[l].reshape(1, D_MODEL), ln2_b[l].reshape(1, D_MODEL))
    return x2.reshape(batch, seq, D_MODEL)
```

```python
import numpy as np
import jax
import jax.numpy as jnp
from jax import lax
from jax.experimental import pallas as pl
from jax.experimental.pallas import tpu as pltpu

D_MODEL = 1024
CHUNK = 64
DSA_WIDTH = 512
DSA_HEAD_DIM = 64
DSA_HEADS = 8
IDX_HEADS = 8
IDX_DIM = 64
TOPK_MAX = 256
HG_WIDTH = 512
HG_DK = 128
HG_HEADS = 4
D_FF = 2816
DEPTH = 1
ALPHA = (2.0 * DEPTH) ** 0.25
LN_EPS = 1e-5
RMS_EPS = 1e-6
SPLIT_SIZES = (512, 512, 512, 512, 64, 8, 512, 512, 512, 512)

F32 = jnp.float32
BF16 = jnp.bfloat16

LANES = 128
SUBLANES = 8
VMEM_LIMIT_BYTES = 56 * 1024 * 1024

Q_TILE = 256
KEY_TILE = 256
BF16_ROWS = 2 * SUBLANES
IW_ROWS = BF16_ROWS
PROJ_ROWS = 512
HG_ROWS = 512
TAIL_ROWS = 512
MASK_BIG = 1e30
LOWEST = -3.0e38
BISECT_VALUE_STEPS = 14
BISECT_COARSE_STEPS = 7
BISECT_FIRST_STEPS = 16
BISECT_STEPS_PER_CHECK = 2
SCORE_GROUP = 4
ATTN_GROUP = 4
LOG2E = 1.4426950408889634
V_ROWS = DSA_HEAD_DIM + BF16_ROWS

NT_DIMS = (((1,), (1,)), ((), ()))
TN_DIMS = (((0,), (0,)), ((), ()))


def _proj_kernel(x_ref, wn_ref, wt_ref, k_ref, ik_ref, hq_ref, hf_ref, hi_ref, hg_ref,
                 qt_ref, iqt_ref, vt_ref, iwt_ref):
    xb = x_ref[...].astype(BF16)

    wd = DSA_WIDTH

    def nn(j, width=wd):
        return jnp.dot(xb, wn_ref[:, j * wd:j * wd + width], preferred_element_type=F32)

    k_ref[...] = nn(0).astype(BF16)
    hq_ref[...] = nn(1)
    hf_ref[...] = nn(2)
    hi_ref[...] = nn(3).astype(BF16)
    hg_ref[...] = nn(4).astype(BF16)
    ik_ref[...] = nn(5, 2 * IDX_DIM).astype(BF16)
    t = lax.dot_general(wt_ref[...], xb, NT_DIMS, preferred_element_type=F32)
    for j in range(PROJ_ROWS // Q_TILE):
        cols = slice(j * Q_TILE, (j + 1) * Q_TILE)
        qt_ref[j] = (t[0:wd, cols] * (LOG2E * DSA_HEAD_DIM ** -0.5)).astype(BF16)
        iqt_ref[j] = t[wd:2 * wd, cols].astype(BF16)
        for h in range(DSA_HEADS):
            lo = 2 * wd + h * DSA_HEAD_DIM
            vt_ref[j, h, :DSA_HEAD_DIM, :] = t[lo:lo + DSA_HEAD_DIM, cols].astype(BF16)
            vt_ref[j, h, DSA_HEAD_DIM:, :] = jnp.ones((V_ROWS - DSA_HEAD_DIM, Q_TILE), BF16)
        iwt_ref[j] = t[3 * wd:3 * wd + IW_ROWS, cols]


def _project(x2, wn, wt, batch, seq):
    assert DSA_WIDTH == HG_WIDTH == IDX_HEADS * IDX_DIM
    n = x2.shape[0]
    nb = seq // PROJ_ROWS
    tiles = PROJ_ROWS // Q_TILE
    n_tiles = seq // Q_TILE
    row_spec = lambda w: pl.BlockSpec((PROJ_ROWS, w), lambda i: (i, 0))
    t_shape = lambda rows, dt: jax.ShapeDtypeStruct((batch, n_tiles, rows, Q_TILE), dt)
    t_spec = lambda rows: pl.BlockSpec((None, tiles, rows, Q_TILE), lambda i: (i // nb, i % nb, 0, 0))
    out_shape = (
        jax.ShapeDtypeStruct((n, 512), BF16),
        jax.ShapeDtypeStruct((n, 2 * IDX_DIM), BF16),
        jax.ShapeDtypeStruct((n, 512), F32),
        jax.ShapeDtypeStruct((n, 512), F32),
        jax.ShapeDtypeStruct((n, 512), BF16),
        jax.ShapeDtypeStruct((n, 512), BF16),
        t_shape(512, BF16),
        t_shape(512, BF16),
        jax.ShapeDtypeStruct((batch, n_tiles, DSA_HEADS, V_ROWS, Q_TILE), BF16),
        t_shape(IW_ROWS, F32),
    )
    out_specs = (
        row_spec(512), row_spec(2 * IDX_DIM), row_spec(512), row_spec(512), row_spec(512),
        row_spec(512), t_spec(512), t_spec(512),
        pl.BlockSpec((None, tiles, DSA_HEADS, V_ROWS, Q_TILE), lambda i: (i // nb, i % nb, 0, 0, 0)),
        t_spec(IW_ROWS),
    )
    return pl.pallas_call(
        _proj_kernel,
        grid=(n // PROJ_ROWS,),
        in_specs=[
            pl.BlockSpec((PROJ_ROWS, D_MODEL), lambda i: (i, 0)),
            pl.BlockSpec(wn.shape, lambda i: (0, 0)),
            pl.BlockSpec(wt.shape, lambda i: (0, 0)),
        ],
        out_specs=out_specs,
        out_shape=out_shape,
        compiler_params=pltpu.CompilerParams(
            dimension_semantics=("arbitrary",), vmem_limit_bytes=VMEM_LIMIT_BYTES),
        name="proj",
    )(x2, wn, wt)


def _fold_rows(x, op):
    return op(x.reshape(x.shape[0] // SUBLANES, SUBLANES, x.shape[1]), axis=0)


def _split3(c):
    out = []
    for _ in range(3):
        t = float(np.asarray(c, dtype=BF16))
        out.append(t)
        c = c - t
    return out


def _dsa_kernel(qt_ref, iqt_ref, iwt_ref, k_ref, vt_ref, ik_ref, o_ref,
                score_ref, coarse_ref, qm_ref, iqm_ref, pos_ref, s_ref, acc_ref, m_ref, ext_ref):
    i = pl.program_id(1)
    topk = TOPK_MAX
    idx_scale = (IDX_DIM ** -0.5) * (IDX_HEADS ** -0.5)
    slopes = [2.0 ** (-8.0 * (h + 1) / DSA_HEADS) for h in range(DSA_HEADS)]

    row_i = lax.broadcasted_iota(jnp.int32, (LANES, Q_TILE), 0)
    even_rows = row_i < DSA_HEAD_DIM
    for p in range(DSA_HEADS // 2):
        rows = slice(p * LANES, (p + 1) * LANES)
        qp = qt_ref[rows, :].astype(F32)
        iqp = iqt_ref[rows, :].astype(F32)
        qm_ref[2 * p, :LANES, :] = jnp.where(even_rows, qp, 0.0).astype(BF16)
        qm_ref[2 * p + 1, :LANES, :] = jnp.where(even_rows, 0.0, qp).astype(BF16)
        iqm_ref[2 * p] = jnp.where(even_rows, iqp, 0.0).astype(BF16)
        iqm_ref[2 * p + 1] = jnp.where(even_rows, 0.0, iqp).astype(BF16)
    for h in range(DSA_HEADS):
        c0, c1, c2 = _split3(LOG2E * slopes[h])
        coef = jnp.where(row_i == 0, c0, jnp.where(row_i == 1, c1, jnp.where(row_i == 2, c2, 0.0)))
        qm_ref[h, LANES:, :] = coef.astype(BF16)
    pos_ref[...] = jnp.where(
        lax.broadcasted_iota(jnp.int32, (KEY_TILE, LANES), 1) < 3,
        lax.broadcasted_iota(jnp.int32, (KEY_TILE, LANES), 0), 0).astype(F32).astype(BF16)

    w = iwt_ref[...]
    q_iota = lax.broadcasted_iota(jnp.int32, (1, Q_TILE), 1)
    key_iota = lax.broadcasted_iota(jnp.int32, (KEY_TILE, 1), 0)
    qpos = i * Q_TILE + q_iota
    limit = (qpos // CHUNK + 1) * CHUNK

    def score_tile(kt, carry, diagonal):
        rmax, rmin = carry
        koff = pl.multiple_of(kt * KEY_TILE, KEY_TILE)
        ik = ik_ref[pl.ds(koff, KEY_TILE), :]
        acc = jnp.zeros((KEY_TILE, Q_TILE), F32)
        for h in range(IDX_HEADS):
            s = jnp.dot(ik, iqm_ref[h], preferred_element_type=F32)
            acc = acc + w[h:h + 1, :] * jnp.maximum(s, 0.0)
        acc = acc * idx_scale
        if diagonal:
            adm = (kt * KEY_TILE + key_iota) < limit
            lo_fill = jnp.where(adm, acc, -jnp.inf)
            hi_fill = jnp.where(adm, acc, jnp.inf)
        else:
            lo_fill = hi_fill = acc
        score_ref[kt] = lo_fill
        coarse_ref[kt] = lo_fill.astype(BF16)
        return (jnp.maximum(rmax, _fold_rows(lo_fill, jnp.max)),
                jnp.minimum(rmin, _fold_rows(hi_fill, jnp.min)))

    def score_group(j, carry):
        for t in range(SCORE_GROUP):
            carry = score_tile(SCORE_GROUP * j + t, carry, False)
        return carry

    ext_ref[0], ext_ref[1] = lax.fori_loop(
        0, i // SCORE_GROUP, score_group,
        (jnp.full((SUBLANES, Q_TILE), -jnp.inf, F32), jnp.full((SUBLANES, Q_TILE), jnp.inf, F32)))

    for rem in range(SCORE_GROUP):
        @pl.when(i % SCORE_GROUP == rem)
        def _():
            carry = (ext_ref[0], ext_ref[1])
            for t in range(rem):
                carry = score_tile(i - rem + t, carry, False)
            ext_ref[0], ext_ref[1] = score_tile(i, carry, True)

    n_tiles = i + 1
    col_max = jnp.max(ext_ref[0], axis=0, keepdims=True)
    col_min = jnp.min(ext_ref[1], axis=0, keepdims=True)

    def to_key(v):
        b = lax.bitcast_convert_type(v, jnp.int32)
        return jnp.where(b < 0, b ^ 0x7FFFFFFF, b)

    def from_key(kv):
        return lax.bitcast_convert_type(jnp.where(kv < 0, kv ^ 0x7FFFFFFF, kv), F32)

    @pl.when(n_tiles < score_ref.shape[0])
    def _():
        score_ref[n_tiles] = jnp.full((KEY_TILE, Q_TILE), -jnp.inf, F32)
        coarse_ref[n_tiles] = jnp.full((KEY_TILE, Q_TILE), -jnp.inf, BF16)

    def count(pred, n_pairs=None):
        def body(j, acc):
            for kt in (2 * j, 2 * j + 1):
                acc = acc + _fold_rows(jnp.where(pred(score_ref[kt]), 1.0, 0.0), jnp.sum)
            return acc
        acc = lax.fori_loop(0, (n_tiles + 1) // 2 if n_pairs is None else n_pairs, body,
                            jnp.zeros((SUBLANES, Q_TILE), F32))
        return jnp.sum(acc, axis=0, keepdims=True)

    need = limit > topk
    any_need = jnp.sum(need.astype(jnp.int32)) > 0

    probe_pairs = jnp.where(any_need, (n_tiles + 1) // 2, 0)
    n_nonneg = count(lambda sc: sc >= 0.0, probe_pairs)
    n_pos = count(lambda sc: sc > 0.0, probe_pairs)
    above_zero = n_pos >= topk
    at_zero = jnp.logical_and(n_nonneg >= topk, n_pos < topk)
    key_min, key_max = to_key(col_min), to_key(col_max)
    lo0 = jnp.where(above_zero, 1, jnp.where(at_zero, 0, key_min))
    c_lo0 = jnp.where(above_zero, n_pos, jnp.where(at_zero, n_nonneg, limit.astype(F32)))
    hi0 = jnp.where(above_zero, key_max + 1, jnp.where(at_zero, 1, 0))
    active0 = jnp.logical_and(need, jnp.logical_not(at_zero)).astype(jnp.int32)

    def bis_cond(c):
        return c[4] > 0

    def bis_step(lo, hi, c_lo, active, it):
        key_mid = (lo & hi) + ((lo ^ hi) >> 1) + ((lo ^ hi) & 1)
        lo_f = from_key(lo)
        val_mid = to_key(lo_f + 0.5 * (from_key(hi) - lo_f))
        val_mid = jnp.minimum(jnp.maximum(val_mid, lo + 1), hi)
        mid = jnp.where(it < BISECT_VALUE_STEPS, val_mid, key_mid)
        mid_f = from_key(mid)
        cnt = count(lambda sc: sc >= mid_f)
        ge = jnp.logical_and(active > 0, cnt >= topk)
        lt = jnp.logical_and(active > 0, cnt < topk)
        lo = jnp.where(ge, mid, lo)
        c_lo = jnp.where(ge, cnt, c_lo)
        hi = jnp.where(lt, mid, hi)
        finished = jnp.logical_or(cnt == topk, hi == lo + 1)
        active = jnp.where(finished, 0, active)
        return lo, hi, c_lo, active

    def bis_body(c):
        lo, hi, c_lo, active, _, it = c
        for j in range(BISECT_STEPS_PER_CHECK):
            lo, hi, c_lo, active = bis_step(lo, hi, c_lo, active, it + j)
        return lo, hi, c_lo, active, jnp.sum(active), it + BISECT_STEPS_PER_CHECK

    bf16_step = 1 << 16
    one_b, zero_b = jnp.ones((), BF16), jnp.zeros((), BF16)
    rows_b = BF16_ROWS

    def count_coarse(t_b):
        def body(j, acc):
            for kt in (2 * j, 2 * j + 1):
                hit = jnp.where(coarse_ref[kt] >= t_b, one_b, zero_b)
                part = hit[0:rows_b, :]
                for r in range(1, KEY_TILE // rows_b):
                    part = part + hit[r * rows_b:(r + 1) * rows_b, :]
                acc = acc + part.astype(F32)
            return acc
        acc = lax.fori_loop(0, (n_tiles + 1) // 2, body, jnp.zeros((rows_b, Q_TILE), F32))
        return jnp.sum(acc, axis=0, keepdims=True)

    def coarse_step(it, c):
        lo, hi, c_lo, active = c
        lo_f = from_key(lo)
        t_b = (lo_f + 0.5 * (from_key(hi) - lo_f)).astype(BF16)
        key_t = to_key(t_b.astype(F32))
        usable = jnp.logical_and(
            active > 0, jnp.logical_and(key_t - bf16_step > lo, key_t + bf16_step < hi))
        cnt = count_coarse(t_b)
        ge = jnp.logical_and(usable, cnt >= topk)
        lt = jnp.logical_and(usable, cnt < topk)
        lo = jnp.where(ge, key_t - bf16_step, lo)
        c_lo = jnp.where(ge, topk + 1.0, c_lo)
        hi = jnp.where(lt, key_t + bf16_step, hi)
        return lo, hi, c_lo, active

    any_active = jnp.sum(active0) > 0
    n_coarse = jnp.where(any_active, BISECT_COARSE_STEPS, 0)
    n_first = jnp.where(any_active, BISECT_FIRST_STEPS, 0)
    first = lax.fori_loop(0, n_coarse, coarse_step, (lo0, hi0, c_lo0, active0))
    first = lax.fori_loop(n_coarse, n_first, lambda it, c: bis_step(*c, it), first)
    lo, _, c_lo, _, _, _ = lax.while_loop(
        bis_cond, bis_body, first + (jnp.sum(first[3]), n_first))
    thr = jnp.where(need, from_key(lo), LOWEST)

    n_tied_rows = jnp.sum(jnp.logical_and(need, c_lo != topk).astype(jnp.int32))

    @pl.when(n_tied_rows > 0)
    def _():
        quota = topk - count(lambda sc: sc > thr)
        r_i = lax.broadcasted_iota(jnp.int32, (KEY_TILE, KEY_TILE), 0)
        c_i = lax.broadcasted_iota(jnp.int32, (KEY_TILE, KEY_TILE), 1)
        strict_lower = jnp.where(r_i > c_i, 1.0, 0.0).astype(BF16)

        def demote_tile(kt, before):
            sc = score_ref[kt]
            tie = sc == thr
            tie_b = jnp.where(tie, 1.0, 0.0).astype(BF16)
            rank = before + jnp.dot(strict_lower, tie_b, preferred_element_type=F32)
            score_ref[kt] = jnp.where(jnp.logical_and(tie, rank >= quota), -jnp.inf, sc)
            return before + jnp.sum(_fold_rows(jnp.where(tie, 1.0, 0.0), jnp.sum),
                                    axis=0, keepdims=True)

        lax.fori_loop(0, n_tiles, demote_tile, jnp.zeros((1, Q_TILE), F32))

    m_ref[...] = jnp.full(m_ref.shape, -jnp.inf, F32)
    acc_ref[...] = jnp.zeros(acc_ref.shape, F32)

    def attn_tiles(tiles):
        pos = pos_ref[...]
        alphas, shifts = [], []
        for slot, (kt, diagonal) in enumerate(tiles):
            koff = pl.multiple_of(kt * KEY_TILE, KEY_TILE)
            mask_bias = jnp.where(score_ref[kt] >= thr, 0.0, -MASK_BIG)
            if diagonal:
                ahead = (2.0 * LOG2E) * jnp.maximum(key_iota - q_iota, 0).astype(F32)
            rel_q = (qpos - kt * KEY_TILE).astype(F32)
            for h in range(DSA_HEADS):
                pair = slice((h // 2) * LANES, (h // 2 + 1) * LANES)
                lhs = jnp.concatenate([k_ref[pl.ds(koff, KEY_TILE), pair], pos], axis=1)
                s = jnp.dot(lhs, qm_ref[h], preferred_element_type=F32) + mask_bias
                if diagonal:
                    s = s - slopes[h] * ahead
                s_ref[slot, h] = s
                offset = (LOG2E * slopes[h]) * rel_q
                m_old = m_ref[h:h + 1, :]
                m_new = jnp.maximum(
                    m_old, jnp.max(_fold_rows(s, jnp.max), axis=0, keepdims=True) - offset)
                m_ref[h:h + 1, :] = m_new
                alphas.append(jnp.exp2(m_old - m_new))
                shifts.append(m_new + offset)
        for slot, (kt, _) in enumerate(tiles):
            for h in range(DSA_HEADS):
                j = slot * DSA_HEADS + h
                p = jnp.exp2(s_ref[slot, h] - shifts[j]).astype(BF16)
                pv = jnp.dot(vt_ref[kt, h], p, preferred_element_type=F32)
                acc_ref[h] = alphas[j] * acc_ref[h] + pv

    def attn_body(j, carry):
        attn_tiles([(ATTN_GROUP * j + t, False) for t in range(ATTN_GROUP)])
        return carry

    lax.fori_loop(0, i // ATTN_GROUP, attn_body, 0)

    for rem in range(ATTN_GROUP):
        @pl.when(i % ATTN_GROUP == rem)
        def _():
            attn_tiles([(i - rem + t, False) for t in range(rem)] + [(i, True)])

    outs = [acc_ref[h, :DSA_HEAD_DIM, :] / acc_ref[h, DSA_HEAD_DIM:DSA_HEAD_DIM + 1, :]
            for h in range(DSA_HEADS)]
    o_ref[...] = jnp.concatenate(outs, axis=0).T.astype(o_ref.dtype)


def _dsa(qt, iqt, iwt, k3, vt, ik3, batch, seq):
    assert Q_TILE == KEY_TILE and Q_TILE % CHUNK == 0
    n = batch * seq
    nqb = seq // Q_TILE
    n_key_tiles = seq // KEY_TILE
    t_spec = lambda rows: pl.BlockSpec((None, None, rows, Q_TILE), lambda b, i: (b, i, 0, 0))
    return pl.pallas_call(
        _dsa_kernel,
        grid=(batch, nqb),
        in_specs=[
            t_spec(512), t_spec(512), t_spec(IW_ROWS),
            pl.BlockSpec((None, seq, 512), lambda b, i: (b, 0, 0)),
            pl.BlockSpec((None, n_key_tiles, DSA_HEADS, V_ROWS, KEY_TILE), lambda b, i: (b, 0, 0, 0, 0)),
            pl.BlockSpec((None, seq, 2 * IDX_DIM), lambda b, i: (b, 0, 0)),
        ],
        out_specs=pl.BlockSpec((Q_TILE, 512), lambda b, i: (b * nqb + i, 0)),
        out_shape=jax.ShapeDtypeStruct((n, 512), BF16),
        scratch_shapes=[
            pltpu.VMEM((n_key_tiles, KEY_TILE, Q_TILE), F32),
            pltpu.VMEM((n_key_tiles, KEY_TILE, Q_TILE), BF16),
            pltpu.VMEM((DSA_HEADS, 2 * LANES, Q_TILE), BF16),
            pltpu.VMEM((IDX_HEADS, LANES, Q_TILE), BF16),
            pltpu.VMEM((KEY_TILE, LANES), BF16),
            pltpu.VMEM((ATTN_GROUP, DSA_HEADS, KEY_TILE, Q_TILE), F32),
            pltpu.VMEM((DSA_HEADS, V_ROWS, Q_TILE), F32),
            pltpu.VMEM((DSA_HEADS, Q_TILE), F32),
            pltpu.VMEM((2, SUBLANES, Q_TILE), F32),
        ],
        compiler_params=pltpu.CompilerParams(
            dimension_semantics=("arbitrary", "arbitrary"), vmem_limit_bytes=VMEM_LIMIT_BYTES),
        name="dsa",
    )(qt, iqt, iwt, k3, vt, ik3)


def _sigmoid(x):
    return 0.5 * jnp.tanh(0.5 * x) + 0.5


def _hgrn_kernel(hq_ref, hf_ref, hi_ref, hg_ref, lb_ref, gain_ref, r_ref,
                 st_ref, qs_ref, qi_ref, ki_ref, ks_ref, dec_ref, a_ref, oi_ref, kv_ref, sp_ref):
    @pl.when(pl.program_id(1) == 0)
    def _():
        st_ref[...] = jnp.zeros(st_ref.shape, F32)

    n_chunks = HG_ROWS // CHUNK
    r_i = lax.broadcasted_iota(jnp.int32, (CHUNK, CHUNK), 0)
    c_i = lax.broadcasted_iota(jnp.int32, (CHUNK, CHUNK), 1)
    tril = r_i >= c_i
    tril_b = jnp.where(tril, 1.0, 0.0).astype(BF16)
    tril3_b = jnp.concatenate([tril_b, tril_b, tril_b], axis=1)
    lb = lb_ref[...]
    gain = gain_ref[...]
    chunk_rows = lambda c: slice(c * CHUNK, (c + 1) * CHUNK)
    head_cols = lambda h: slice(h * HG_DK, (h + 1) * HG_DK)

    for c in range(n_chunks):
        rows = chunk_rows(c)
        f = lb + (1.0 - lb) * _sigmoid(hf_ref[rows, :])
        logf = jnp.log2(f)
        kk = 1.0 - f
        t0 = logf.astype(BF16)
        r1 = logf - t0.astype(F32)
        t1 = r1.astype(BF16)
        t2 = (r1 - t1.astype(F32)).astype(BF16)
        g = jnp.dot(tril3_b, jnp.concatenate([t0, t1, t2], axis=0),
                    preferred_element_type=F32)
        g_last = g[CHUNK - 1:CHUNK, :]
        g_mid = g[CHUNK // 2 - 1:CHUNK // 2, :]
        q_intra = hq_ref[rows, :] * jnp.exp2(g - g_mid)
        k_intra = kk * jnp.exp2(g_mid - g)
        qi_ref[rows, :] = q_intra.astype(BF16)
        ki_ref[rows, :] = k_intra.astype(BF16)
        qs_ref[rows, :] = (q_intra * jnp.exp2(g_mid)).astype(BF16)
        ks_ref[rows, :] = (k_intra * jnp.exp2(g_last - g_mid)).astype(BF16)
        dec_ref[c:c + 1, :] = jnp.exp2(g_last)

    items = [(c, h) for c in range(n_chunks) for h in range(HG_HEADS)]
    for c, h in items:
        a = lax.dot_general(qi_ref[chunk_rows(c), head_cols(h)], ki_ref[chunk_rows(c), head_cols(h)],
                            NT_DIMS, preferred_element_type=F32)
        a_ref[c, h] = jnp.where(tril, a, 0.0).astype(BF16)
    for c, h in items:
        kv_ref[c, h] = lax.dot_general(hi_ref[chunk_rows(c), head_cols(h)],
                                       ks_ref[chunk_rows(c), head_cols(h)], TN_DIMS,
                                       preferred_element_type=F32)
    for c, h in items:
        oi_ref[chunk_rows(c), head_cols(h)] = jnp.dot(
            a_ref[c, h], hi_ref[chunk_rows(c), head_cols(h)], preferred_element_type=F32)

    for h in range(HG_HEADS):
        st = st_ref[h]
        for c in range(n_chunks):
            sp_ref[c, h] = st.astype(BF16)
            st = st * dec_ref[c:c + 1, head_cols(h)] + kv_ref[c, h]
        st_ref[h] = st

    for c in range(n_chunks):
        rows = chunk_rows(c)
        for h in range(HG_HEADS):
            sl = head_cols(h)
            o = oi_ref[rows, sl] + lax.dot_general(qs_ref[rows, sl], sp_ref[c, h], NT_DIMS,
                                                   preferred_element_type=F32)
            ms = jnp.mean(o * o, axis=1, keepdims=True)
            on = o * lax.rsqrt(ms + RMS_EPS)
            half_gate = 0.5 * hg_ref[rows, sl].astype(F32)
            silu = half_gate + half_gate * jnp.tanh(half_gate)
            r = on * gain[:, sl] * silu
            r_ref[rows, sl] = r.astype(r_ref.dtype)


def _hgrn(hq, hf, hi, hg, lb, gain, batch, seq):
    n = hq.shape[0]
    nb = seq // HG_ROWS
    row_spec = pl.BlockSpec((HG_ROWS, 512), lambda b, t: (b * nb + t, 0))
    vec_spec = pl.BlockSpec((1, 512), lambda b, t: (0, 0))
    return pl.pallas_call(
        _hgrn_kernel,
        grid=(batch, nb),
        in_specs=[row_spec, row_spec, row_spec, row_spec, vec_spec, vec_spec],
        out_specs=row_spec,
        out_shape=jax.ShapeDtypeStruct((n, 512), BF16),
        scratch_shapes=[
            pltpu.VMEM((HG_HEADS, HG_DK, HG_DK), F32),
            pltpu.VMEM((HG_ROWS, 512), BF16),
            pltpu.VMEM((HG_ROWS, 512), BF16),
            pltpu.VMEM((HG_ROWS, 512), BF16),
            pltpu.VMEM((HG_ROWS, 512), BF16),
            pltpu.VMEM((HG_ROWS // CHUNK, 512), F32),
            pltpu.VMEM((HG_ROWS // CHUNK, HG_HEADS, CHUNK, CHUNK), BF16),
            pltpu.VMEM((HG_ROWS, 512), F32),
            pltpu.VMEM((HG_ROWS // CHUNK, HG_HEADS, HG_DK, HG_DK), F32),
            pltpu.VMEM((HG_ROWS // CHUNK, HG_HEADS, HG_DK, HG_DK), BF16),
        ],
        compiler_params=pltpu.CompilerParams(
            dimension_semantics=("arbitrary", "arbitrary"), vmem_limit_bytes=VMEM_LIMIT_BYTES),
        name="hgrn",
    )(hq, hf, hi, hg, lb, gain)


def _layer_norm(y, g, b):
    mu = jnp.mean(y, axis=1, keepdims=True)
    d = y - mu
    var = jnp.mean(d * d, axis=1, keepdims=True)
    return d * lax.rsqrt(var + LN_EPS) * g + b


def _tail_kernel(a_ref, r_ref, x_ref, wo_ref, g1_ref, b1_ref, wg_ref, wu_ref, wd_ref,
                 g2_ref, b2_ref, o_ref):
    halves = [slice(j * (TAIL_ROWS // 2), (j + 1) * (TAIL_ROWS // 2)) for j in range(2)]
    mix = [jnp.dot(a_ref[rows, :], wo_ref[:DSA_WIDTH, :], preferred_element_type=F32)
           + jnp.dot(r_ref[rows, :], wo_ref[DSA_WIDTH:, :], preferred_element_type=F32)
           for rows in halves]
    x1, h = [], []
    for j, rows in enumerate(halves):
        x1.append(_layer_norm(ALPHA * x_ref[rows, :] + mix[j], g1_ref[...], b1_ref[...]))
        xb = x1[j].astype(BF16)
        gate = jnp.dot(xb, wg_ref[...], preferred_element_type=F32)
        up = jnp.dot(xb, wu_ref[...], preferred_element_type=F32)
        h.append((gate * jax.nn.sigmoid(gate) * up).astype(BF16))
    for j, rows in enumerate(halves):
        ff = jnp.dot(h[j], wd_ref[...], preferred_element_type=F32)
        o_ref[rows, :] = _layer_norm(ALPHA * x1[j] + ff, g2_ref[...], b2_ref[...])


def _tail(a, r, x2, wo, g1, b1, wg, wu, wd, g2, b2):
    n = x2.shape[0]
    half = pl.BlockSpec((TAIL_ROWS, 512), lambda i: (i, 0))
    full = pl.BlockSpec((TAIL_ROWS, D_MODEL), lambda i: (i, 0))
    vec = pl.BlockSpec((1, D_MODEL), lambda i: (0, 0))
    const = lambda w: pl.BlockSpec(w.shape, lambda i: (0, 0), pipeline_mode=pl.Buffered(1))
    return pl.pallas_call(
        _tail_kernel,
        grid=(n // TAIL_ROWS,),
        in_specs=[half, half, full, const(wo), vec, vec, const(wg), const(wu), const(wd), vec, vec],
        out_specs=full,
        out_shape=jax.ShapeDtypeStruct((n, D_MODEL), F32),
        compiler_params=pltpu.CompilerParams(
            dimension_semantics=("arbitrary",), vmem_limit_bytes=VMEM_LIMIT_BYTES),
        name="mix_ffn",
    )(a, r, x2, wo, g1, b1, wg, wu, wd, g2, b2)


def _proj_weights(w):
    pts = [0]
    for s in SPLIT_SIZES:
        pts.append(pts[-1] + s)
    col = lambda j: w[:, pts[j]:pts[j + 1]]
    wn = jnp.concatenate([col(1), col(6), col(7), col(8), col(9), col(4), col(4)], axis=1)
    wt = jnp.concatenate(
        [col(0), col(3), col(2), jnp.pad(col(5), ((0, 0), (0, IW_ROWS - IDX_HEADS)))], axis=1).T
    return wn.astype(BF16), wt.astype(BF16)


def kernel(x, w_in, w_out, hg_lb_logits, hg_norm_g, ln1_g, ln1_b, w_gate, w_up, w_down, ln2_g, ln2_b):
    batch, seq, _ = x.shape
    n = batch * seq
    lb_all = jnp.cumsum(jax.nn.softmax(hg_lb_logits.astype(F32), axis=0), axis=0)

    x2 = x.reshape(n, D_MODEL)
    for l in range(DEPTH):
        wn, wt = _proj_weights(w_in[l])
        k, ik, hq, hf, hi, hg, qt, iqt, vt, iwt = _project(x2, wn, wt, batch, seq)
        a = _dsa(qt, iqt, iwt, k.reshape(batch, seq, 512), vt,
                 ik.reshape(batch, seq, 2 * IDX_DIM), batch, seq)
        r = _hgrn(hq, hf, hi, hg, lb_all[l].reshape(1, 512), hg_norm_g[l].reshape(1, 512).astype(F32),
                  batch, seq)
        x2 = _tail(a, r, x2, w_out[l].astype(BF16), ln1_g[l].reshape(1, D_MODEL),
                   ln1_b[l].reshape(1, D_MODEL), w_gate[l].astype(BF16), w_up[l].astype(BF16),
                   w_down[l].astype(BF16), ln2_g[l].reshape(1, D_MODEL), ln2_b[l].reshape(1, D_MODEL))
    return x2.reshape(batch, seq, D_MODEL)
```

```python
import functools

import numpy as np
import jax
import jax.numpy as jnp
from jax import lax
from jax.experimental import pallas as pl
from jax.experimental.pallas import tpu as pltpu

D_MODEL = 1024
CHUNK = 64
DSA_WIDTH = 512
DSA_HEAD_DIM = 64
DSA_HEADS = 8
IDX_HEADS = 8
IDX_DIM = 64
TOPK_MAX = 256
HG_WIDTH = 512
HG_DK = 128
HG_HEADS = 4
D_FF = 2816
DEPTH = 1
ALPHA = (2.0 * DEPTH) ** 0.25
LN_EPS = 1e-5
RMS_EPS = 1e-6
SPLIT_SIZES = (512, 512, 512, 512, 64, 8, 512, 512, 512, 512)

F32 = jnp.float32
BF16 = jnp.bfloat16

LANES = 128
SUBLANES = 8
VMEM_LIMIT_BYTES = 56 * 1024 * 1024

Q_TILE = 256
KEY_TILE = 256
BF16_ROWS = 2 * SUBLANES
IW_ROWS = BF16_ROWS
PROJ_ROWS = 512
HG_ROWS = 512
TAIL_ROWS = 512
MASK_BIG = 1e30
LOWEST = -3.0e38
BISECT_VALUE_STEPS = 14
BISECT_COARSE_STEPS = 7
BISECT_FIRST_STEPS = 16
BISECT_STEPS_PER_CHECK = 2
SCORE_GROUP = 4
ATTN_GROUP = 4
LOG2E = 1.4426950408889634
V_ROWS = DSA_HEAD_DIM + BF16_ROWS

NT_DIMS = (((1,), (1,)), ((), ()))
TN_DIMS = (((0,), (0,)), ((), ()))


def _proj_kernel(blocks_per_seq, x_ref, wn_ref, wt_ref, lb_ref, gain_ref,
                 k_ref, ik_ref, r_ref, qt_ref, iqt_ref, vt_ref, iwt_ref,
                 hq_ref, hf_ref, hi_ref, hg_ref, *hgrn_scratch):
    xb = x_ref[...].astype(BF16)

    wd = DSA_WIDTH

    def nn(j, width=wd):
        return jnp.dot(xb, wn_ref[:, j * wd:j * wd + width], preferred_element_type=F32)

    hq_ref[...] = nn(1)
    hf_ref[...] = nn(2)
    hi_ref[...] = nn(3).astype(BF16)
    hg_ref[...] = nn(4).astype(BF16)
    k_ref[...] = nn(0).astype(BF16)
    ik_ref[...] = nn(5, 2 * IDX_DIM).astype(BF16)
    t = lax.dot_general(wt_ref[...], xb, NT_DIMS, preferred_element_type=F32)
    for j in range(PROJ_ROWS // Q_TILE):
        cols = slice(j * Q_TILE, (j + 1) * Q_TILE)
        qt_ref[j] = (t[0:wd, cols] * (LOG2E * DSA_HEAD_DIM ** -0.5)).astype(BF16)
        iqt_ref[j] = t[wd:2 * wd, cols].astype(BF16)
        for h in range(DSA_HEADS):
            lo = 2 * wd + h * DSA_HEAD_DIM
            vt_ref[j, h, :DSA_HEAD_DIM, :] = t[lo:lo + DSA_HEAD_DIM, cols].astype(BF16)
            vt_ref[j, h, DSA_HEAD_DIM:, :] = jnp.ones((V_ROWS - DSA_HEAD_DIM, Q_TILE), BF16)
        iwt_ref[j] = t[3 * wd:3 * wd + IW_ROWS, cols]

    _hgrn_block(pl.program_id(0) % blocks_per_seq == 0, hq_ref, hf_ref, hi_ref, hg_ref,
                lb_ref, gain_ref, r_ref, *hgrn_scratch)


def _project(x2, wn, wt, lb, gain, batch, seq):
    assert DSA_WIDTH == HG_WIDTH == IDX_HEADS * IDX_DIM and PROJ_ROWS == HG_ROWS
    n = x2.shape[0]
    nb = seq // PROJ_ROWS
    tiles = PROJ_ROWS // Q_TILE
    n_tiles = seq // Q_TILE
    row_spec = lambda w: pl.BlockSpec((PROJ_ROWS, w), lambda i: (i, 0))
    vec_spec = pl.BlockSpec((1, HG_WIDTH), lambda i: (0, 0))
    t_shape = lambda rows, dt: jax.ShapeDtypeStruct((batch, n_tiles, rows, Q_TILE), dt)
    t_spec = lambda rows: pl.BlockSpec((None, tiles, rows, Q_TILE), lambda i: (i // nb, i % nb, 0, 0))
    out_shape = (
        jax.ShapeDtypeStruct((n, DSA_WIDTH), BF16),
        jax.ShapeDtypeStruct((n, 2 * IDX_DIM), BF16),
        jax.ShapeDtypeStruct((n, HG_WIDTH), BF16),
        t_shape(DSA_WIDTH, BF16),
        t_shape(DSA_WIDTH, BF16),
        jax.ShapeDtypeStruct((batch, n_tiles, DSA_HEADS, V_ROWS, Q_TILE), BF16),
        t_shape(IW_ROWS, F32),
    )
    out_specs = (
        row_spec(DSA_WIDTH), row_spec(2 * IDX_DIM), row_spec(HG_WIDTH),
        t_spec(DSA_WIDTH), t_spec(DSA_WIDTH),
        pl.BlockSpec((None, tiles, DSA_HEADS, V_ROWS, Q_TILE), lambda i: (i // nb, i % nb, 0, 0, 0)),
        t_spec(IW_ROWS),
    )
    return pl.pallas_call(
        functools.partial(_proj_kernel, nb),
        grid=(n // PROJ_ROWS,),
        in_specs=[
            pl.BlockSpec((PROJ_ROWS, D_MODEL), lambda i: (i, 0)),
            pl.BlockSpec(wn.shape, lambda i: (0, 0)),
            pl.BlockSpec(wt.shape, lambda i: (0, 0)),
            vec_spec, vec_spec,
        ],
        out_specs=out_specs,
        out_shape=out_shape,
        scratch_shapes=[
            pltpu.VMEM((HG_ROWS, HG_WIDTH), F32),
            pltpu.VMEM((HG_ROWS, HG_WIDTH), F32),
            pltpu.VMEM((HG_ROWS, HG_WIDTH), BF16),
            pltpu.VMEM((HG_ROWS, HG_WIDTH), BF16),
        ] + _hgrn_scratch(),
        compiler_params=pltpu.CompilerParams(
            dimension_semantics=("arbitrary",), vmem_limit_bytes=VMEM_LIMIT_BYTES),
        name="proj_hgrn",
    )(x2, wn, wt, lb, gain)


def _fold_rows(x, op):
    return op(x.reshape(x.shape[0] // SUBLANES, SUBLANES, x.shape[1]), axis=0)


def _split3(c):
    out = []
    for _ in range(3):
        t = float(np.asarray(c, dtype=BF16))
        out.append(t)
        c = c - t
    return out


def _dsa_kernel(qt_ref, iqt_ref, iwt_ref, k_ref, vt_ref, ik_ref, o_ref,
                score_ref, coarse_ref, qm_ref, iqm_ref, pos_ref, s_ref, acc_ref, m_ref, ext_ref):
    i = pl.program_id(1)
    topk = TOPK_MAX
    idx_scale = (IDX_DIM ** -0.5) * (IDX_HEADS ** -0.5)
    slopes = [2.0 ** (-8.0 * (h + 1) / DSA_HEADS) for h in range(DSA_HEADS)]

    row_i = lax.broadcasted_iota(jnp.int32, (LANES, Q_TILE), 0)
    even_rows = row_i < DSA_HEAD_DIM
    for p in range(DSA_HEADS // 2):
        rows = slice(p * LANES, (p + 1) * LANES)
        qp = qt_ref[rows, :].astype(F32)
        iqp = iqt_ref[rows, :].astype(F32)
        qm_ref[2 * p, :LANES, :] = jnp.where(even_rows, qp, 0.0).astype(BF16)
        qm_ref[2 * p + 1, :LANES, :] = jnp.where(even_rows, 0.0, qp).astype(BF16)
        iqm_ref[2 * p] = jnp.where(even_rows, iqp, 0.0).astype(BF16)
        iqm_ref[2 * p + 1] = jnp.where(even_rows, 0.0, iqp).astype(BF16)
    for h in range(DSA_HEADS):
        c0, c1, c2 = _split3(LOG2E * slopes[h])
        coef = jnp.where(row_i == 0, c0, jnp.where(row_i == 1, c1, jnp.where(row_i == 2, c2, 0.0)))
        qm_ref[h, LANES:, :] = coef.astype(BF16)
    pos_ref[...] = jnp.where(
        lax.broadcasted_iota(jnp.int32, (KEY_TILE, LANES), 1) < 3,
        lax.broadcasted_iota(jnp.int32, (KEY_TILE, LANES), 0), 0).astype(F32).astype(BF16)

    w = iwt_ref[...]
    q_iota = lax.broadcasted_iota(jnp.int32, (1, Q_TILE), 1)
    key_iota = lax.broadcasted_iota(jnp.int32, (KEY_TILE, 1), 0)
    qpos = i * Q_TILE + q_iota
    limit = (qpos // CHUNK + 1) * CHUNK

    def score_tile(kt, carry, diagonal):
        rmax, rmin = carry
        koff = pl.multiple_of(kt * KEY_TILE, KEY_TILE)
        ik = ik_ref[pl.ds(koff, KEY_TILE), :]
        acc = jnp.zeros((KEY_TILE, Q_TILE), F32)
        for h in range(IDX_HEADS):
            s = jnp.dot(ik, iqm_ref[h], preferred_element_type=F32)
            acc = acc + w[h:h + 1, :] * jnp.maximum(s, 0.0)
        acc = acc * idx_scale
        if diagonal:
            adm = (kt * KEY_TILE + key_iota) < limit
            lo_fill = jnp.where(adm, acc, -jnp.inf)
            hi_fill = jnp.where(adm, acc, jnp.inf)
        else:
            lo_fill = hi_fill = acc
        score_ref[kt] = lo_fill
        coarse_ref[kt] = lo_fill.astype(BF16)
        return (jnp.maximum(rmax, _fold_rows(lo_fill, jnp.max)),
                jnp.minimum(rmin, _fold_rows(hi_fill, jnp.min)))

    def score_group(j, carry):
        for t in range(SCORE_GROUP):
            carry = score_tile(SCORE_GROUP * j + t, carry, False)
        return carry

    ext_ref[0], ext_ref[1] = lax.fori_loop(
        0, i // SCORE_GROUP, score_group,
        (jnp.full((SUBLANES, Q_TILE), -jnp.inf, F32), jnp.full((SUBLANES, Q_TILE), jnp.inf, F32)))

    for rem in range(SCORE_GROUP):
        @pl.when(i % SCORE_GROUP == rem)
        def _():
            carry = (ext_ref[0], ext_ref[1])
            for t in range(rem):
                carry = score_tile(i - rem + t, carry, False)
            ext_ref[0], ext_ref[1] = score_tile(i, carry, True)

    n_tiles = i + 1
    col_max = jnp.max(ext_ref[0], axis=0, keepdims=True)
    col_min = jnp.min(ext_ref[1], axis=0, keepdims=True)

    def to_key(v):
        b = lax.bitcast_convert_type(v, jnp.int32)
        return jnp.where(b < 0, b ^ 0x7FFFFFFF, b)

    def from_key(kv):
        return lax.bitcast_convert_type(jnp.where(kv < 0, kv ^ 0x7FFFFFFF, kv), F32)

    @pl.when(n_tiles < score_ref.shape[0])
    def _():
        score_ref[n_tiles] = jnp.full((KEY_TILE, Q_TILE), -jnp.inf, F32)
        coarse_ref[n_tiles] = jnp.full((KEY_TILE, Q_TILE), -jnp.inf, BF16)

    def count(pred, n_pairs=None):
        def body(j, acc):
            for kt in (2 * j, 2 * j + 1):
                acc = acc + _fold_rows(jnp.where(pred(score_ref[kt]), 1.0, 0.0), jnp.sum)
            return acc
        acc = lax.fori_loop(0, (n_tiles + 1) // 2 if n_pairs is None else n_pairs, body,
                            jnp.zeros((SUBLANES, Q_TILE), F32))
        return jnp.sum(acc, axis=0, keepdims=True)

    need = limit > topk
    any_need = jnp.sum(need.astype(jnp.int32)) > 0

    probe_pairs = jnp.where(any_need, (n_tiles + 1) // 2, 0)
    n_nonneg = count(lambda sc: sc >= 0.0, probe_pairs)
    n_pos = count(lambda sc: sc > 0.0, probe_pairs)
    above_zero = n_pos >= topk
    at_zero = jnp.logical_and(n_nonneg >= topk, n_pos < topk)
    key_min, key_max = to_key(col_min), to_key(col_max)
    lo0 = jnp.where(above_zero, 1, jnp.where(at_zero, 0, key_min))
    c_lo0 = jnp.where(above_zero, n_pos, jnp.where(at_zero, n_nonneg, limit.astype(F32)))
    hi0 = jnp.where(above_zero, key_max + 1, jnp.where(at_zero, 1, 0))
    active0 = jnp.logical_and(need, jnp.logical_not(at_zero)).astype(jnp.int32)

    def bis_cond(c):
        return c[4] > 0

    def bis_step(lo, hi, c_lo, active, it):
        key_mid = (lo & hi) + ((lo ^ hi) >> 1) + ((lo ^ hi) & 1)
        lo_f = from_key(lo)
        val_mid = to_key(lo_f + 0.5 * (from_key(hi) - lo_f))
        val_mid = jnp.minimum(jnp.maximum(val_mid, lo + 1), hi)
        mid = jnp.where(it < BISECT_VALUE_STEPS, val_mid, key_mid)
        mid_f = from_key(mid)
        cnt = count(lambda sc: sc >= mid_f)
        ge = jnp.logical_and(active > 0, cnt >= topk)
        lt = jnp.logical_and(active > 0, cnt < topk)
        lo = jnp.where(ge, mid, lo)
        c_lo = jnp.where(ge, cnt, c_lo)
        hi = jnp.where(lt, mid, hi)
        finished = jnp.logical_or(cnt == topk, hi == lo + 1)
        active = jnp.where(finished, 0, active)
        return lo, hi, c_lo, active

    def bis_body(c):
        lo, hi, c_lo, active, _, it = c
        for j in range(BISECT_STEPS_PER_CHECK):
            lo, hi, c_lo, active = bis_step(lo, hi, c_lo, active, it + j)
        return lo, hi, c_lo, active, jnp.sum(active), it + BISECT_STEPS_PER_CHECK

    bf16_step = 1 << 16
    one_b, zero_b = jnp.ones((), BF16), jnp.zeros((), BF16)
    rows_b = BF16_ROWS

    def count_coarse(t_b):
        def body(j, acc):
            for kt in (2 * j, 2 * j + 1):
                hit = jnp.where(coarse_ref[kt] >= t_b, one_b, zero_b)
                part = hit[0:rows_b, :]
                for r in range(1, KEY_TILE // rows_b):
                    part = part + hit[r * rows_b:(r + 1) * rows_b, :]
                acc = acc + part.astype(F32)
            return acc
        acc = lax.fori_loop(0, (n_tiles + 1) // 2, body, jnp.zeros((rows_b, Q_TILE), F32))
        return jnp.sum(acc, axis=0, keepdims=True)

    def coarse_step(it, c):
        lo, hi, c_lo, active = c
        lo_f = from_key(lo)
        t_b = (lo_f + 0.5 * (from_key(hi) - lo_f)).astype(BF16)
        key_t = to_key(t_b.astype(F32))
        usable = jnp.logical_and(
            active > 0, jnp.logical_and(key_t - bf16_step > lo, key_t + bf16_step < hi))
        cnt = count_coarse(t_b)
        ge = jnp.logical_and(usable, cnt >= topk)
        lt = jnp.logical_and(usable, cnt < topk)
        lo = jnp.where(ge, key_t - bf16_step, lo)
        c_lo = jnp.where(ge, topk + 1.0, c_lo)
        hi = jnp.where(lt, key_t + bf16_step, hi)
        return lo, hi, c_lo, active

    any_active = jnp.sum(active0) > 0
    n_coarse = jnp.where(any_active, BISECT_COARSE_STEPS, 0)
    n_first = jnp.where(any_active, BISECT_FIRST_STEPS, 0)
    first = lax.fori_loop(0, n_coarse, coarse_step, (lo0, hi0, c_lo0, active0))
    first = lax.fori_loop(n_coarse, n_first, lambda it, c: bis_step(*c, it), first)
    lo, _, c_lo, _, _, _ = lax.while_loop(
        bis_cond, bis_body, first + (jnp.sum(first[3]), n_first))
    thr = jnp.where(need, from_key(lo), LOWEST)

    n_tied_rows = jnp.sum(jnp.logical_and(need, c_lo != topk).astype(jnp.int32))

    @pl.when(n_tied_rows > 0)
    def _():
        quota = topk - count(lambda sc: sc > thr)
        r_i = lax.broadcasted_iota(jnp.int32, (KEY_TILE, KEY_TILE), 0)
        c_i = lax.broadcasted_iota(jnp.int32, (KEY_TILE, KEY_TILE), 1)
        strict_lower = jnp.where(r_i > c_i, 1.0, 0.0).astype(BF16)

        def demote_tile(kt, before):
            sc = score_ref[kt]
            tie = sc == thr
            tie_b = jnp.where(tie, 1.0, 0.0).astype(BF16)
            rank = before + jnp.dot(strict_lower, tie_b, preferred_element_type=F32)
            score_ref[kt] = jnp.where(jnp.logical_and(tie, rank >= quota), -jnp.inf, sc)
            return before + jnp.sum(_fold_rows(jnp.where(tie, 1.0, 0.0), jnp.sum),
                                    axis=0, keepdims=True)

        lax.fori_loop(0, n_tiles, demote_tile, jnp.zeros((1, Q_TILE), F32))

    m_ref[...] = jnp.full(m_ref.shape, -jnp.inf, F32)
    acc_ref[...] = jnp.zeros(acc_ref.shape, F32)

    def attn_tiles(tiles):
        pos = pos_ref[...]
        alphas, shifts = [], []
        for slot, (kt, diagonal) in enumerate(tiles):
            koff = pl.multiple_of(kt * KEY_TILE, KEY_TILE)
            mask_bias = jnp.where(score_ref[kt] >= thr, 0.0, -MASK_BIG)
            if diagonal:
                ahead = (2.0 * LOG2E) * jnp.maximum(key_iota - q_iota, 0).astype(F32)
            rel_q = (qpos - kt * KEY_TILE).astype(F32)
            for h in range(DSA_HEADS):
                pair = slice((h // 2) * LANES, (h // 2 + 1) * LANES)
                lhs = jnp.concatenate([k_ref[pl.ds(koff, KEY_TILE), pair], pos], axis=1)
                s = jnp.dot(lhs, qm_ref[h], preferred_element_type=F32) + mask_bias
                if diagonal:
                    s = s - slopes[h] * ahead
                s_ref[slot, h] = s
                offset = (LOG2E * slopes[h]) * rel_q
                m_old = m_ref[h:h + 1, :]
                m_new = jnp.maximum(
                    m_old, jnp.max(_fold_rows(s, jnp.max), axis=0, keepdims=True) - offset)
                m_ref[h:h + 1, :] = m_new
                alphas.append(jnp.exp2(m_old - m_new))
                shifts.append(m_new + offset)
        for slot, (kt, _) in enumerate(tiles):
            for h in range(DSA_HEADS):
                j = slot * DSA_HEADS + h
                p = jnp.exp2(s_ref[slot, h] - shifts[j]).astype(BF16)
                pv = jnp.dot(vt_ref[kt, h], p, preferred_element_type=F32)
                acc_ref[h] = alphas[j] * acc_ref[h] + pv

    def attn_body(j, carry):
        attn_tiles([(ATTN_GROUP * j + t, False) for t in range(ATTN_GROUP)])
        return carry

    lax.fori_loop(0, i // ATTN_GROUP, attn_body, 0)

    for rem in range(ATTN_GROUP):
        @pl.when(i % ATTN_GROUP == rem)
        def _():
            attn_tiles([(i - rem + t, False) for t in range(rem)] + [(i, True)])

    outs = [acc_ref[h, :DSA_HEAD_DIM, :] / acc_ref[h, DSA_HEAD_DIM:DSA_HEAD_DIM + 1, :]
            for h in range(DSA_HEADS)]
    o_ref[...] = jnp.concatenate(outs, axis=0).T.astype(o_ref.dtype)


def _dsa(qt, iqt, iwt, k3, vt, ik3, batch, seq):
    assert Q_TILE == KEY_TILE and Q_TILE % CHUNK == 0
    n = batch * seq
    nqb = seq // Q_TILE
    n_key_tiles = seq // KEY_TILE
    t_spec = lambda rows: pl.BlockSpec((None, None, rows, Q_TILE), lambda b, i: (b, i, 0, 0))
    return pl.pallas_call(
        _dsa_kernel,
        grid=(batch, nqb),
        in_specs=[
            t_spec(512), t_spec(512), t_spec(IW_ROWS),
            pl.BlockSpec((None, seq, 512), lambda b, i: (b, 0, 0)),
            pl.BlockSpec((None, n_key_tiles, DSA_HEADS, V_ROWS, KEY_TILE), lambda b, i: (b, 0, 0, 0, 0)),
            pl.BlockSpec((None, seq, 2 * IDX_DIM), lambda b, i: (b, 0, 0)),
        ],
        out_specs=pl.BlockSpec((Q_TILE, 512), lambda b, i: (b * nqb + i, 0)),
        out_shape=jax.ShapeDtypeStruct((n, 512), BF16),
        scratch_shapes=[
            pltpu.VMEM((n_key_tiles, KEY_TILE, Q_TILE), F32),
            pltpu.VMEM((n_key_tiles, KEY_TILE, Q_TILE), BF16),
            pltpu.VMEM((DSA_HEADS, 2 * LANES, Q_TILE), BF16),
            pltpu.VMEM((IDX_HEADS, LANES, Q_TILE), BF16),
            pltpu.VMEM((KEY_TILE, LANES), BF16),
            pltpu.VMEM((ATTN_GROUP, DSA_HEADS, KEY_TILE, Q_TILE), F32),
            pltpu.VMEM((DSA_HEADS, V_ROWS, Q_TILE), F32),
            pltpu.VMEM((DSA_HEADS, Q_TILE), F32),
            pltpu.VMEM((2, SUBLANES, Q_TILE), F32),
        ],
        compiler_params=pltpu.CompilerParams(
            dimension_semantics=("arbitrary", "arbitrary"), vmem_limit_bytes=VMEM_LIMIT_BYTES),
        name="dsa",
    )(qt, iqt, iwt, k3, vt, ik3)


def _sigmoid(x):
    return 0.5 * jnp.tanh(0.5 * x) + 0.5


def _hgrn_block(first_block, hq_ref, hf_ref, hi_ref, hg_ref, lb_ref, gain_ref, r_ref,
                st_ref, qs_ref, qi_ref, ki_ref, ks_ref, dec_ref, a_ref, oi_ref, kv_ref, sp_ref):
    @pl.when(first_block)
    def _():
        st_ref[...] = jnp.zeros(st_ref.shape, F32)

    n_chunks = HG_ROWS // CHUNK
    r_i = lax.broadcasted_iota(jnp.int32, (CHUNK, CHUNK), 0)
    c_i = lax.broadcasted_iota(jnp.int32, (CHUNK, CHUNK), 1)
    tril = r_i >= c_i
    tril_b = jnp.where(tril, 1.0, 0.0).astype(BF16)
    tril3_b = jnp.concatenate([tril_b, tril_b, tril_b], axis=1)
    lb = lb_ref[...]
    gain = gain_ref[...]
    chunk_rows = lambda c: slice(c * CHUNK, (c + 1) * CHUNK)
    head_cols = lambda h: slice(h * HG_DK, (h + 1) * HG_DK)

    for c in range(n_chunks):
        rows = chunk_rows(c)
        f = lb + (1.0 - lb) * _sigmoid(hf_ref[rows, :])
        logf = jnp.log2(f)
        kk = 1.0 - f
        t0 = logf.astype(BF16)
        r1 = logf - t0.astype(F32)
        t1 = r1.astype(BF16)
        t2 = (r1 - t1.astype(F32)).astype(BF16)
        g = jnp.dot(tril3_b, jnp.concatenate([t0, t1, t2], axis=0),
                    preferred_element_type=F32)
        g_last = g[CHUNK - 1:CHUNK, :]
        g_mid = g[CHUNK // 2 - 1:CHUNK // 2, :]
        q_intra = hq_ref[rows, :] * jnp.exp2(g - g_mid)
        k_intra = kk * jnp.exp2(g_mid - g)
        qi_ref[rows, :] = q_intra.astype(BF16)
        ki_ref[rows, :] = k_intra.astype(BF16)
        qs_ref[rows, :] = (q_intra * jnp.exp2(g_mid)).astype(BF16)
        ks_ref[rows, :] = (k_intra * jnp.exp2(g_last - g_mid)).astype(BF16)
        dec_ref[c:c + 1, :] = jnp.exp2(g_last)

    items = [(c, h) for c in range(n_chunks) for h in range(HG_HEADS)]
    for c, h in items:
        a = lax.dot_general(qi_ref[chunk_rows(c), head_cols(h)], ki_ref[chunk_rows(c), head_cols(h)],
                            NT_DIMS, preferred_element_type=F32)
        a_ref[c, h] = jnp.where(tril, a, 0.0).astype(BF16)
    for c, h in items:
        kv_ref[c, h] = lax.dot_general(hi_ref[chunk_rows(c), head_cols(h)],
                                       ks_ref[chunk_rows(c), head_cols(h)], TN_DIMS,
                                       preferred_element_type=F32)
    for c, h in items:
        oi_ref[chunk_rows(c), head_cols(h)] = jnp.dot(
            a_ref[c, h], hi_ref[chunk_rows(c), head_cols(h)], preferred_element_type=F32)

    for h in range(HG_HEADS):
        st = st_ref[h]
        for c in range(n_chunks):
            sp_ref[c, h] = st.astype(BF16)
            st = st * dec_ref[c:c + 1, head_cols(h)] + kv_ref[c, h]
        st_ref[h] = st

    for c in range(n_chunks):
        rows = chunk_rows(c)
        for h in range(HG_HEADS):
            sl = head_cols(h)
            o = oi_ref[rows, sl] + lax.dot_general(qs_ref[rows, sl], sp_ref[c, h], NT_DIMS,
                                                   preferred_element_type=F32)
            ms = jnp.mean(o * o, axis=1, keepdims=True)
            on = o * lax.rsqrt(ms + RMS_EPS)
            half_gate = 0.5 * hg_ref[rows, sl].astype(F32)
            silu = half_gate + half_gate * jnp.tanh(half_gate)
            r = on * gain[:, sl] * silu
            r_ref[rows, sl] = r.astype(r_ref.dtype)


def _hgrn_scratch():
    return [
        pltpu.VMEM((HG_HEADS, HG_DK, HG_DK), F32),
        pltpu.VMEM((HG_ROWS, HG_WIDTH), BF16),
        pltpu.VMEM((HG_ROWS, HG_WIDTH), BF16),
        pltpu.VMEM((HG_ROWS, HG_WIDTH), BF16),
        pltpu.VMEM((HG_ROWS, HG_WIDTH), BF16),
        pltpu.VMEM((HG_ROWS // CHUNK, HG_WIDTH), F32),
        pltpu.VMEM((HG_ROWS // CHUNK, HG_HEADS, CHUNK, CHUNK), BF16),
        pltpu.VMEM((HG_ROWS, HG_WIDTH), F32),
        pltpu.VMEM((HG_ROWS // CHUNK, HG_HEADS, HG_DK, HG_DK), F32),
        pltpu.VMEM((HG_ROWS // CHUNK, HG_HEADS, HG_DK, HG_DK), BF16),
    ]


def _layer_norm(y, g, b):
    mu = jnp.mean(y, axis=1, keepdims=True)
    d = y - mu
    var = jnp.mean(d * d, axis=1, keepdims=True)
    return d * lax.rsqrt(var + LN_EPS) * g + b


def _tail_kernel(a_ref, r_ref, x_ref, wo_ref, g1_ref, b1_ref, wg_ref, wu_ref, wd_ref,
                 g2_ref, b2_ref, o_ref):
    halves = [slice(j * (TAIL_ROWS // 2), (j + 1) * (TAIL_ROWS // 2)) for j in range(2)]
    mix = [jnp.dot(a_ref[rows, :], wo_ref[:DSA_WIDTH, :], preferred_element_type=F32)
           + jnp.dot(r_ref[rows, :], wo_ref[DSA_WIDTH:, :], preferred_element_type=F32)
           for rows in halves]
    x1, h = [], []
    for j, rows in enumerate(halves):
        x1.append(_layer_norm(ALPHA * x_ref[rows, :] + mix[j], g1_ref[...], b1_ref[...]))
        xb = x1[j].astype(BF16)
        gate = jnp.dot(xb, wg_ref[...], preferred_element_type=F32)
        up = jnp.dot(xb, wu_ref[...], preferred_element_type=F32)
        h.append((gate * jax.nn.sigmoid(gate) * up).astype(BF16))
    for j, rows in enumerate(halves):
        ff = jnp.dot(h[j], wd_ref[...], preferred_element_type=F32)
        o_ref[rows, :] = _layer_norm(ALPHA * x1[j] + ff, g2_ref[...], b2_ref[...])


def _tail(a, r, x2, wo, g1, b1, wg, wu, wd, g2, b2):
    n = x2.shape[0]
    half = pl.BlockSpec((TAIL_ROWS, 512), lambda i: (i, 0))
    full = pl.BlockSpec((TAIL_ROWS, D_MODEL), lambda i: (i, 0))
    vec = pl.BlockSpec((1, D_MODEL), lambda i: (0, 0))
    const = lambda w: pl.BlockSpec(w.shape, lambda i: (0, 0), pipeline_mode=pl.Buffered(1))
    return pl.pallas_call(
        _tail_kernel,
        grid=(n // TAIL_ROWS,),
        in_specs=[half, half, full, const(wo), vec, vec, const(wg), const(wu), const(wd), vec, vec],
        out_specs=full,
        out_shape=jax.ShapeDtypeStruct((n, D_MODEL), F32),
        compiler_params=pltpu.CompilerParams(
            dimension_semantics=("arbitrary",), vmem_limit_bytes=VMEM_LIMIT_BYTES),
        name="mix_ffn",
    )(a, r, x2, wo, g1, b1, wg, wu, wd, g2, b2)


def _proj_weights(w):
    pts = [0]
    for s in SPLIT_SIZES:
        pts.append(pts[-1] + s)
    col = lambda j: w[:, pts[j]:pts[j + 1]]
    wn = jnp.concatenate([col(1), col(6), col(7), col(8), col(9), col(4), col(4)], axis=1)
    wt = jnp.concatenate(
        [col(0), col(3), col(2), jnp.pad(col(5), ((0, 0), (0, IW_ROWS - IDX_HEADS)))], axis=1).T
    return wn.astype(BF16), wt.astype(BF16)


def kernel(x, w_in, w_out, hg_lb_logits, hg_norm_g, ln1_g, ln1_b, w_gate, w_up, w_down, ln2_g, ln2_b):
    batch, seq, _ = x.shape
    n = batch * seq
    lb_all = jnp.cumsum(jax.nn.softmax(hg_lb_logits.astype(F32), axis=0), axis=0)

    x2 = x.reshape(n, D_MODEL)
    for l in range(DEPTH):
        wn, wt = _proj_weights(w_in[l])
        k, ik, r, qt, iqt, vt, iwt = _project(
            x2, wn, wt, lb_all[l].reshape(1, HG_WIDTH),
            hg_norm_g[l].reshape(1, HG_WIDTH).astype(F32), batch, seq)
        a = _dsa(qt, iqt, iwt, k.reshape(batch, seq, DSA_WIDTH), vt,
                 ik.reshape(batch, seq, 2 * IDX_DIM), batch, seq)
        x2 = _tail(a, r, x2, w_out[l].astype(BF16), ln1_g[l].reshape(1, D_MODEL),
                   ln1_b[l].reshape(1, D_MODEL), w_gate[l].astype(BF16), w_up[l].astype(BF16),
                   w_down[l].astype(BF16), ln2_g[l].reshape(1, D_MODEL), ln2_b[l].reshape(1, D_MODEL))
    return x2.reshape(batch, seq, D_MODEL)
```

```python
import functools

import numpy as np
import jax
import jax.numpy as jnp
from jax import lax
from jax.experimental import pallas as pl
from jax.experimental.pallas import tpu as pltpu

D_MODEL = 1024
CHUNK = 64
DSA_WIDTH = 512
DSA_HEAD_DIM = 64
DSA_HEADS = 8
IDX_HEADS = 8
IDX_DIM = 64
TOPK_MAX = 256
HG_WIDTH = 512
HG_DK = 128
HG_HEADS = 4
D_FF = 2816
DEPTH = 1
ALPHA = (2.0 * DEPTH) ** 0.25
LN_EPS = 1e-5
RMS_EPS = 1e-6
SPLIT_SIZES = (512, 512, 512, 512, 64, 8, 512, 512, 512, 512)

F32 = jnp.float32
BF16 = jnp.bfloat16

LANES = 128
SUBLANES = 8
VMEM_LIMIT_BYTES = 56 * 1024 * 1024

Q_TILE = 256
KEY_TILE = 256
BF16_ROWS = 2 * SUBLANES
IW_ROWS = BF16_ROWS
PROJ_ROWS = 512
HG_ROWS = 512
TAIL_ROWS = 512
MASK_BIG = 1e30
LOWEST = -3.0e38
BISECT_VALUE_STEPS = 14
BISECT_COARSE_STEPS = 7
BISECT_FIRST_STEPS = 16
BISECT_STEPS_PER_CHECK = 2
SCORE_GROUP = 4
ATTN_GROUP = 4
LOG2E = 1.4426950408889634
V_ROWS = DSA_HEAD_DIM + BF16_ROWS

NT_DIMS = (((1,), (1,)), ((), ()))
TN_DIMS = (((0,), (0,)), ((), ()))


def _proj_kernel(blocks_per_seq, x_ref, wn_ref, wt_ref, lb_ref, gain_ref,
                 k_ref, ik_ref, r_ref, qt_ref, iqt_ref, vt_ref, iwt_ref,
                 hq_ref, hf_ref, hi_ref, hg_ref, *hgrn_scratch):
    @pl.when(pl.program_id(0) % blocks_per_seq == 0)
    def _():
        hgrn_scratch[0][...] = jnp.zeros(hgrn_scratch[0].shape, F32)

    xb = x_ref[...].astype(BF16)

    wd = DSA_WIDTH

    def nn(j, width=wd):
        return jnp.dot(xb, wn_ref[:, j * wd:j * wd + width], preferred_element_type=F32)

    hq_ref[...] = nn(1)
    hf_ref[...] = nn(2)
    hi_ref[...] = nn(3).astype(BF16)
    hg_ref[...] = nn(4).astype(BF16)
    k_ref[...] = nn(0).astype(BF16)
    ik_ref[...] = nn(5, 2 * IDX_DIM).astype(BF16)
    col_tiles = [slice(j * Q_TILE, (j + 1) * Q_TILE) for j in range(PROJ_ROWS // Q_TILE)]
    t = lax.dot_general(wt_ref[:2 * wd, :], xb, NT_DIMS, preferred_element_type=F32)
    for j, cols in enumerate(col_tiles):
        qt_ref[j] = (t[0:wd, cols] * (LOG2E * DSA_HEAD_DIM ** -0.5)).astype(BF16)
        iqt_ref[j] = t[wd:2 * wd, cols].astype(BF16)

    _hgrn_block(hq_ref, hf_ref, hi_ref, hg_ref, lb_ref, gain_ref, r_ref, *hgrn_scratch)

    t = lax.dot_general(wt_ref[2 * wd:, :], xb, NT_DIMS, preferred_element_type=F32)
    for j, cols in enumerate(col_tiles):
        for h in range(DSA_HEADS):
            lo = h * DSA_HEAD_DIM
            vt_ref[j, h, :DSA_HEAD_DIM, :] = t[lo:lo + DSA_HEAD_DIM, cols].astype(BF16)
            vt_ref[j, h, DSA_HEAD_DIM:, :] = jnp.ones((V_ROWS - DSA_HEAD_DIM, Q_TILE), BF16)
        iwt_ref[j] = t[wd:wd + IW_ROWS, cols]


def _project(x2, wn, wt, lb, gain, batch, seq):
    assert DSA_WIDTH == HG_WIDTH == IDX_HEADS * IDX_DIM and PROJ_ROWS == HG_ROWS
    n = x2.shape[0]
    nb = seq // PROJ_ROWS
    tiles = PROJ_ROWS // Q_TILE
    n_tiles = seq // Q_TILE
    row_spec = lambda w: pl.BlockSpec((PROJ_ROWS, w), lambda i: (i, 0))
    vec_spec = pl.BlockSpec((1, HG_WIDTH), lambda i: (0, 0))
    t_shape = lambda rows, dt: jax.ShapeDtypeStruct((batch, n_tiles, rows, Q_TILE), dt)
    t_spec = lambda rows: pl.BlockSpec((None, tiles, rows, Q_TILE), lambda i: (i // nb, i % nb, 0, 0))
    out_shape = (
        jax.ShapeDtypeStruct((n, DSA_WIDTH), BF16),
        jax.ShapeDtypeStruct((n, 2 * IDX_DIM), BF16),
        jax.ShapeDtypeStruct((n, HG_WIDTH), BF16),
        t_shape(DSA_WIDTH, BF16),
        t_shape(DSA_WIDTH, BF16),
        jax.ShapeDtypeStruct((batch, n_tiles, DSA_HEADS, V_ROWS, Q_TILE), BF16),
        t_shape(IW_ROWS, F32),
    )
    out_specs = (
        row_spec(DSA_WIDTH), row_spec(2 * IDX_DIM), row_spec(HG_WIDTH),
        t_spec(DSA_WIDTH), t_spec(DSA_WIDTH),
        pl.BlockSpec((None, tiles, DSA_HEADS, V_ROWS, Q_TILE), lambda i: (i // nb, i % nb, 0, 0, 0)),
        t_spec(IW_ROWS),
    )
    return pl.pallas_call(
        functools.partial(_proj_kernel, nb),
        grid=(n // PROJ_ROWS,),
        in_specs=[
            pl.BlockSpec((PROJ_ROWS, D_MODEL), lambda i: (i, 0)),
            pl.BlockSpec(wn.shape, lambda i: (0, 0)),
            pl.BlockSpec(wt.shape, lambda i: (0, 0)),
            vec_spec, vec_spec,
        ],
        out_specs=out_specs,
        out_shape=out_shape,
        scratch_shapes=[
            pltpu.VMEM((HG_ROWS, HG_WIDTH), F32),
            pltpu.VMEM((HG_ROWS, HG_WIDTH), F32),
            pltpu.VMEM((HG_ROWS, HG_WIDTH), BF16),
            pltpu.VMEM((HG_ROWS, HG_WIDTH), BF16),
        ] + _hgrn_scratch(),
        compiler_params=pltpu.CompilerParams(
            dimension_semantics=("arbitrary",), vmem_limit_bytes=VMEM_LIMIT_BYTES),
        name="proj_hgrn",
    )(x2, wn, wt, lb, gain)


def _fold_rows(x, op):
    return op(x.reshape(x.shape[0] // SUBLANES, SUBLANES, x.shape[1]), axis=0)


def _split3(c):
    out = []
    for _ in range(3):
        t = float(np.asarray(c, dtype=BF16))
        out.append(t)
        c = c - t
    return out


def _dsa_kernel(qt_ref, iqt_ref, iwt_ref, k_ref, vt_ref, ik_ref, o_ref,
                score_ref, coarse_ref, qm_ref, iqm_ref, pos_ref, s_ref, acc_ref, m_ref, ext_ref):
    i = pl.program_id(1)
    topk = TOPK_MAX
    idx_scale = (IDX_DIM ** -0.5) * (IDX_HEADS ** -0.5)
    slopes = [2.0 ** (-8.0 * (h + 1) / DSA_HEADS) for h in range(DSA_HEADS)]

    row_i = lax.broadcasted_iota(jnp.int32, (LANES, Q_TILE), 0)
    even_rows = row_i < DSA_HEAD_DIM
    for p in range(DSA_HEADS // 2):
        rows = slice(p * LANES, (p + 1) * LANES)
        qp = qt_ref[rows, :].astype(F32)
        iqp = iqt_ref[rows, :].astype(F32)
        qm_ref[2 * p, :LANES, :] = jnp.where(even_rows, qp, 0.0).astype(BF16)
        qm_ref[2 * p + 1, :LANES, :] = jnp.where(even_rows, 0.0, qp).astype(BF16)
        iqm_ref[2 * p] = jnp.where(even_rows, iqp, 0.0).astype(BF16)
        iqm_ref[2 * p + 1] = jnp.where(even_rows, 0.0, iqp).astype(BF16)
    for h in range(DSA_HEADS):
        c0, c1, c2 = _split3(LOG2E * slopes[h])
        coef = jnp.where(row_i == 0, c0, jnp.where(row_i == 1, c1, jnp.where(row_i == 2, c2, 0.0)))
        qm_ref[h, LANES:, :] = coef.astype(BF16)
    pos_ref[...] = jnp.where(
        lax.broadcasted_iota(jnp.int32, (KEY_TILE, LANES), 1) < 3,
        lax.broadcasted_iota(jnp.int32, (KEY_TILE, LANES), 0), 0).astype(F32).astype(BF16)

    w = iwt_ref[...]
    q_iota = lax.broadcasted_iota(jnp.int32, (1, Q_TILE), 1)
    key_iota = lax.broadcasted_iota(jnp.int32, (KEY_TILE, 1), 0)
    qpos = i * Q_TILE + q_iota
    limit = (qpos // CHUNK + 1) * CHUNK

    def score_tile(kt, carry, diagonal):
        rmax, rmin = carry
        koff = pl.multiple_of(kt * KEY_TILE, KEY_TILE)
        ik = ik_ref[pl.ds(koff, KEY_TILE), :]
        acc = jnp.zeros((KEY_TILE, Q_TILE), F32)
        for h in range(IDX_HEADS):
            s = jnp.dot(ik, iqm_ref[h], preferred_element_type=F32)
            acc = acc + w[h:h + 1, :] * jnp.maximum(s, 0.0)
        acc = acc * idx_scale
        if diagonal:
            adm = (kt * KEY_TILE + key_iota) < limit
            lo_fill = jnp.where(adm, acc, -jnp.inf)
            hi_fill = jnp.where(adm, acc, jnp.inf)
        else:
            lo_fill = hi_fill = acc
        score_ref[kt] = lo_fill
        coarse_ref[kt] = lo_fill.astype(BF16)
        return (jnp.maximum(rmax, _fold_rows(lo_fill, jnp.max)),
                jnp.minimum(rmin, _fold_rows(hi_fill, jnp.min)))

    def score_group(j, carry):
        for t in range(SCORE_GROUP):
            carry = score_tile(SCORE_GROUP * j + t, carry, False)
        return carry

    ext_ref[0], ext_ref[1] = lax.fori_loop(
        0, i // SCORE_GROUP, score_group,
        (jnp.full((SUBLANES, Q_TILE), -jnp.inf, F32), jnp.full((SUBLANES, Q_TILE), jnp.inf, F32)))

    for rem in range(SCORE_GROUP):
        @pl.when(i % SCORE_GROUP == rem)
        def _():
            carry = (ext_ref[0], ext_ref[1])
            for t in range(rem):
                carry = score_tile(i - rem + t, carry, False)
            ext_ref[0], ext_ref[1] = score_tile(i, carry, True)

    n_tiles = i + 1
    col_max = jnp.max(ext_ref[0], axis=0, keepdims=True)
    col_min = jnp.min(ext_ref[1], axis=0, keepdims=True)

    def to_key(v):
        b = lax.bitcast_convert_type(v, jnp.int32)
        return jnp.where(b < 0, b ^ 0x7FFFFFFF, b)

    def from_key(kv):
        return lax.bitcast_convert_type(jnp.where(kv < 0, kv ^ 0x7FFFFFFF, kv), F32)

    @pl.when(n_tiles < score_ref.shape[0])
    def _():
        score_ref[n_tiles] = jnp.full((KEY_TILE, Q_TILE), -jnp.inf, F32)
        coarse_ref[n_tiles] = jnp.full((KEY_TILE, Q_TILE), -jnp.inf, BF16)

    def count(pred, n_pairs=None):
        def body(j, acc):
            for kt in (2 * j, 2 * j + 1):
                acc = acc + _fold_rows(jnp.where(pred(score_ref[kt]), 1.0, 0.0), jnp.sum)
            return acc
        acc = lax.fori_loop(0, (n_tiles + 1) // 2 if n_pairs is None else n_pairs, body,
                            jnp.zeros((SUBLANES, Q_TILE), F32))
        return jnp.sum(acc, axis=0, keepdims=True)

    need = limit > topk
    any_need = jnp.sum(need.astype(jnp.int32)) > 0

    probe_pairs = jnp.where(any_need, (n_tiles + 1) // 2, 0)
    n_nonneg = count(lambda sc: sc >= 0.0, probe_pairs)
    n_pos = count(lambda sc: sc > 0.0, probe_pairs)
    above_zero = n_pos >= topk
    at_zero = jnp.logical_and(n_nonneg >= topk, n_pos < topk)
    key_min, key_max = to_key(col_min), to_key(col_max)
    lo0 = jnp.where(above_zero, 1, jnp.where(at_zero, 0, key_min))
    c_lo0 = jnp.where(above_zero, n_pos, jnp.where(at_zero, n_nonneg, limit.astype(F32)))
    hi0 = jnp.where(above_zero, key_max + 1, jnp.where(at_zero, 1, 0))
    active0 = jnp.logical_and(need, jnp.logical_not(at_zero)).astype(jnp.int32)

    def bis_cond(c):
        return c[4] > 0

    def bis_step(lo, hi, c_lo, active, it):
        key_mid = (lo & hi) + ((lo ^ hi) >> 1) + ((lo ^ hi) & 1)
        lo_f = from_key(lo)
        val_mid = to_key(lo_f + 0.5 * (from_key(hi) - lo_f))
        val_mid = jnp.minimum(jnp.maximum(val_mid, lo + 1), hi)
        mid = jnp.where(it < BISECT_VALUE_STEPS, val_mid, key_mid)
        mid_f = from_key(mid)
        cnt = count(lambda sc: sc >= mid_f)
        ge = jnp.logical_and(active > 0, cnt >= topk)
        lt = jnp.logical_and(active > 0, cnt < topk)
        lo = jnp.where(ge, mid, lo)
        c_lo = jnp.where(ge, cnt, c_lo)
        hi = jnp.where(lt, mid, hi)
        finished = jnp.logical_or(cnt == topk, hi == lo + 1)
        active = jnp.where(finished, 0, active)
        return lo, hi, c_lo, active

    def bis_body(c):
        lo, hi, c_lo, active, _, it = c
        for j in range(BISECT_STEPS_PER_CHECK):
            lo, hi, c_lo, active = bis_step(lo, hi, c_lo, active, it + j)
        return lo, hi, c_lo, active, jnp.sum(active), it + BISECT_STEPS_PER_CHECK

    bf16_step = 1 << 16
    one_b, zero_b = jnp.ones((), BF16), jnp.zeros((), BF16)
    rows_b = BF16_ROWS

    def count_coarse(t_b):
        def body(j, acc):
            for kt in (2 * j, 2 * j + 1):
                hit = jnp.where(coarse_ref[kt] >= t_b, one_b, zero_b)
                part = hit[0:rows_b, :]
                for r in range(1, KEY_TILE // rows_b):
                    part = part + hit[r * rows_b:(r + 1) * rows_b, :]
                acc = acc + part.astype(F32)
            return acc
        acc = lax.fori_loop(0, (n_tiles + 1) // 2, body, jnp.zeros((rows_b, Q_TILE), F32))
        return jnp.sum(acc, axis=0, keepdims=True)

    def coarse_step(it, c):
        lo, hi, c_lo, active = c
        lo_f = from_key(lo)
        t_b = (lo_f + 0.5 * (from_key(hi) - lo_f)).astype(BF16)
        key_t = to_key(t_b.astype(F32))
        usable = jnp.logical_and(
            active > 0, jnp.logical_and(key_t - bf16_step > lo, key_t + bf16_step < hi))
        cnt = count_coarse(t_b)
        ge = jnp.logical_and(usable, cnt >= topk)
        lt = jnp.logical_and(usable, cnt < topk)
        lo = jnp.where(ge, key_t - bf16_step, lo)
        c_lo = jnp.where(ge, topk + 1.0, c_lo)
        hi = jnp.where(lt, key_t + bf16_step, hi)
        return lo, hi, c_lo, active

    any_active = jnp.sum(active0) > 0
    n_coarse = jnp.where(any_active, BISECT_COARSE_STEPS, 0)
    n_first = jnp.where(any_active, BISECT_FIRST_STEPS, 0)
    first = lax.fori_loop(0, n_coarse, coarse_step, (lo0, hi0, c_lo0, active0))
    first = lax.fori_loop(n_coarse, n_first, lambda it, c: bis_step(*c, it), first)
    lo, _, c_lo, _, _, _ = lax.while_loop(
        bis_cond, bis_body, first + (jnp.sum(first[3]), n_first))
    thr = jnp.where(need, from_key(lo), LOWEST)

    n_tied_rows = jnp.sum(jnp.logical_and(need, c_lo != topk).astype(jnp.int32))

    @pl.when(n_tied_rows > 0)
    def _():
        quota = topk - count(lambda sc: sc > thr)
        r_i = lax.broadcasted_iota(jnp.int32, (KEY_TILE, KEY_TILE), 0)
        c_i = lax.broadcasted_iota(jnp.int32, (KEY_TILE, KEY_TILE), 1)
        strict_lower = jnp.where(r_i > c_i, 1.0, 0.0).astype(BF16)

        def demote_tile(kt, before):
            sc = score_ref[kt]
            tie = sc == thr
            tie_b = jnp.where(tie, 1.0, 0.0).astype(BF16)
            rank = before + jnp.dot(strict_lower, tie_b, preferred_element_type=F32)
            score_ref[kt] = jnp.where(jnp.logical_and(tie, rank >= quota), -jnp.inf, sc)
            return before + jnp.sum(_fold_rows(jnp.where(tie, 1.0, 0.0), jnp.sum),
                                    axis=0, keepdims=True)

        lax.fori_loop(0, n_tiles, demote_tile, jnp.zeros((1, Q_TILE), F32))

    m_ref[...] = jnp.full(m_ref.shape, -jnp.inf, F32)
    acc_ref[...] = jnp.zeros(acc_ref.shape, F32)

    def attn_tiles(tiles):
        pos = pos_ref[...]
        alphas, shifts = [], []
        for slot, (kt, diagonal) in enumerate(tiles):
            koff = pl.multiple_of(kt * KEY_TILE, KEY_TILE)
            mask_bias = jnp.where(score_ref[kt] >= thr, 0.0, -MASK_BIG)
            if diagonal:
                ahead = (2.0 * LOG2E) * jnp.maximum(key_iota - q_iota, 0).astype(F32)
            rel_q = (qpos - kt * KEY_TILE).astype(F32)
            for h in range(DSA_HEADS):
                pair = slice((h // 2) * LANES, (h // 2 + 1) * LANES)
                lhs = jnp.concatenate([k_ref[pl.ds(koff, KEY_TILE), pair], pos], axis=1)
                s = jnp.dot(lhs, qm_ref[h], preferred_element_type=F32) + mask_bias
                if diagonal:
                    s = s - slopes[h] * ahead
                s_ref[slot, h] = s
                offset = (LOG2E * slopes[h]) * rel_q
                m_old = m_ref[h:h + 1, :]
                m_new = jnp.maximum(
                    m_old, jnp.max(_fold_rows(s, jnp.max), axis=0, keepdims=True) - offset)
                m_ref[h:h + 1, :] = m_new
                alphas.append(jnp.exp2(m_old - m_new))
                shifts.append(m_new + offset)
        for slot, (kt, _) in enumerate(tiles):
            for h in range(DSA_HEADS):
                j = slot * DSA_HEADS + h
                p = jnp.exp2(s_ref[slot, h] - shifts[j]).astype(BF16)
                pv = jnp.dot(vt_ref[kt, h], p, preferred_element_type=F32)
                acc_ref[h] = alphas[j] * acc_ref[h] + pv

    def attn_body(j, carry):
        attn_tiles([(ATTN_GROUP * j + t, False) for t in range(ATTN_GROUP)])
        return carry

    lax.fori_loop(0, i // ATTN_GROUP, attn_body, 0)

    for rem in range(ATTN_GROUP):
        @pl.when(i % ATTN_GROUP == rem)
        def _():
            attn_tiles([(i - rem + t, False) for t in range(rem)] + [(i, True)])

    outs = [acc_ref[h, :DSA_HEAD_DIM, :] / acc_ref[h, DSA_HEAD_DIM:DSA_HEAD_DIM + 1, :]
            for h in range(DSA_HEADS)]
    o_ref[...] = jnp.concatenate(outs, axis=0).T.astype(o_ref.dtype)


def _dsa(qt, iqt, iwt, k3, vt, ik3, batch, seq):
    assert Q_TILE == KEY_TILE and Q_TILE % CHUNK == 0
    n = batch * seq
    nqb = seq // Q_TILE
    n_key_tiles = seq // KEY_TILE
    t_spec = lambda rows: pl.BlockSpec((None, None, rows, Q_TILE), lambda b, i: (b, i, 0, 0))
    return pl.pallas_call(
        _dsa_kernel,
        grid=(batch, nqb),
        in_specs=[
            t_spec(512), t_spec(512), t_spec(IW_ROWS),
            pl.BlockSpec((None, seq, 512), lambda b, i: (b, 0, 0)),
            pl.BlockSpec((None, n_key_tiles, DSA_HEADS, V_ROWS, KEY_TILE), lambda b, i: (b, 0, 0, 0, 0)),
            pl.BlockSpec((None, seq, 2 * IDX_DIM), lambda b, i: (b, 0, 0)),
        ],
        out_specs=pl.BlockSpec((Q_TILE, 512), lambda b, i: (b * nqb + i, 0)),
        out_shape=jax.ShapeDtypeStruct((n, 512), BF16),
        scratch_shapes=[
            pltpu.VMEM((n_key_tiles, KEY_TILE, Q_TILE), F32),
            pltpu.VMEM((n_key_tiles, KEY_TILE, Q_TILE), BF16),
            pltpu.VMEM((DSA_HEADS, 2 * LANES, Q_TILE), BF16),
            pltpu.VMEM((IDX_HEADS, LANES, Q_TILE), BF16),
            pltpu.VMEM((KEY_TILE, LANES), BF16),
            pltpu.VMEM((ATTN_GROUP, DSA_HEADS, KEY_TILE, Q_TILE), F32),
            pltpu.VMEM((DSA_HEADS, V_ROWS, Q_TILE), F32),
            pltpu.VMEM((DSA_HEADS, Q_TILE), F32),
            pltpu.VMEM((2, SUBLANES, Q_TILE), F32),
        ],
        compiler_params=pltpu.CompilerParams(
            dimension_semantics=("arbitrary", "arbitrary"), vmem_limit_bytes=VMEM_LIMIT_BYTES),
        name="dsa",
    )(qt, iqt, iwt, k3, vt, ik3)


def _sigmoid(x):
    return 0.5 * jnp.tanh(0.5 * x) + 0.5


def _hgrn_block(hq_ref, hf_ref, hi_ref, hg_ref, lb_ref, gain_ref, r_ref,
                st_ref, qs_ref, qi_ref, ki_ref, ks_ref, dec_ref, a_ref, oi_ref, kv_ref, sp_ref):
    n_chunks = HG_ROWS // CHUNK
    r_i = lax.broadcasted_iota(jnp.int32, (CHUNK, CHUNK), 0)
    c_i = lax.broadcasted_iota(jnp.int32, (CHUNK, CHUNK), 1)
    tril = r_i >= c_i
    tril_b = jnp.where(tril, 1.0, 0.0).astype(BF16)
    tril3_b = jnp.concatenate([tril_b, tril_b, tril_b], axis=1)
    lb = lb_ref[...]
    gain = gain_ref[...]
    chunk_rows = lambda c: slice(c * CHUNK, (c + 1) * CHUNK)
    head_cols = lambda h: slice(h * HG_DK, (h + 1) * HG_DK)

    for c in range(n_chunks):
        rows = chunk_rows(c)
        f = lb + (1.0 - lb) * _sigmoid(hf_ref[rows, :])
        logf = jnp.log2(f)
        kk = 1.0 - f
        t0 = logf.astype(BF16)
        r1 = logf - t0.astype(F32)
        t1 = r1.astype(BF16)
        t2 = (r1 - t1.astype(F32)).astype(BF16)
        g = jnp.dot(tril3_b, jnp.concatenate([t0, t1, t2], axis=0),
                    preferred_element_type=F32)
        g_last = g[CHUNK - 1:CHUNK, :]
        g_mid = g[CHUNK // 2 - 1:CHUNK // 2, :]
        q_intra = hq_ref[rows, :] * jnp.exp2(g - g_mid)
        k_intra = kk * jnp.exp2(g_mid - g)
        qi_ref[rows, :] = q_intra.astype(BF16)
        ki_ref[rows, :] = k_intra.astype(BF16)
        qs_ref[rows, :] = (q_intra * jnp.exp2(g_mid)).astype(BF16)
        ks_ref[rows, :] = (k_intra * jnp.exp2(g_last - g_mid)).astype(BF16)
        dec_ref[c:c + 1, :] = jnp.exp2(g_last)

    items = [(c, h) for c in range(n_chunks) for h in range(HG_HEADS)]
    for c, h in items:
        a = lax.dot_general(qi_ref[chunk_rows(c), head_cols(h)], ki_ref[chunk_rows(c), head_cols(h)],
                            NT_DIMS, preferred_element_type=F32)
        a_ref[c, h] = jnp.where(tril, a, 0.0).astype(BF16)
    for c, h in items:
        kv_ref[c, h] = lax.dot_general(hi_ref[chunk_rows(c), head_cols(h)],
                                       ks_ref[chunk_rows(c), head_cols(h)], TN_DIMS,
                                       preferred_element_type=F32)
    for c, h in items:
        oi_ref[chunk_rows(c), head_cols(h)] = jnp.dot(
            a_ref[c, h], hi_ref[chunk_rows(c), head_cols(h)], preferred_element_type=F32)

    for h in range(HG_HEADS):
        st = st_ref[h]
        for c in range(n_chunks):
            sp_ref[c, h] = st.astype(BF16)
            st = st * dec_ref[c:c + 1, head_cols(h)] + kv_ref[c, h]
        st_ref[h] = st

    for c in range(n_chunks):
        rows = chunk_rows(c)
        for h in range(HG_HEADS):
            sl = head_cols(h)
            o = oi_ref[rows, sl] + lax.dot_general(qs_ref[rows, sl], sp_ref[c, h], NT_DIMS,
                                                   preferred_element_type=F32)
            ms = jnp.mean(o * o, axis=1, keepdims=True)
            on = o * lax.rsqrt(ms + RMS_EPS)
            half_gate = 0.5 * hg_ref[rows, sl].astype(F32)
            silu = half_gate + half_gate * jnp.tanh(half_gate)
            r = on * gain[:, sl] * silu
            r_ref[rows, sl] = r.astype(r_ref.dtype)


def _hgrn_scratch():
    return [
        pltpu.VMEM((HG_HEADS, HG_DK, HG_DK), F32),
        pltpu.VMEM((HG_ROWS, HG_WIDTH), BF16),
        pltpu.VMEM((HG_ROWS, HG_WIDTH), BF16),
        pltpu.VMEM((HG_ROWS, HG_WIDTH), BF16),
        pltpu.VMEM((HG_ROWS, HG_WIDTH), BF16),
        pltpu.VMEM((HG_ROWS // CHUNK, HG_WIDTH), F32),
        pltpu.VMEM((HG_ROWS // CHUNK, HG_HEADS, CHUNK, CHUNK), BF16),
        pltpu.VMEM((HG_ROWS, HG_WIDTH), F32),
        pltpu.VMEM((HG_ROWS // CHUNK, HG_HEADS, HG_DK, HG_DK), F32),
        pltpu.VMEM((HG_ROWS // CHUNK, HG_HEADS, HG_DK, HG_DK), BF16),
    ]


def _layer_norm(y, g, b):
    mu = jnp.mean(y, axis=1, keepdims=True)
    d = y - mu
    var = jnp.mean(d * d, axis=1, keepdims=True)
    return d * lax.rsqrt(var + LN_EPS) * g + b


def _tail_kernel(a_ref, r_ref, x_ref, wo_ref, g1_ref, b1_ref, wg_ref, wu_ref, wd_ref,
                 g2_ref, b2_ref, o_ref):
    halves = [slice(j * (TAIL_ROWS // 2), (j + 1) * (TAIL_ROWS // 2)) for j in range(2)]
    mix = [jnp.dot(a_ref[rows, :], wo_ref[:DSA_WIDTH, :], preferred_element_type=F32)
           + jnp.dot(r_ref[rows, :], wo_ref[DSA_WIDTH:, :], preferred_element_type=F32)
           for rows in halves]
    x1, h = [], []
    for j, rows in enumerate(halves):
        x1.append(_layer_norm(ALPHA * x_ref[rows, :] + mix[j], g1_ref[...], b1_ref[...]))
        xb = x1[j].astype(BF16)
        gate = jnp.dot(xb, wg_ref[...], preferred_element_type=F32)
        up = jnp.dot(xb, wu_ref[...], preferred_element_type=F32)
        h.append((gate * jax.nn.sigmoid(gate) * up).astype(BF16))
    for j, rows in enumerate(halves):
        ff = jnp.dot(h[j], wd_ref[...], preferred_element_type=F32)
        o_ref[rows, :] = _layer_norm(ALPHA * x1[j] + ff, g2_ref[...], b2_ref[...])


def _tail(a, r, x2, wo, g1, b1, wg, wu, wd, g2, b2):
    n = x2.shape[0]
    half = pl.BlockSpec((TAIL_ROWS, 512), lambda i: (i, 0))
    full = pl.BlockSpec((TAIL_ROWS, D_MODEL), lambda i: (i, 0))
    vec = pl.BlockSpec((1, D_MODEL), lambda i: (0, 0))
    const = lambda w: pl.BlockSpec(w.shape, lambda i: (0, 0), pipeline_mode=pl.Buffered(1))
    return pl.pallas_call(
        _tail_kernel,
        grid=(n // TAIL_ROWS,),
        in_specs=[half, half, full, const(wo), vec, vec, const(wg), const(wu), const(wd), vec, vec],
        out_specs=full,
        out_shape=jax.ShapeDtypeStruct((n, D_MODEL), F32),
        compiler_params=pltpu.CompilerParams(
            dimension_semantics=("arbitrary",), vmem_limit_bytes=VMEM_LIMIT_BYTES),
        name="mix_ffn",
    )(a, r, x2, wo, g1, b1, wg, wu, wd, g2, b2)


def _proj_weights(w):
    pts = [0]
    for s in SPLIT_SIZES:
        pts.append(pts[-1] + s)
    col = lambda j: w[:, pts[j]:pts[j + 1]]
    wn = jnp.concatenate([col(1), col(6), col(7), col(8), col(9), col(4), col(4)], axis=1)
    wt = jnp.concatenate(
        [col(0), col(3), col(2), jnp.pad(col(5), ((0, 0), (0, IW_ROWS - IDX_HEADS)))], axis=1).T
    return wn.astype(BF16), wt.astype(BF16)


def kernel(x, w_in, w_out, hg_lb_logits, hg_norm_g, ln1_g, ln1_b, w_gate, w_up, w_down, ln2_g, ln2_b):
    batch, seq, _ = x.shape
    n = batch * seq
    lb_all = jnp.cumsum(jax.nn.softmax(hg_lb_logits.astype(F32), axis=0), axis=0)

    x2 = x.reshape(n, D_MODEL)
    for l in range(DEPTH):
        wn, wt = _proj_weights(w_in[l])
        k, ik, r, qt, iqt, vt, iwt = _project(
            x2, wn, wt, lb_all[l].reshape(1, HG_WIDTH),
            hg_norm_g[l].reshape(1, HG_WIDTH).astype(F32), batch, seq)
        a = _dsa(qt, iqt, iwt, k.reshape(batch, seq, DSA_WIDTH), vt,
                 ik.reshape(batch, seq, 2 * IDX_DIM), batch, seq)
        x2 = _tail(a, r, x2, w_out[l].astype(BF16), ln1_g[l].reshape(1, D_MODEL),
                   ln1_b[l].reshape(1, D_MODEL), w_gate[l].astype(BF16), w_up[l].astype(BF16),
                   w_down[l].astype(BF16), ln2_g[l].reshape(1, D_MODEL), ln2_b[l].reshape(1, D_MODEL))
    return x2.reshape(batch, seq, D_MODEL)
```

```python
import functools

import numpy as np
import jax
import jax.numpy as jnp
from jax import lax
from jax.experimental import pallas as pl
from jax.experimental.pallas import tpu as pltpu

D_MODEL = 1024
CHUNK = 64
DSA_WIDTH = 512
DSA_HEAD_DIM = 64
DSA_HEADS = 8
IDX_HEADS = 8
IDX_DIM = 64
TOPK_MAX = 256
HG_WIDTH = 512
HG_DK = 128
HG_HEADS = 4
D_FF = 2816
DEPTH = 1
ALPHA = (2.0 * DEPTH) ** 0.25
LN_EPS = 1e-5
RMS_EPS = 1e-6
SPLIT_SIZES = (512, 512, 512, 512, 64, 8, 512, 512, 512, 512)

F32 = jnp.float32
BF16 = jnp.bfloat16

LANES = 128
SUBLANES = 8
VMEM_LIMIT_BYTES = 56 * 1024 * 1024

Q_TILE = 256
KEY_TILE = 256
BF16_ROWS = 2 * SUBLANES
IW_ROWS = BF16_ROWS
PROJ_ROWS = 512
HG_ROWS = 512
TAIL_ROWS = 512
MASK_BIG = 1e30
LOWEST = -3.0e38
BISECT_VALUE_STEPS = 14
BISECT_COARSE_STEPS = 7
BISECT_FIRST_STEPS = 16
BISECT_STEPS_PER_CHECK = 2
SCORE_GROUP = 8
ATTN_GROUP = 4
LOG2E = 1.4426950408889634
V_ROWS = DSA_HEAD_DIM + BF16_ROWS

NT_DIMS = (((1,), (1,)), ((), ()))
TN_DIMS = (((0,), (0,)), ((), ()))


def _proj_kernel(blocks_per_seq, x_ref, wn_ref, wt_ref, lb_ref, gain_ref,
                 k_ref, ik_ref, r_ref, qt_ref, iqt_ref, vt_ref, iwt_ref,
                 hq_ref, hf_ref, hi_ref, hg_ref, *hgrn_scratch):
    @pl.when(pl.program_id(0) % blocks_per_seq == 0)
    def _():
        hgrn_scratch[0][...] = jnp.zeros(hgrn_scratch[0].shape, F32)

    xb = x_ref[...].astype(BF16)

    wd = DSA_WIDTH

    def nn(j, width=wd):
        return jnp.dot(xb, wn_ref[:, j * wd:j * wd + width], preferred_element_type=F32)

    hq_ref[...] = nn(1)
    hf_ref[...] = nn(2)
    hi_ref[...] = nn(3).astype(BF16)
    hg_ref[...] = nn(4).astype(BF16)
    k_ref[...] = nn(0).astype(BF16)
    ik_ref[...] = nn(5, 2 * IDX_DIM).astype(BF16)
    col_tiles = [slice(j * Q_TILE, (j + 1) * Q_TILE) for j in range(PROJ_ROWS // Q_TILE)]
    t = lax.dot_general(wt_ref[:2 * wd, :], xb, NT_DIMS, preferred_element_type=F32)
    for j, cols in enumerate(col_tiles):
        qt_ref[j] = (t[0:wd, cols] * (LOG2E * DSA_HEAD_DIM ** -0.5)).astype(BF16)
        iqt_ref[j] = t[wd:2 * wd, cols].astype(BF16)

    _hgrn_block(hq_ref, hf_ref, hi_ref, hg_ref, lb_ref, gain_ref, r_ref, *hgrn_scratch)

    t = lax.dot_general(wt_ref[2 * wd:, :], xb, NT_DIMS, preferred_element_type=F32)
    for j, cols in enumerate(col_tiles):
        for h in range(DSA_HEADS):
            lo = h * DSA_HEAD_DIM
            vt_ref[j, h, :DSA_HEAD_DIM, :] = t[lo:lo + DSA_HEAD_DIM, cols].astype(BF16)
            vt_ref[j, h, DSA_HEAD_DIM:, :] = jnp.ones((V_ROWS - DSA_HEAD_DIM, Q_TILE), BF16)
        iwt_ref[j] = t[wd:wd + IW_ROWS, cols]


def _project(x2, wn, wt, lb, gain, batch, seq):
    assert DSA_WIDTH == HG_WIDTH == IDX_HEADS * IDX_DIM and PROJ_ROWS == HG_ROWS
    n = x2.shape[0]
    nb = seq // PROJ_ROWS
    tiles = PROJ_ROWS // Q_TILE
    n_tiles = seq // Q_TILE
    row_spec = lambda w: pl.BlockSpec((PROJ_ROWS, w), lambda i: (i, 0))
    vec_spec = pl.BlockSpec((1, HG_WIDTH), lambda i: (0, 0))
    t_shape = lambda rows, dt: jax.ShapeDtypeStruct((batch, n_tiles, rows, Q_TILE), dt)
    t_spec = lambda rows: pl.BlockSpec((None, tiles, rows, Q_TILE), lambda i: (i // nb, i % nb, 0, 0))
    out_shape = (
        jax.ShapeDtypeStruct((n, DSA_WIDTH), BF16),
        jax.ShapeDtypeStruct((n, 2 * IDX_DIM), BF16),
        jax.ShapeDtypeStruct((n, HG_WIDTH), BF16),
        t_shape(DSA_WIDTH, BF16),
        t_shape(DSA_WIDTH, BF16),
        jax.ShapeDtypeStruct((batch, n_tiles, DSA_HEADS, V_ROWS, Q_TILE), BF16),
        t_shape(IW_ROWS, F32),
    )
    out_specs = (
        row_spec(DSA_WIDTH), row_spec(2 * IDX_DIM), row_spec(HG_WIDTH),
        t_spec(DSA_WIDTH), t_spec(DSA_WIDTH),
        pl.BlockSpec((None, tiles, DSA_HEADS, V_ROWS, Q_TILE), lambda i: (i // nb, i % nb, 0, 0, 0)),
        t_spec(IW_ROWS),
    )
    return pl.pallas_call(
        functools.partial(_proj_kernel, nb),
        grid=(n // PROJ_ROWS,),
        in_specs=[
            pl.BlockSpec((PROJ_ROWS, D_MODEL), lambda i: (i, 0)),
            pl.BlockSpec(wn.shape, lambda i: (0, 0)),
            pl.BlockSpec(wt.shape, lambda i: (0, 0)),
            vec_spec, vec_spec,
        ],
        out_specs=out_specs,
        out_shape=out_shape,
        scratch_shapes=[
            pltpu.VMEM((HG_ROWS, HG_WIDTH), F32),
            pltpu.VMEM((HG_ROWS, HG_WIDTH), F32),
            pltpu.VMEM((HG_ROWS, HG_WIDTH), BF16),
            pltpu.VMEM((HG_ROWS, HG_WIDTH), BF16),
        ] + _hgrn_scratch(),
        compiler_params=pltpu.CompilerParams(
            dimension_semantics=("arbitrary",), vmem_limit_bytes=VMEM_LIMIT_BYTES),
        name="proj_hgrn",
    )(x2, wn, wt, lb, gain)


def _fold_rows(x, op):
    return op(x.reshape(x.shape[0] // SUBLANES, SUBLANES, x.shape[1]), axis=0)


def _split3(c):
    out = []
    for _ in range(3):
        t = float(np.asarray(c, dtype=BF16))
        out.append(t)
        c = c - t
    return out


def _dsa_kernel(qt_ref, iqt_ref, iwt_ref, k_ref, vt_ref, ik_ref, o_ref,
                score_ref, coarse_ref, qm_ref, iqm_ref, pos_ref, s_ref, acc_ref, m_ref, ext_ref):
    i = pl.program_id(1)
    topk = TOPK_MAX
    idx_scale = (IDX_DIM ** -0.5) * (IDX_HEADS ** -0.5)
    slopes = [2.0 ** (-8.0 * (h + 1) / DSA_HEADS) for h in range(DSA_HEADS)]

    row_i = lax.broadcasted_iota(jnp.int32, (LANES, Q_TILE), 0)
    even_rows = row_i < DSA_HEAD_DIM
    for p in range(DSA_HEADS // 2):
        rows = slice(p * LANES, (p + 1) * LANES)
        qp = qt_ref[rows, :].astype(F32)
        iqp = iqt_ref[rows, :].astype(F32)
        qm_ref[2 * p, :LANES, :] = jnp.where(even_rows, qp, 0.0).astype(BF16)
        qm_ref[2 * p + 1, :LANES, :] = jnp.where(even_rows, 0.0, qp).astype(BF16)
        iqm_ref[2 * p] = jnp.where(even_rows, iqp, 0.0).astype(BF16)
        iqm_ref[2 * p + 1] = jnp.where(even_rows, 0.0, iqp).astype(BF16)
    for h in range(DSA_HEADS):
        c0, c1, c2 = _split3(LOG2E * slopes[h])
        coef = jnp.where(row_i == 0, c0, jnp.where(row_i == 1, c1, jnp.where(row_i == 2, c2, 0.0)))
        qm_ref[h, LANES:, :] = coef.astype(BF16)
    pos_ref[...] = jnp.where(
        lax.broadcasted_iota(jnp.int32, (KEY_TILE, LANES), 1) < 3,
        lax.broadcasted_iota(jnp.int32, (KEY_TILE, LANES), 0), 0).astype(F32).astype(BF16)

    w = iwt_ref[...]
    q_iota = lax.broadcasted_iota(jnp.int32, (1, Q_TILE), 1)
    key_iota = lax.broadcasted_iota(jnp.int32, (KEY_TILE, 1), 0)
    qpos = i * Q_TILE + q_iota
    limit = (qpos // CHUNK + 1) * CHUNK

    def score_tile(kt, carry, diagonal):
        rmax, rmin = carry
        koff = pl.multiple_of(kt * KEY_TILE, KEY_TILE)
        ik = ik_ref[pl.ds(koff, KEY_TILE), :]
        acc = jnp.zeros((KEY_TILE, Q_TILE), F32)
        for h in range(IDX_HEADS):
            s = jnp.dot(ik, iqm_ref[h], preferred_element_type=F32)
            acc = acc + w[h:h + 1, :] * jnp.maximum(s, 0.0)
        acc = acc * idx_scale
        if diagonal:
            adm = (kt * KEY_TILE + key_iota) < limit
            lo_fill = jnp.where(adm, acc, -jnp.inf)
            hi_fill = jnp.where(adm, acc, jnp.inf)
        else:
            lo_fill = hi_fill = acc
        score_ref[kt] = lo_fill
        coarse_ref[kt] = lo_fill.astype(BF16)
        return (jnp.maximum(rmax, _fold_rows(lo_fill, jnp.max)),
                jnp.minimum(rmin, _fold_rows(hi_fill, jnp.min)))

    def score_group(j, carry):
        for t in range(SCORE_GROUP):
            carry = score_tile(SCORE_GROUP * j + t, carry, False)
        return carry

    ext_ref[0], ext_ref[1] = lax.fori_loop(
        0, i // SCORE_GROUP, score_group,
        (jnp.full((SUBLANES, Q_TILE), -jnp.inf, F32), jnp.full((SUBLANES, Q_TILE), jnp.inf, F32)))

    for rem in range(SCORE_GROUP):
        @pl.when(i % SCORE_GROUP == rem)
        def _():
            carry = (ext_ref[0], ext_ref[1])
            for t in range(rem):
                carry = score_tile(i - rem + t, carry, False)
            ext_ref[0], ext_ref[1] = score_tile(i, carry, True)

    n_tiles = i + 1
    col_max = jnp.max(ext_ref[0], axis=0, keepdims=True)
    col_min = jnp.min(ext_ref[1], axis=0, keepdims=True)

    def to_key(v):
        b = lax.bitcast_convert_type(v, jnp.int32)
        return jnp.where(b < 0, b ^ 0x7FFFFFFF, b)

    def from_key(kv):
        return lax.bitcast_convert_type(jnp.where(kv < 0, kv ^ 0x7FFFFFFF, kv), F32)

    @pl.when(n_tiles < score_ref.shape[0])
    def _():
        score_ref[n_tiles] = jnp.full((KEY_TILE, Q_TILE), -jnp.inf, F32)
        coarse_ref[n_tiles] = jnp.full((KEY_TILE, Q_TILE), -jnp.inf, BF16)

    def count(pred, n_pairs=None):
        def body(j, acc):
            for kt in (2 * j, 2 * j + 1):
                acc = acc + _fold_rows(jnp.where(pred(score_ref[kt]), 1.0, 0.0), jnp.sum)
            return acc
        acc = lax.fori_loop(0, (n_tiles + 1) // 2 if n_pairs is None else n_pairs, body,
                            jnp.zeros((SUBLANES, Q_TILE), F32))
        return jnp.sum(acc, axis=0, keepdims=True)

    need = limit > topk
    any_need = jnp.sum(need.astype(jnp.int32)) > 0

    probe_pairs = jnp.where(any_need, (n_tiles + 1) // 2, 0)
    n_nonneg = count(lambda sc: sc >= 0.0, probe_pairs)
    n_pos = count(lambda sc: sc > 0.0, probe_pairs)
    above_zero = n_pos >= topk
    at_zero = jnp.logical_and(n_nonneg >= topk, n_pos < topk)
    key_min, key_max = to_key(col_min), to_key(col_max)
    lo0 = jnp.where(above_zero, 1, jnp.where(at_zero, 0, key_min))
    c_lo0 = jnp.where(above_zero, n_pos, jnp.where(at_zero, n_nonneg, limit.astype(F32)))
    hi0 = jnp.where(above_zero, key_max + 1, jnp.where(at_zero, 1, 0))
    active0 = jnp.logical_and(need, jnp.logical_not(at_zero)).astype(jnp.int32)

    def bis_cond(c):
        return c[4] > 0

    def bis_step(lo, hi, c_lo, active, it):
        key_mid = (lo & hi) + ((lo ^ hi) >> 1) + ((lo ^ hi) & 1)
        lo_f = from_key(lo)
        val_mid = to_key(lo_f + 0.5 * (from_key(hi) - lo_f))
        val_mid = jnp.minimum(jnp.maximum(val_mid, lo + 1), hi)
        mid = jnp.where(it < BISECT_VALUE_STEPS, val_mid, key_mid)
        mid_f = from_key(mid)
        cnt = count(lambda sc: sc >= mid_f)
        ge = jnp.logical_and(active > 0, cnt >= topk)
        lt = jnp.logical_and(active > 0, cnt < topk)
        lo = jnp.where(ge, mid, lo)
        c_lo = jnp.where(ge, cnt, c_lo)
        hi = jnp.where(lt, mid, hi)
        finished = jnp.logical_or(cnt == topk, hi == lo + 1)
        active = jnp.where(finished, 0, active)
        return lo, hi, c_lo, active

    def bis_body(c):
        lo, hi, c_lo, active, _, it = c
        for j in range(BISECT_STEPS_PER_CHECK):
            lo, hi, c_lo, active = bis_step(lo, hi, c_lo, active, it + j)
        return lo, hi, c_lo, active, jnp.sum(active), it + BISECT_STEPS_PER_CHECK

    bf16_step = 1 << 16
    one_b, zero_b = jnp.ones((), BF16), jnp.zeros((), BF16)
    rows_b = BF16_ROWS

    def count_coarse(t_b):
        def body(j, acc):
            for kt in (2 * j, 2 * j + 1):
                hit = jnp.where(coarse_ref[kt] >= t_b, one_b, zero_b)
                part = hit[0:rows_b, :]
                for r in range(1, KEY_TILE // rows_b):
                    part = part + hit[r * rows_b:(r + 1) * rows_b, :]
                acc = acc + part.astype(F32)
            return acc
        acc = lax.fori_loop(0, (n_tiles + 1) // 2, body, jnp.zeros((rows_b, Q_TILE), F32))
        return jnp.sum(acc, axis=0, keepdims=True)

    def coarse_step(it, c):
        lo, hi, c_lo, active = c
        lo_f = from_key(lo)
        t_b = (lo_f + 0.5 * (from_key(hi) - lo_f)).astype(BF16)
        key_t = to_key(t_b.astype(F32))
        usable = jnp.logical_and(
            active > 0, jnp.logical_and(key_t - bf16_step > lo, key_t + bf16_step < hi))
        cnt = count_coarse(t_b)
        ge = jnp.logical_and(usable, cnt >= topk)
        lt = jnp.logical_and(usable, cnt < topk)
        lo = jnp.where(ge, key_t - bf16_step, lo)
        c_lo = jnp.where(ge, topk + 1.0, c_lo)
        hi = jnp.where(lt, key_t + bf16_step, hi)
        return lo, hi, c_lo, active

    any_active = jnp.sum(active0) > 0
    n_coarse = jnp.where(any_active, BISECT_COARSE_STEPS, 0)
    n_first = jnp.where(any_active, BISECT_FIRST_STEPS, 0)
    first = lax.fori_loop(0, n_coarse, coarse_step, (lo0, hi0, c_lo0, active0))
    first = lax.fori_loop(n_coarse, n_first, lambda it, c: bis_step(*c, it), first)
    lo, _, c_lo, _, _, _ = lax.while_loop(
        bis_cond, bis_body, first + (jnp.sum(first[3]), n_first))
    thr = jnp.where(need, from_key(lo), LOWEST)

    n_tied_rows = jnp.sum(jnp.logical_and(need, c_lo != topk).astype(jnp.int32))

    @pl.when(n_tied_rows > 0)
    def _():
        quota = topk - count(lambda sc: sc > thr)
        r_i = lax.broadcasted_iota(jnp.int32, (KEY_TILE, KEY_TILE), 0)
        c_i = lax.broadcasted_iota(jnp.int32, (KEY_TILE, KEY_TILE), 1)
        strict_lower = jnp.where(r_i > c_i, 1.0, 0.0).astype(BF16)

        def demote_tile(kt, before):
            sc = score_ref[kt]
            tie = sc == thr
            tie_b = jnp.where(tie, 1.0, 0.0).astype(BF16)
            rank = before + jnp.dot(strict_lower, tie_b, preferred_element_type=F32)
            score_ref[kt] = jnp.where(jnp.logical_and(tie, rank >= quota), -jnp.inf, sc)
            return before + jnp.sum(_fold_rows(jnp.where(tie, 1.0, 0.0), jnp.sum),
                                    axis=0, keepdims=True)

        lax.fori_loop(0, n_tiles, demote_tile, jnp.zeros((1, Q_TILE), F32))

    m_ref[...] = jnp.full(m_ref.shape, -jnp.inf, F32)
    acc_ref[...] = jnp.zeros(acc_ref.shape, F32)

    def attn_tiles(tiles):
        pos = pos_ref[...]
        alphas, shifts = [], []
        for slot, (kt, diagonal) in enumerate(tiles):
            koff = pl.multiple_of(kt * KEY_TILE, KEY_TILE)
            mask_bias = jnp.where(score_ref[kt] >= thr, 0.0, -MASK_BIG)
            if diagonal:
                ahead = (2.0 * LOG2E) * jnp.maximum(key_iota - q_iota, 0).astype(F32)
            rel_q = (qpos - kt * KEY_TILE).astype(F32)
            for h in range(DSA_HEADS):
                pair = slice((h // 2) * LANES, (h // 2 + 1) * LANES)
                lhs = jnp.concatenate([k_ref[pl.ds(koff, KEY_TILE), pair], pos], axis=1)
                s = jnp.dot(lhs, qm_ref[h], preferred_element_type=F32) + mask_bias
                if diagonal:
                    s = s - slopes[h] * ahead
                s_ref[slot, h] = s
                offset = (LOG2E * slopes[h]) * rel_q
                m_old = m_ref[h:h + 1, :]
                m_new = jnp.maximum(
                    m_old, jnp.max(_fold_rows(s, jnp.max), axis=0, keepdims=True) - offset)
                m_ref[h:h + 1, :] = m_new
                alphas.append(jnp.exp2(m_old - m_new))
                shifts.append(m_new + offset)
        for slot, (kt, _) in enumerate(tiles):
            for h in range(DSA_HEADS):
                j = slot * DSA_HEADS + h
                p = jnp.exp2(s_ref[slot, h] - shifts[j]).astype(BF16)
                pv = jnp.dot(vt_ref[kt, h], p, preferred_element_type=F32)
                acc_ref[h] = alphas[j] * acc_ref[h] + pv

    def attn_body(j, carry):
        attn_tiles([(ATTN_GROUP * j + t, False) for t in range(ATTN_GROUP)])
        return carry

    lax.fori_loop(0, i // ATTN_GROUP, attn_body, 0)

    for rem in range(ATTN_GROUP):
        @pl.when(i % ATTN_GROUP == rem)
        def _():
            attn_tiles([(i - rem + t, False) for t in range(rem)] + [(i, True)])

    outs = [acc_ref[h, :DSA_HEAD_DIM, :] / acc_ref[h, DSA_HEAD_DIM:DSA_HEAD_DIM + 1, :]
            for h in range(DSA_HEADS)]
    o_ref[...] = jnp.concatenate(outs, axis=0).T.astype(o_ref.dtype)


def _dsa(qt, iqt, iwt, k3, vt, ik3, batch, seq):
    assert Q_TILE == KEY_TILE and Q_TILE % CHUNK == 0
    n = batch * seq
    nqb = seq // Q_TILE
    n_key_tiles = seq // KEY_TILE
    t_spec = lambda rows: pl.BlockSpec((None, None, rows, Q_TILE), lambda b, i: (b, i, 0, 0))
    return pl.pallas_call(
        _dsa_kernel,
        grid=(batch, nqb),
        in_specs=[
            t_spec(512), t_spec(512), t_spec(IW_ROWS),
            pl.BlockSpec((None, seq, 512), lambda b, i: (b, 0, 0)),
            pl.BlockSpec((None, n_key_tiles, DSA_HEADS, V_ROWS, KEY_TILE), lambda b, i: (b, 0, 0, 0, 0)),
            pl.BlockSpec((None, seq, 2 * IDX_DIM), lambda b, i: (b, 0, 0)),
        ],
        out_specs=pl.BlockSpec((Q_TILE, 512), lambda b, i: (b * nqb + i, 0)),
        out_shape=jax.ShapeDtypeStruct((n, 512), BF16),
        scratch_shapes=[
            pltpu.VMEM((n_key_tiles, KEY_TILE, Q_TILE), F32),
            pltpu.VMEM((n_key_tiles, KEY_TILE, Q_TILE), BF16),
            pltpu.VMEM((DSA_HEADS, 2 * LANES, Q_TILE), BF16),
            pltpu.VMEM((IDX_HEADS, LANES, Q_TILE), BF16),
            pltpu.VMEM((KEY_TILE, LANES), BF16),
            pltpu.VMEM((ATTN_GROUP, DSA_HEADS, KEY_TILE, Q_TILE), F32),
            pltpu.VMEM((DSA_HEADS, V_ROWS, Q_TILE), F32),
            pltpu.VMEM((DSA_HEADS, Q_TILE), F32),
            pltpu.VMEM((2, SUBLANES, Q_TILE), F32),
        ],
        compiler_params=pltpu.CompilerParams(
            dimension_semantics=("arbitrary", "arbitrary"), vmem_limit_bytes=VMEM_LIMIT_BYTES),
        name="dsa",
    )(qt, iqt, iwt, k3, vt, ik3)


def _sigmoid(x):
    return 0.5 * jnp.tanh(0.5 * x) + 0.5


def _hgrn_block(hq_ref, hf_ref, hi_ref, hg_ref, lb_ref, gain_ref, r_ref,
                st_ref, qs_ref, qi_ref, ki_ref, ks_ref, dec_ref, a_ref, oi_ref, kv_ref, sp_ref):
    n_chunks = HG_ROWS // CHUNK
    r_i = lax.broadcasted_iota(jnp.int32, (CHUNK, CHUNK), 0)
    c_i = lax.broadcasted_iota(jnp.int32, (CHUNK, CHUNK), 1)
    tril = r_i >= c_i
    tril_b = jnp.where(tril, 1.0, 0.0).astype(BF16)
    tril3_b = jnp.concatenate([tril_b, tril_b, tril_b], axis=1)
    lb = lb_ref[...]
    gain = gain_ref[...]
    chunk_rows = lambda c: slice(c * CHUNK, (c + 1) * CHUNK)
    head_cols = lambda h: slice(h * HG_DK, (h + 1) * HG_DK)

    for c in range(n_chunks):
        rows = chunk_rows(c)
        f = lb + (1.0 - lb) * _sigmoid(hf_ref[rows, :])
        logf = jnp.log2(f)
        kk = 1.0 - f
        t0 = logf.astype(BF16)
        r1 = logf - t0.astype(F32)
        t1 = r1.astype(BF16)
        t2 = (r1 - t1.astype(F32)).astype(BF16)
        g = jnp.dot(tril3_b, jnp.concatenate([t0, t1, t2], axis=0),
                    preferred_element_type=F32)
        g_last = g[CHUNK - 1:CHUNK, :]
        g_mid = g[CHUNK // 2 - 1:CHUNK // 2, :]
        q_intra = hq_ref[rows, :] * jnp.exp2(g - g_mid)
        k_intra = kk * jnp.exp2(g_mid - g)
        qi_ref[rows, :] = q_intra.astype(BF16)
        ki_ref[rows, :] = k_intra.astype(BF16)
        qs_ref[rows, :] = (q_intra * jnp.exp2(g_mid)).astype(BF16)
        ks_ref[rows, :] = (k_intra * jnp.exp2(g_last - g_mid)).astype(BF16)
        dec_ref[c:c + 1, :] = jnp.exp2(g_last)

    items = [(c, h) for c in range(n_chunks) for h in range(HG_HEADS)]
    for c, h in items:
        a = lax.dot_general(qi_ref[chunk_rows(c), head_cols(h)], ki_ref[chunk_rows(c), head_cols(h)],
                            NT_DIMS, preferred_element_type=F32)
        a_ref[c, h] = jnp.where(tril, a, 0.0).astype(BF16)
    for c, h in items:
        kv_ref[c, h] = lax.dot_general(hi_ref[chunk_rows(c), head_cols(h)],
                                       ks_ref[chunk_rows(c), head_cols(h)], TN_DIMS,
                                       preferred_element_type=F32)
    for c, h in items:
        oi_ref[chunk_rows(c), head_cols(h)] = jnp.dot(
            a_ref[c, h], hi_ref[chunk_rows(c), head_cols(h)], preferred_element_type=F32)

    for h in range(HG_HEADS):
        st = st_ref[h]
        for c in range(n_chunks):
            sp_ref[c, h] = st.astype(BF16)
            st = st * dec_ref[c:c + 1, head_cols(h)] + kv_ref[c, h]
        st_ref[h] = st

    for c in range(n_chunks):
        rows = chunk_rows(c)
        for h in range(HG_HEADS):
            sl = head_cols(h)
            o = oi_ref[rows, sl] + lax.dot_general(qs_ref[rows, sl], sp_ref[c, h], NT_DIMS,
                                                   preferred_element_type=F32)
            ms = jnp.mean(o * o, axis=1, keepdims=True)
            on = o * lax.rsqrt(ms + RMS_EPS)
            half_gate = 0.5 * hg_ref[rows, sl].astype(F32)
            silu = half_gate + half_gate * jnp.tanh(half_gate)
            r = on * gain[:, sl] * silu
            r_ref[rows, sl] = r.astype(r_ref.dtype)


def _hgrn_scratch():
    return [
        pltpu.VMEM((HG_HEADS, HG_DK, HG_DK), F32),
        pltpu.VMEM((HG_ROWS, HG_WIDTH), BF16),
        pltpu.VMEM((HG_ROWS, HG_WIDTH), BF16),
        pltpu.VMEM((HG_ROWS, HG_WIDTH), BF16),
        pltpu.VMEM((HG_ROWS, HG_WIDTH), BF16),
        pltpu.VMEM((HG_ROWS // CHUNK, HG_WIDTH), F32),
        pltpu.VMEM((HG_ROWS // CHUNK, HG_HEADS, CHUNK, CHUNK), BF16),
        pltpu.VMEM((HG_ROWS, HG_WIDTH), F32),
        pltpu.VMEM((HG_ROWS // CHUNK, HG_HEADS, HG_DK, HG_DK), F32),
        pltpu.VMEM((HG_ROWS // CHUNK, HG_HEADS, HG_DK, HG_DK), BF16),
    ]


def _layer_norm(y, g, b):
    mu = jnp.mean(y, axis=1, keepdims=True)
    d = y - mu
    var = jnp.mean(d * d, axis=1, keepdims=True)
    return d * lax.rsqrt(var + LN_EPS) * g + b


def _tail_kernel(a_ref, r_ref, x_ref, wo_ref, g1_ref, b1_ref, wg_ref, wu_ref, wd_ref,
                 g2_ref, b2_ref, o_ref):
    halves = [slice(j * (TAIL_ROWS // 2), (j + 1) * (TAIL_ROWS // 2)) for j in range(2)]
    mix = [jnp.dot(a_ref[rows, :], wo_ref[:DSA_WIDTH, :], preferred_element_type=F32)
           + jnp.dot(r_ref[rows, :], wo_ref[DSA_WIDTH:, :], preferred_element_type=F32)
           for rows in halves]
    x1, h = [], []
    for j, rows in enumerate(halves):
        x1.append(_layer_norm(ALPHA * x_ref[rows, :] + mix[j], g1_ref[...], b1_ref[...]))
        xb = x1[j].astype(BF16)
        gate = jnp.dot(xb, wg_ref[...], preferred_element_type=F32)
        up = jnp.dot(xb, wu_ref[...], preferred_element_type=F32)
        h.append((gate * jax.nn.sigmoid(gate) * up).astype(BF16))
    for j, rows in enumerate(halves):
        ff = jnp.dot(h[j], wd_ref[...], preferred_element_type=F32)
        o_ref[rows, :] = _layer_norm(ALPHA * x1[j] + ff, g2_ref[...], b2_ref[...])


def _tail(a, r, x2, wo, g1, b1, wg, wu, wd, g2, b2):
    n = x2.shape[0]
    half = pl.BlockSpec((TAIL_ROWS, 512), lambda i: (i, 0))
    full = pl.BlockSpec((TAIL_ROWS, D_MODEL), lambda i: (i, 0))
    vec = pl.BlockSpec((1, D_MODEL), lambda i: (0, 0))
    const = lambda w: pl.BlockSpec(w.shape, lambda i: (0, 0), pipeline_mode=pl.Buffered(1))
    return pl.pallas_call(
        _tail_kernel,
        grid=(n // TAIL_ROWS,),
        in_specs=[half, half, full, const(wo), vec, vec, const(wg), const(wu), const(wd), vec, vec],
        out_specs=full,
        out_shape=jax.ShapeDtypeStruct((n, D_MODEL), F32),
        compiler_params=pltpu.CompilerParams(
            dimension_semantics=("arbitrary",), vmem_limit_bytes=VMEM_LIMIT_BYTES),
        name="mix_ffn",
    )(a, r, x2, wo, g1, b1, wg, wu, wd, g2, b2)


def _proj_weights(w):
    pts = [0]
    for s in SPLIT_SIZES:
        pts.append(pts[-1] + s)
    col = lambda j: w[:, pts[j]:pts[j + 1]]
    wn = jnp.concatenate([col(1), col(6), col(7), col(8), col(9), col(4), col(4)], axis=1)
    wt = jnp.concatenate(
        [col(0), col(3), col(2), jnp.pad(col(5), ((0, 0), (0, IW_ROWS - IDX_HEADS)))], axis=1).T
    return wn.astype(BF16), wt.astype(BF16)


def kernel(x, w_in, w_out, hg_lb_logits, hg_norm_g, ln1_g, ln1_b, w_gate, w_up, w_down, ln2_g, ln2_b):
    batch, seq, _ = x.shape
    n = batch * seq
    lb_all = jnp.cumsum(jax.nn.softmax(hg_lb_logits.astype(F32), axis=0), axis=0)

    x2 = x.reshape(n, D_MODEL)
    for l in range(DEPTH):
        wn, wt = _proj_weights(w_in[l])
        k, ik, r, qt, iqt, vt, iwt = _project(
            x2, wn, wt, lb_all[l].reshape(1, HG_WIDTH),
            hg_norm_g[l].reshape(1, HG_WIDTH).astype(F32), batch, seq)
        a = _dsa(qt, iqt, iwt, k.reshape(batch, seq, DSA_WIDTH), vt,
                 ik.reshape(batch, seq, 2 * IDX_DIM), batch, seq)
        x2 = _tail(a, r, x2, w_out[l].astype(BF16), ln1_g[l].reshape(1, D_MODEL),
                   ln1_b[l].reshape(1, D_MODEL), w_gate[l].astype(BF16), w_up[l].astype(BF16),
                   w_down[l].astype(BF16), ln2_g[l].reshape(1, D_MODEL), ln2_b[l].reshape(1, D_MODEL))
    return x2.reshape(batch, seq, D_MODEL)
```

```python
import functools

import numpy as np
import jax
import jax.numpy as jnp
from jax import lax
from jax.experimental import pallas as pl
from jax.experimental.pallas import tpu as pltpu

D_MODEL = 1024
CHUNK = 64
DSA_WIDTH = 512
DSA_HEAD_DIM = 64
DSA_HEADS = 8
IDX_HEADS = 8
IDX_DIM = 64
TOPK_MAX = 256
HG_WIDTH = 512
HG_DK = 128
HG_HEADS = 4
D_FF = 2816
DEPTH = 1
ALPHA = (2.0 * DEPTH) ** 0.25
LN_EPS = 1e-5
RMS_EPS = 1e-6
SPLIT_SIZES = (512, 512, 512, 512, 64, 8, 512, 512, 512, 512)

F32 = jnp.float32
BF16 = jnp.bfloat16

LANES = 128
SUBLANES = 8
VMEM_LIMIT_BYTES = 56 * 1024 * 1024

Q_TILE = 256
KEY_TILE = 256
BF16_ROWS = 2 * SUBLANES
IW_ROWS = BF16_ROWS
PROJ_ROWS = 512
HG_ROWS = 512
TAIL_ROWS = 512
MASK_BIG = 1e30
LOWEST = -3.0e38
BISECT_VALUE_STEPS = 14
BISECT_COARSE_STEPS = 7
BISECT_FIRST_STEPS = 16
BISECT_STEPS_PER_CHECK = 2
SCORE_GROUP = 8
ATTN_GROUP = 8
LOG2E = 1.4426950408889634
V_ROWS = DSA_HEAD_DIM + BF16_ROWS

NT_DIMS = (((1,), (1,)), ((), ()))
TN_DIMS = (((0,), (0,)), ((), ()))


def _proj_kernel(blocks_per_seq, x_ref, wn_ref, wt_ref, lb_ref, gain_ref,
                 k_ref, ik_ref, r_ref, qt_ref, iqt_ref, vt_ref, iwt_ref,
                 hq_ref, hf_ref, hi_ref, hg_ref, *hgrn_scratch):
    @pl.when(pl.program_id(0) % blocks_per_seq == 0)
    def _():
        hgrn_scratch[0][...] = jnp.zeros(hgrn_scratch[0].shape, F32)

    xb = x_ref[...].astype(BF16)

    wd = DSA_WIDTH

    def nn(j, width=wd):
        return jnp.dot(xb, wn_ref[:, j * wd:j * wd + width], preferred_element_type=F32)

    hq_ref[...] = nn(1)
    hf_ref[...] = nn(2)
    hi_ref[...] = nn(3).astype(BF16)
    hg_ref[...] = nn(4).astype(BF16)
    k_ref[...] = nn(0).astype(BF16)
    ik_ref[...] = nn(5, 2 * IDX_DIM).astype(BF16)
    col_tiles = [slice(j * Q_TILE, (j + 1) * Q_TILE) for j in range(PROJ_ROWS // Q_TILE)]
    t = lax.dot_general(wt_ref[:2 * wd, :], xb, NT_DIMS, preferred_element_type=F32)
    for j, cols in enumerate(col_tiles):
        qt_ref[j] = (t[0:wd, cols] * (LOG2E * DSA_HEAD_DIM ** -0.5)).astype(BF16)
        iqt_ref[j] = t[wd:2 * wd, cols].astype(BF16)

    _hgrn_block(hq_ref, hf_ref, hi_ref, hg_ref, lb_ref, gain_ref, r_ref, *hgrn_scratch)

    t = lax.dot_general(wt_ref[2 * wd:, :], xb, NT_DIMS, preferred_element_type=F32)
    for j, cols in enumerate(col_tiles):
        for h in range(DSA_HEADS):
            lo = h * DSA_HEAD_DIM
            vt_ref[j, h, :DSA_HEAD_DIM, :] = t[lo:lo + DSA_HEAD_DIM, cols].astype(BF16)
            vt_ref[j, h, DSA_HEAD_DIM:, :] = jnp.ones((V_ROWS - DSA_HEAD_DIM, Q_TILE), BF16)
        iwt_ref[j] = t[wd:wd + IW_ROWS, cols]


def _project(x2, wn, wt, lb, gain, batch, seq):
    assert DSA_WIDTH == HG_WIDTH == IDX_HEADS * IDX_DIM and PROJ_ROWS == HG_ROWS
    n = x2.shape[0]
    nb = seq // PROJ_ROWS
    tiles = PROJ_ROWS // Q_TILE
    n_tiles = seq // Q_TILE
    row_spec = lambda w: pl.BlockSpec((PROJ_ROWS, w), lambda i: (i, 0))
    vec_spec = pl.BlockSpec((1, HG_WIDTH), lambda i: (0, 0))
    t_shape = lambda rows, dt: jax.ShapeDtypeStruct((batch, n_tiles, rows, Q_TILE), dt)
    t_spec = lambda rows: pl.BlockSpec((None, tiles, rows, Q_TILE), lambda i: (i // nb, i % nb, 0, 0))
    out_shape = (
        jax.ShapeDtypeStruct((n, DSA_WIDTH), BF16),
        jax.ShapeDtypeStruct((n, 2 * IDX_DIM), BF16),
        jax.ShapeDtypeStruct((n, HG_WIDTH), BF16),
        t_shape(DSA_WIDTH, BF16),
        t_shape(DSA_WIDTH, BF16),
        jax.ShapeDtypeStruct((batch, n_tiles, DSA_HEADS, V_ROWS, Q_TILE), BF16),
        t_shape(IW_ROWS, F32),
    )
    out_specs = (
        row_spec(DSA_WIDTH), row_spec(2 * IDX_DIM), row_spec(HG_WIDTH),
        t_spec(DSA_WIDTH), t_spec(DSA_WIDTH),
        pl.BlockSpec((None, tiles, DSA_HEADS, V_ROWS, Q_TILE), lambda i: (i // nb, i % nb, 0, 0, 0)),
        t_spec(IW_ROWS),
    )
    return pl.pallas_call(
        functools.partial(_proj_kernel, nb),
        grid=(n // PROJ_ROWS,),
        in_specs=[
            pl.BlockSpec((PROJ_ROWS, D_MODEL), lambda i: (i, 0)),
            pl.BlockSpec(wn.shape, lambda i: (0, 0)),
            pl.BlockSpec(wt.shape, lambda i: (0, 0)),
            vec_spec, vec_spec,
        ],
        out_specs=out_specs,
        out_shape=out_shape,
        scratch_shapes=[
            pltpu.VMEM((HG_ROWS, HG_WIDTH), F32),
            pltpu.VMEM((HG_ROWS, HG_WIDTH), F32),
            pltpu.VMEM((HG_ROWS, HG_WIDTH), BF16),
            pltpu.VMEM((HG_ROWS, HG_WIDTH), BF16),
        ] + _hgrn_scratch(),
        compiler_params=pltpu.CompilerParams(
            dimension_semantics=("arbitrary",), vmem_limit_bytes=VMEM_LIMIT_BYTES),
        name="proj_hgrn",
    )(x2, wn, wt, lb, gain)


def _fold_rows(x, op):
    return op(x.reshape(x.shape[0] // SUBLANES, SUBLANES, x.shape[1]), axis=0)


def _split3(c):
    out = []
    for _ in range(3):
        t = float(np.asarray(c, dtype=BF16))
        out.append(t)
        c = c - t
    return out


def _dsa_kernel(qt_ref, iqt_ref, iwt_ref, k_ref, vt_ref, ik_ref, o_ref,
                score_ref, coarse_ref, qm_ref, iqm_ref, pos_ref, s_ref, acc_ref, m_ref, ext_ref):
    i = pl.program_id(1)
    topk = TOPK_MAX
    idx_scale = (IDX_DIM ** -0.5) * (IDX_HEADS ** -0.5)
    slopes = [2.0 ** (-8.0 * (h + 1) / DSA_HEADS) for h in range(DSA_HEADS)]

    row_i = lax.broadcasted_iota(jnp.int32, (LANES, Q_TILE), 0)
    even_rows = row_i < DSA_HEAD_DIM
    for p in range(DSA_HEADS // 2):
        rows = slice(p * LANES, (p + 1) * LANES)
        qp = qt_ref[rows, :].astype(F32)
        iqp = iqt_ref[rows, :].astype(F32)
        qm_ref[2 * p, :LANES, :] = jnp.where(even_rows, qp, 0.0).astype(BF16)
        qm_ref[2 * p + 1, :LANES, :] = jnp.where(even_rows, 0.0, qp).astype(BF16)
        iqm_ref[2 * p] = jnp.where(even_rows, iqp, 0.0).astype(BF16)
        iqm_ref[2 * p + 1] = jnp.where(even_rows, 0.0, iqp).astype(BF16)
    for h in range(DSA_HEADS):
        c0, c1, c2 = _split3(LOG2E * slopes[h])
        coef = jnp.where(row_i == 0, c0, jnp.where(row_i == 1, c1, jnp.where(row_i == 2, c2, 0.0)))
        qm_ref[h, LANES:, :] = coef.astype(BF16)
    pos_ref[...] = jnp.where(
        lax.broadcasted_iota(jnp.int32, (KEY_TILE, LANES), 1) < 3,
        lax.broadcasted_iota(jnp.int32, (KEY_TILE, LANES), 0), 0).astype(F32).astype(BF16)

    w = iwt_ref[...]
    q_iota = lax.broadcasted_iota(jnp.int32, (1, Q_TILE), 1)
    key_iota = lax.broadcasted_iota(jnp.int32, (KEY_TILE, 1), 0)
    qpos = i * Q_TILE + q_iota
    limit = (qpos // CHUNK + 1) * CHUNK

    def score_tile(kt, carry, diagonal):
        rmax, rmin = carry
        koff = pl.multiple_of(kt * KEY_TILE, KEY_TILE)
        ik = ik_ref[pl.ds(koff, KEY_TILE), :]
        acc = jnp.zeros((KEY_TILE, Q_TILE), F32)
        for h in range(IDX_HEADS):
            s = jnp.dot(ik, iqm_ref[h], preferred_element_type=F32)
            acc = acc + w[h:h + 1, :] * jnp.maximum(s, 0.0)
        acc = acc * idx_scale
        if diagonal:
            adm = (kt * KEY_TILE + key_iota) < limit
            lo_fill = jnp.where(adm, acc, -jnp.inf)
            hi_fill = jnp.where(adm, acc, jnp.inf)
        else:
            lo_fill = hi_fill = acc
        score_ref[kt] = lo_fill
        coarse_ref[kt] = lo_fill.astype(BF16)
        return (jnp.maximum(rmax, _fold_rows(lo_fill, jnp.max)),
                jnp.minimum(rmin, _fold_rows(hi_fill, jnp.min)))

    def score_group(j, carry):
        for t in range(SCORE_GROUP):
            carry = score_tile(SCORE_GROUP * j + t, carry, False)
        return carry

    ext_ref[0], ext_ref[1] = lax.fori_loop(
        0, i // SCORE_GROUP, score_group,
        (jnp.full((SUBLANES, Q_TILE), -jnp.inf, F32), jnp.full((SUBLANES, Q_TILE), jnp.inf, F32)))

    for rem in range(SCORE_GROUP):
        @pl.when(i % SCORE_GROUP == rem)
        def _():
            carry = (ext_ref[0], ext_ref[1])
            for t in range(rem):
                carry = score_tile(i - rem + t, carry, False)
            ext_ref[0], ext_ref[1] = score_tile(i, carry, True)

    n_tiles = i + 1
    col_max = jnp.max(ext_ref[0], axis=0, keepdims=True)
    col_min = jnp.min(ext_ref[1], axis=0, keepdims=True)

    def to_key(v):
        b = lax.bitcast_convert_type(v, jnp.int32)
        return jnp.where(b < 0, b ^ 0x7FFFFFFF, b)

    def from_key(kv):
        return lax.bitcast_convert_type(jnp.where(kv < 0, kv ^ 0x7FFFFFFF, kv), F32)

    @pl.when(n_tiles < score_ref.shape[0])
    def _():
        score_ref[n_tiles] = jnp.full((KEY_TILE, Q_TILE), -jnp.inf, F32)
        coarse_ref[n_tiles] = jnp.full((KEY_TILE, Q_TILE), -jnp.inf, BF16)

    def count(pred, n_pairs=None):
        def body(j, acc):
            for kt in (2 * j, 2 * j + 1):
                acc = acc + _fold_rows(jnp.where(pred(score_ref[kt]), 1.0, 0.0), jnp.sum)
            return acc
        acc = lax.fori_loop(0, (n_tiles + 1) // 2 if n_pairs is None else n_pairs, body,
                            jnp.zeros((SUBLANES, Q_TILE), F32))
        return jnp.sum(acc, axis=0, keepdims=True)

    need = limit > topk
    any_need = jnp.sum(need.astype(jnp.int32)) > 0

    probe_pairs = jnp.where(any_need, (n_tiles + 1) // 2, 0)
    n_nonneg = count(lambda sc: sc >= 0.0, probe_pairs)
    n_pos = count(lambda sc: sc > 0.0, probe_pairs)
    above_zero = n_pos >= topk
    at_zero = jnp.logical_and(n_nonneg >= topk, n_pos < topk)
    key_min, key_max = to_key(col_min), to_key(col_max)
    lo0 = jnp.where(above_zero, 1, jnp.where(at_zero, 0, key_min))
    c_lo0 = jnp.where(above_zero, n_pos, jnp.where(at_zero, n_nonneg, limit.astype(F32)))
    hi0 = jnp.where(above_zero, key_max + 1, jnp.where(at_zero, 1, 0))
    active0 = jnp.logical_and(need, jnp.logical_not(at_zero)).astype(jnp.int32)

    def bis_cond(c):
        return c[4] > 0

    def bis_step(lo, hi, c_lo, active, it):
        key_mid = (lo & hi) + ((lo ^ hi) >> 1) + ((lo ^ hi) & 1)
        lo_f = from_key(lo)
        val_mid = to_key(lo_f + 0.5 * (from_key(hi) - lo_f))
        val_mid = jnp.minimum(jnp.maximum(val_mid, lo + 1), hi)
        mid = jnp.where(it < BISECT_VALUE_STEPS, val_mid, key_mid)
        mid_f = from_key(mid)
        cnt = count(lambda sc: sc >= mid_f)
        ge = jnp.logical_and(active > 0, cnt >= topk)
        lt = jnp.logical_and(active > 0, cnt < topk)
        lo = jnp.where(ge, mid, lo)
        c_lo = jnp.where(ge, cnt, c_lo)
        hi = jnp.where(lt, mid, hi)
        finished = jnp.logical_or(cnt == topk, hi == lo + 1)
        active = jnp.where(finished, 0, active)
        return lo, hi, c_lo, active

    def bis_body(c):
        lo, hi, c_lo, active, _, it = c
        for j in range(BISECT_STEPS_PER_CHECK):
            lo, hi, c_lo, active = bis_step(lo, hi, c_lo, active, it + j)
        return lo, hi, c_lo, active, jnp.sum(active), it + BISECT_STEPS_PER_CHECK

    bf16_step = 1 << 16
    one_b, zero_b = jnp.ones((), BF16), jnp.zeros((), BF16)
    rows_b = BF16_ROWS

    def count_coarse(t_b):
        def body(j, acc):
            for kt in (2 * j, 2 * j + 1):
                hit = jnp.where(coarse_ref[kt] >= t_b, one_b, zero_b)
                part = hit[0:rows_b, :]
                for r in range(1, KEY_TILE // rows_b):
                    part = part + hit[r * rows_b:(r + 1) * rows_b, :]
                acc = acc + part.astype(F32)
            return acc
        acc = lax.fori_loop(0, (n_tiles + 1) // 2, body, jnp.zeros((rows_b, Q_TILE), F32))
        return jnp.sum(acc, axis=0, keepdims=True)

    def coarse_step(it, c):
        lo, hi, c_lo, active = c
        lo_f = from_key(lo)
        t_b = (lo_f + 0.5 * (from_key(hi) - lo_f)).astype(BF16)
        key_t = to_key(t_b.astype(F32))
        usable = jnp.logical_and(
            active > 0, jnp.logical_and(key_t - bf16_step > lo, key_t + bf16_step < hi))
        cnt = count_coarse(t_b)
        ge = jnp.logical_and(usable, cnt >= topk)
        lt = jnp.logical_and(usable, cnt < topk)
        lo = jnp.where(ge, key_t - bf16_step, lo)
        c_lo = jnp.where(ge, topk + 1.0, c_lo)
        hi = jnp.where(lt, key_t + bf16_step, hi)
        return lo, hi, c_lo, active

    any_active = jnp.sum(active0) > 0
    n_coarse = jnp.where(any_active, BISECT_COARSE_STEPS, 0)
    n_first = jnp.where(any_active, BISECT_FIRST_STEPS, 0)
    first = lax.fori_loop(0, n_coarse, coarse_step, (lo0, hi0, c_lo0, active0))
    first = lax.fori_loop(n_coarse, n_first, lambda it, c: bis_step(*c, it), first)
    lo, _, c_lo, _, _, _ = lax.while_loop(
        bis_cond, bis_body, first + (jnp.sum(first[3]), n_first))
    thr = jnp.where(need, from_key(lo), LOWEST)

    n_tied_rows = jnp.sum(jnp.logical_and(need, c_lo != topk).astype(jnp.int32))

    @pl.when(n_tied_rows > 0)
    def _():
        quota = topk - count(lambda sc: sc > thr)
        r_i = lax.broadcasted_iota(jnp.int32, (KEY_TILE, KEY_TILE), 0)
        c_i = lax.broadcasted_iota(jnp.int32, (KEY_TILE, KEY_TILE), 1)
        strict_lower = jnp.where(r_i > c_i, 1.0, 0.0).astype(BF16)

        def demote_tile(kt, before):
            sc = score_ref[kt]
            tie = sc == thr
            tie_b = jnp.where(tie, 1.0, 0.0).astype(BF16)
            rank = before + jnp.dot(strict_lower, tie_b, preferred_element_type=F32)
            score_ref[kt] = jnp.where(jnp.logical_and(tie, rank >= quota), -jnp.inf, sc)
            return before + jnp.sum(_fold_rows(jnp.where(tie, 1.0, 0.0), jnp.sum),
                                    axis=0, keepdims=True)

        lax.fori_loop(0, n_tiles, demote_tile, jnp.zeros((1, Q_TILE), F32))

    m_ref[...] = jnp.full(m_ref.shape, -jnp.inf, F32)
    acc_ref[...] = jnp.zeros(acc_ref.shape, F32)

    def attn_tiles(tiles):
        pos = pos_ref[...]
        alphas, shifts = [], []
        for slot, (kt, diagonal) in enumerate(tiles):
            koff = pl.multiple_of(kt * KEY_TILE, KEY_TILE)
            mask_bias = jnp.where(score_ref[kt] >= thr, 0.0, -MASK_BIG)
            if diagonal:
                ahead = (2.0 * LOG2E) * jnp.maximum(key_iota - q_iota, 0).astype(F32)
            rel_q = (qpos - kt * KEY_TILE).astype(F32)
            for h in range(DSA_HEADS):
                pair = slice((h // 2) * LANES, (h // 2 + 1) * LANES)
                lhs = jnp.concatenate([k_ref[pl.ds(koff, KEY_TILE), pair], pos], axis=1)
                s = jnp.dot(lhs, qm_ref[h], preferred_element_type=F32) + mask_bias
                if diagonal:
                    s = s - slopes[h] * ahead
                s_ref[slot, h] = s
                offset = (LOG2E * slopes[h]) * rel_q
                m_old = m_ref[h:h + 1, :]
                m_new = jnp.maximum(
                    m_old, jnp.max(_fold_rows(s, jnp.max), axis=0, keepdims=True) - offset)
                m_ref[h:h + 1, :] = m_new
                alphas.append(jnp.exp2(m_old - m_new))
                shifts.append(m_new + offset)
        for slot, (kt, _) in enumerate(tiles):
            for h in range(DSA_HEADS):
                j = slot * DSA_HEADS + h
                p = jnp.exp2(s_ref[slot, h] - shifts[j]).astype(BF16)
                pv = jnp.dot(vt_ref[kt, h], p, preferred_element_type=F32)
                acc_ref[h] = alphas[j] * acc_ref[h] + pv

    def attn_body(j, carry):
        attn_tiles([(ATTN_GROUP * j + t, False) for t in range(ATTN_GROUP)])
        return carry

    lax.fori_loop(0, i // ATTN_GROUP, attn_body, 0)

    for rem in range(ATTN_GROUP):
        @pl.when(i % ATTN_GROUP == rem)
        def _():
            attn_tiles([(i - rem + t, False) for t in range(rem)] + [(i, True)])

    outs = [acc_ref[h, :DSA_HEAD_DIM, :] / acc_ref[h, DSA_HEAD_DIM:DSA_HEAD_DIM + 1, :]
            for h in range(DSA_HEADS)]
    o_ref[...] = jnp.concatenate(outs, axis=0).T.astype(o_ref.dtype)


def _dsa(qt, iqt, iwt, k3, vt, ik3, batch, seq):
    assert Q_TILE == KEY_TILE and Q_TILE % CHUNK == 0
    n = batch * seq
    nqb = seq // Q_TILE
    n_key_tiles = seq // KEY_TILE
    t_spec = lambda rows: pl.BlockSpec((None, None, rows, Q_TILE), lambda b, i: (b, i, 0, 0))
    return pl.pallas_call(
        _dsa_kernel,
        grid=(batch, nqb),
        in_specs=[
            t_spec(512), t_spec(512), t_spec(IW_ROWS),
            pl.BlockSpec((None, seq, 512), lambda b, i: (b, 0, 0)),
            pl.BlockSpec((None, n_key_tiles, DSA_HEADS, V_ROWS, KEY_TILE), lambda b, i: (b, 0, 0, 0, 0)),
            pl.BlockSpec((None, seq, 2 * IDX_DIM), lambda b, i: (b, 0, 0)),
        ],
        out_specs=pl.BlockSpec((Q_TILE, 512), lambda b, i: (b * nqb + i, 0)),
        out_shape=jax.ShapeDtypeStruct((n, 512), BF16),
        scratch_shapes=[
            pltpu.VMEM((n_key_tiles, KEY_TILE, Q_TILE), F32),
            pltpu.VMEM((n_key_tiles, KEY_TILE, Q_TILE), BF16),
            pltpu.VMEM((DSA_HEADS, 2 * LANES, Q_TILE), BF16),
            pltpu.VMEM((IDX_HEADS, LANES, Q_TILE), BF16),
            pltpu.VMEM((KEY_TILE, LANES), BF16),
            pltpu.VMEM((ATTN_GROUP, DSA_HEADS, KEY_TILE, Q_TILE), F32),
            pltpu.VMEM((DSA_HEADS, V_ROWS, Q_TILE), F32),
            pltpu.VMEM((DSA_HEADS, Q_TILE), F32),
            pltpu.VMEM((2, SUBLANES, Q_TILE), F32),
        ],
        compiler_params=pltpu.CompilerParams(
            dimension_semantics=("arbitrary", "arbitrary"), vmem_limit_bytes=VMEM_LIMIT_BYTES),
        name="dsa",
    )(qt, iqt, iwt, k3, vt, ik3)


def _sigmoid(x):
    return 0.5 * jnp.tanh(0.5 * x) + 0.5


def _hgrn_block(hq_ref, hf_ref, hi_ref, hg_ref, lb_ref, gain_ref, r_ref,
                st_ref, qs_ref, qi_ref, ki_ref, ks_ref, dec_ref, a_ref, oi_ref, kv_ref, sp_ref):
    n_chunks = HG_ROWS // CHUNK
    r_i = lax.broadcasted_iota(jnp.int32, (CHUNK, CHUNK), 0)
    c_i = lax.broadcasted_iota(jnp.int32, (CHUNK, CHUNK), 1)
    tril = r_i >= c_i
    tril_b = jnp.where(tril, 1.0, 0.0).astype(BF16)
    tril3_b = jnp.concatenate([tril_b, tril_b, tril_b], axis=1)
    lb = lb_ref[...]
    gain = gain_ref[...]
    chunk_rows = lambda c: slice(c * CHUNK, (c + 1) * CHUNK)
    head_cols = lambda h: slice(h * HG_DK, (h + 1) * HG_DK)

    for c in range(n_chunks):
        rows = chunk_rows(c)
        f = lb + (1.0 - lb) * _sigmoid(hf_ref[rows, :])
        logf = jnp.log2(f)
        kk = 1.0 - f
        t0 = logf.astype(BF16)
        r1 = logf - t0.astype(F32)
        t1 = r1.astype(BF16)
        t2 = (r1 - t1.astype(F32)).astype(BF16)
        g = jnp.dot(tril3_b, jnp.concatenate([t0, t1, t2], axis=0),
                    preferred_element_type=F32)
        g_last = g[CHUNK - 1:CHUNK, :]
        g_mid = g[CHUNK // 2 - 1:CHUNK // 2, :]
        q_intra = hq_ref[rows, :] * jnp.exp2(g - g_mid)
        k_intra = kk * jnp.exp2(g_mid - g)
        qi_ref[rows, :] = q_intra.astype(BF16)
        ki_ref[rows, :] = k_intra.astype(BF16)
        qs_ref[rows, :] = (q_intra * jnp.exp2(g_mid)).astype(BF16)
        ks_ref[rows, :] = (k_intra * jnp.exp2(g_last - g_mid)).astype(BF16)
        dec_ref[c:c + 1, :] = jnp.exp2(g_last)

    items = [(c, h) for c in range(n_chunks) for h in range(HG_HEADS)]
    for c, h in items:
        a = lax.dot_general(qi_ref[chunk_rows(c), head_cols(h)], ki_ref[chunk_rows(c), head_cols(h)],
                            NT_DIMS, preferred_element_type=F32)
        a_ref[c, h] = jnp.where(tril, a, 0.0).astype(BF16)
    for c, h in items:
        kv_ref[c, h] = lax.dot_general(hi_ref[chunk_rows(c), head_cols(h)],
                                       ks_ref[chunk_rows(c), head_cols(h)], TN_DIMS,
                                       preferred_element_type=F32)
    for c, h in items:
        oi_ref[chunk_rows(c), head_cols(h)] = jnp.dot(
            a_ref[c, h], hi_ref[chunk_rows(c), head_cols(h)], preferred_element_type=F32)

    for h in range(HG_HEADS):
        st = st_ref[h]
        for c in range(n_chunks):
            sp_ref[c, h] = st.astype(BF16)
            st = st * dec_ref[c:c + 1, head_cols(h)] + kv_ref[c, h]
        st_ref[h] = st

    for c in range(n_chunks):
        rows = chunk_rows(c)
        for h in range(HG_HEADS):
            sl = head_cols(h)
            o = oi_ref[rows, sl] + lax.dot_general(qs_ref[rows, sl], sp_ref[c, h], NT_DIMS,
                                                   preferred_element_type=F32)
            ms = jnp.mean(o * o, axis=1, keepdims=True)
            on = o * lax.rsqrt(ms + RMS_EPS)
            half_gate = 0.5 * hg_ref[rows, sl].astype(F32)
            silu = half_gate + half_gate * jnp.tanh(half_gate)
            r = on * gain[:, sl] * silu
            r_ref[rows, sl] = r.astype(r_ref.dtype)


def _hgrn_scratch():
    return [
        pltpu.VMEM((HG_HEADS, HG_DK, HG_DK), F32),
        pltpu.VMEM((HG_ROWS, HG_WIDTH), BF16),
        pltpu.VMEM((HG_ROWS, HG_WIDTH), BF16),
        pltpu.VMEM((HG_ROWS, HG_WIDTH), BF16),
        pltpu.VMEM((HG_ROWS, HG_WIDTH), BF16),
        pltpu.VMEM((HG_ROWS // CHUNK, HG_WIDTH), F32),
        pltpu.VMEM((HG_ROWS // CHUNK, HG_HEADS, CHUNK, CHUNK), BF16),
        pltpu.VMEM((HG_ROWS, HG_WIDTH), F32),
        pltpu.VMEM((HG_ROWS // CHUNK, HG_HEADS, HG_DK, HG_DK), F32),
        pltpu.VMEM((HG_ROWS // CHUNK, HG_HEADS, HG_DK, HG_DK), BF16),
    ]


def _layer_norm(y, g, b):
    mu = jnp.mean(y, axis=1, keepdims=True)
    d = y - mu
    var = jnp.mean(d * d, axis=1, keepdims=True)
    return d * lax.rsqrt(var + LN_EPS) * g + b


def _tail_kernel(a_ref, r_ref, x_ref, wo_ref, g1_ref, b1_ref, wg_ref, wu_ref, wd_ref,
                 g2_ref, b2_ref, o_ref):
    halves = [slice(j * (TAIL_ROWS // 2), (j + 1) * (TAIL_ROWS // 2)) for j in range(2)]
    mix = [jnp.dot(a_ref[rows, :], wo_ref[:DSA_WIDTH, :], preferred_element_type=F32)
           + jnp.dot(r_ref[rows, :], wo_ref[DSA_WIDTH:, :], preferred_element_type=F32)
           for rows in halves]
    x1, h = [], []
    for j, rows in enumerate(halves):
        x1.append(_layer_norm(ALPHA * x_ref[rows, :] + mix[j], g1_ref[...], b1_ref[...]))
        xb = x1[j].astype(BF16)
        gate = jnp.dot(xb, wg_ref[...], preferred_element_type=F32)
        up = jnp.dot(xb, wu_ref[...], preferred_element_type=F32)
        h.append((gate * jax.nn.sigmoid(gate) * up).astype(BF16))
    for j, rows in enumerate(halves):
        ff = jnp.dot(h[j], wd_ref[...], preferred_element_type=F32)
        o_ref[rows, :] = _layer_norm(ALPHA * x1[j] + ff, g2_ref[...], b2_ref[...])


def _tail(a, r, x2, wo, g1, b1, wg, wu, wd, g2, b2):
    n = x2.shape[0]
    half = pl.BlockSpec((TAIL_ROWS, 512), lambda i: (i, 0))
    full = pl.BlockSpec((TAIL_ROWS, D_MODEL), lambda i: (i, 0))
    vec = pl.BlockSpec((1, D_MODEL), lambda i: (0, 0))
    const = lambda w: pl.BlockSpec(w.shape, lambda i: (0, 0), pipeline_mode=pl.Buffered(1))
    return pl.pallas_call(
        _tail_kernel,
        grid=(n // TAIL_ROWS,),
        in_specs=[half, half, full, const(wo), vec, vec, const(wg), const(wu), const(wd), vec, vec],
        out_specs=full,
        out_shape=jax.ShapeDtypeStruct((n, D_MODEL), F32),
        compiler_params=pltpu.CompilerParams(
            dimension_semantics=("arbitrary",), vmem_limit_bytes=VMEM_LIMIT_BYTES),
        name="mix_ffn",
    )(a, r, x2, wo, g1, b1, wg, wu, wd, g2, b2)


def _proj_weights(w):
    pts = [0]
    for s in SPLIT_SIZES:
        pts.append(pts[-1] + s)
    col = lambda j: w[:, pts[j]:pts[j + 1]]
    wn = jnp.concatenate([col(1), col(6), col(7), col(8), col(9), col(4), col(4)], axis=1)
    wt = jnp.concatenate(
        [col(0), col(3), col(2), jnp.pad(col(5), ((0, 0), (0, IW_ROWS - IDX_HEADS)))], axis=1).T
    return wn.astype(BF16), wt.astype(BF16)


def kernel(x, w_in, w_out, hg_lb_logits, hg_norm_g, ln1_g, ln1_b, w_gate, w_up, w_down, ln2_g, ln2_b):
    batch, seq, _ = x.shape
    n = batch * seq
    lb_all = jnp.cumsum(jax.nn.softmax(hg_lb_logits.astype(F32), axis=0), axis=0)

    x2 = x.reshape(n, D_MODEL)
    for l in range(DEPTH):
        wn, wt = _proj_weights(w_in[l])
        k, ik, r, qt, iqt, vt, iwt = _project(
            x2, wn, wt, lb_all[l].reshape(1, HG_WIDTH),
            hg_norm_g[l].reshape(1, HG_WIDTH).astype(F32), batch, seq)
        a = _dsa(qt, iqt, iwt, k.reshape(batch, seq, DSA_WIDTH), vt,
                 ik.reshape(batch, seq, 2 * IDX_DIM), batch, seq)
        x2 = _tail(a, r, x2, w_out[l].astype(BF16), ln1_g[l].reshape(1, D_MODEL),
                   ln1_b[l].reshape(1, D_MODEL), w_gate[l].astype(BF16), w_up[l].astype(BF16),
                   w_down[l].astype(BF16), ln2_g[l].reshape(1, D_MODEL), ln2_b[l].reshape(1, D_MODEL))
    return x2.reshape(batch, seq, D_MODEL)
```
